```python
import math
import jax, jax.numpy as jnp
from jax import lax
import numpy as np

D_MODEL = 2048
BATCH = 2
SEQ = 16384
DEPTH = 1
DEC_BATCH = 4
DEC_SEQ = 2048
PAST_LEN = 128

HEAD_DIM = 128
N_HEADS = D_MODEL // 256
N_KV_HEADS = N_HEADS // 4
KV_GROUP = N_HEADS // N_KV_HEADS
ATT_W = N_HEADS * HEAD_DIM
KV_W = N_KV_HEADS * HEAD_DIM
Q_BLOCK = 128
ROPE_THETA = 10000.0
ROPE_AXIS_PAIRS = HEAD_DIM // 4
GRID_W = 64
HY_W = D_MODEL // 4
HY_ORDER = 2
HY_STREAMS = HY_ORDER + 1
HY_SHORT = 3
HY_BANDS = 8
HY_EMB = 1 + 2 * HY_BANDS
HY_FILTER_HIDDEN = 64
HY_DIRS = 2
HY_DECAY_TARGET = 1e-2
HY_FAST_PCT = 0.3
HY_SLOW_PCT = 1.5
MEM_HEADS = 4
MEM_TOKENS = 256
MEM_W = MEM_HEADS * HEAD_DIM
N_BRANCH = 3
N_GROUPS = 4
EXPERTS_PER_GROUP = 8
N_EXPERTS = N_GROUPS * EXPERTS_PER_GROUP
TOP_K = 2
D_EXPERT = D_MODEL // 2
MOE_BLOCK = 128
EPS = 1e-6

OFF_K = ATT_W
OFF_V = OFF_K + KV_W
OFF_HY = OFF_V + KV_W
OFF_MQ = OFF_HY + HY_STREAMS * HY_W
OFF_GATE = OFF_MQ + MEM_W
IN_COLS = OFF_GATE + N_BRANCH * D_MODEL
IN_SPLITS = (OFF_K, OFF_V, OFF_HY, OFF_MQ, OFF_GATE)

kernel_name = "hybrid_gated_attn_hyena_hmoe_encoder"


def rms_norm(x, g):
    xf = x.astype(jnp.float32)
    y = xf * lax.rsqrt(jnp.mean(xf * xf, axis=-1, keepdims=True) + EPS) * g.astype(jnp.float32)
    return y.astype(x.dtype)


def axial_rope(L):
    rows = L // GRID_W
    row = jnp.repeat(jnp.arange(rows, dtype=jnp.float32), GRID_W)
    col = jnp.tile(jnp.arange(GRID_W, dtype=jnp.float32), rows)
    inv = jnp.power(ROPE_THETA, -jnp.arange(ROPE_AXIS_PAIRS, dtype=jnp.float32) / ROPE_AXIS_PAIRS)
    ang = jnp.concatenate([row[:, None] * inv, col[:, None] * inv], axis=-1)
    return jnp.cos(ang), jnp.sin(ang)


def apply_rope(x, cos, sin):
    xf = x.astype(jnp.float32).reshape(*x.shape[:-1], HEAD_DIM // 2, 2)
    c = cos[None, :, None, :]
    s = sin[None, :, None, :]
    xr, xi = xf[..., 0], xf[..., 1]
    out = jnp.stack([xr * c - xi * s, xr * s + xi * c], axis=-1)
    return out.reshape(x.shape).astype(x.dtype)


def grouped_attention(q, k, v):
    B, L = q.shape[:2]
    n_blk = L // Q_BLOCK
    qb = q.reshape(B, n_blk, Q_BLOCK, N_KV_HEADS, KV_GROUP, HEAD_DIM).swapaxes(0, 1)
    scale = HEAD_DIM ** -0.5

    def block(q_blk):
        s = jnp.einsum('bqkgd,bskd->bkgqs', q_blk, k, preferred_element_type=jnp.float32) * scale
        p = jax.nn.softmax(s, axis=-1).astype(v.dtype)
        return jnp.einsum('bkgqs,bskd->bqkgd', p, v)

    o = lax.map(block, qb)
    return o.swapaxes(0, 1).reshape(B, L, ATT_W)


def memory_attention(zm, mem, lp):
    B, L = zm.shape[:2]
    M = mem.shape[1]
    qm = rms_norm(zm.reshape(B, L, MEM_HEADS, HEAD_DIM), lp['mq_norm_g'])
    kv = (rms_norm(mem, lp['mem_norm_g']) @ lp['w_mem_kv']).reshape(B, M, 2, MEM_HEADS, HEAD_DIM)
    km = rms_norm(kv[:, :, 0], lp['mk_norm_g'])
    vm = kv[:, :, 1]
    s = jnp.einsum('bqhd,bmhd->bhqm', qm, km, preferred_element_type=jnp.float32) * (HEAD_DIM ** -0.5)
    p = jax.nn.softmax(s, axis=-1).astype(vm.dtype)
    return jnp.einsum('bhqm,bmhd->bqhd', p, vm).reshape(B, L, MEM_W)


def short_conv(u, w, b):
    pad = HY_SHORT // 2
    y = lax.conv_general_dilated(u, w.astype(u.dtype), window_strides=(1,), padding=((pad, pad),),
                                 dimension_numbers=('NWC', 'WIO', 'NWC'),
                                 feature_group_count=u.shape[-1])
    return y + b.astype(u.dtype)


def hyena_filter(L, lp):
    f32 = jnp.float32
    t = jnp.linspace(0.0, 1.0, L, dtype=f32)[:, None]
    w = 2.0 * math.pi * jnp.arange(L, dtype=f32) / L
    bands = jnp.linspace(1e-4, HY_BANDS - 1, HY_BANDS, dtype=f32)
    fw = w[:, None] * bands[None, :]
    emb = jnp.concatenate([t, jnp.cos(fw), -jnp.sin(fw)], axis=-1)
    freq = lp['hf_freq'].astype(f32)
    h = jnp.sin(freq * (emb @ lp['hf_w1'].astype(f32) + lp['hf_b1'].astype(f32)))
    h = jnp.sin(freq * (h @ lp['hf_w2'].astype(f32) + lp['hf_b2'].astype(f32)))
    h = h @ lp['hf_w3'].astype(f32) + lp['hf_b3'].astype(f32)
    decay = jnp.abs(lp['hy_decay'].astype(f32))
    h = h.reshape(L, HY_DIRS, HY_W) * jnp.exp(-t[:, :, None] * decay[None])
    kf = jnp.concatenate([h[:, 0], jnp.zeros((1, HY_W), f32), h[:0:-1, 1]], axis=0)
    return kf * lax.rsqrt(jnp.sum(kf * kf, axis=0, keepdims=True) + EPS)


def long_conv(u, kf, bias):
    L = u.shape[1]
    uf = jnp.fft.rfft(u.astype(jnp.float32), n=2 * L, axis=1)
    kk = jnp.fft.rfft(kf, n=2 * L, axis=0)
    y = jnp.fft.irfft(uf * kk[None], n=2 * L, axis=1)[:, :L]
    return (y + u.astype(jnp.float32) * bias.astype(jnp.float32)).astype(u.dtype)


def hyena_branch(zh, lp):
    L = zh.shape[1]
    u = short_conv(zh, lp['hy_conv_w'], lp['hy_conv_b'])
    x0, x1, v = jnp.split(u, HY_STREAMS, axis=-1)
    kf = hyena_filter(L, lp)
    v = long_conv(v * x1, kf, lp['hy_bias'])
    return v * x0


def mixer_sublayer(x, mem, lp):
    B, L, _ = x.shape
    h = rms_norm(x, lp['norm1_g'])
    z = h @ lp['w_in']
    zq, zk, zv, zh, zm, zg = jnp.split(z, IN_SPLITS, axis=-1)
    cos, sin = axial_rope(L)
    q = apply_rope(rms_norm(zq.reshape(B, L, N_HEADS, HEAD_DIM), lp['q_norm_g']), cos, sin)
    k = apply_rope(rms_norm(zk.reshape(B, L, N_KV_HEADS, HEAD_DIM), lp['k_norm_g']), cos, sin)
    v = zv.reshape(B, L, N_KV_HEADS, HEAD_DIM)
    a_out = grouped_attention(q, k, v)
    h_out = hyena_branch(zh, lp)
    m_out = memory_attention(zm, mem, lp)
    gates = jax.nn.sigmoid((zg + lp['b_gate']).astype(jnp.float32)).astype(x.dtype)
    gates = gates.reshape(B, L, N_BRANCH, D_MODEL)
    merged = (gates[:, :, 0] * (a_out @ lp['w_br_attn'])
              + gates[:, :, 1] * (h_out @ lp['w_br_hyena'])
              + gates[:, :, 2] * (m_out @ lp['w_br_mem']))
    return x + merged @ lp['w_out']


def moe_sublayer(x, lp):
    B, L, D = x.shape
    T = B * L
    h = rms_norm(x, lp['norm2_g']).reshape(T, D)
    hf = h.astype(jnp.float32)
    g_logits = hf @ lp['w_router_group'].astype(jnp.float32) + lp['b_router_group'].astype(jnp.float32)
    g_p, g_idx = lax.top_k(jax.nn.softmax(g_logits, axis=-1), 1)
    e_logits = (hf @ lp['w_router_expert'].astype(jnp.float32)
                + lp['b_router_expert'].astype(jnp.float32)).reshape(T, N_GROUPS, EXPERTS_PER_GROUP)
    idx = jnp.broadcast_to(g_idx[:, :, None], (T, 1, EXPERTS_PER_GROUP))
    e_in = jnp.take_along_axis(e_logits, idx, axis=1)[:, 0]
    top_v, top_i = lax.top_k(e_in, TOP_K)
    comb = jax.nn.softmax(top_v, axis=-1) * g_p
    e_flat = (g_idx * EXPERTS_PER_GROUP + top_i).reshape(-1).astype(jnp.int32)
    w_flat = comb.reshape(-1).astype(x.dtype)
    tok_flat = jnp.repeat(jnp.arange(T, dtype=jnp.int32), TOP_K)
    A = T * TOP_K
    n_blk = (A + N_EXPERTS * (MOE_BLOCK - 1) + MOE_BLOCK - 1) // MOE_BLOCK
    P = n_blk * MOE_BLOCK
    order = jnp.argsort(e_flat)
    e_s, tok_s, w_s = e_flat[order], tok_flat[order], w_flat[order]
    counts = jnp.bincount(e_flat, length=N_EXPERTS).astype(jnp.int32)
    starts = jnp.cumsum(counts) - counts
    padded = (counts + MOE_BLOCK - 1) // MOE_BLOCK * MOE_BLOCK
    pends = jnp.cumsum(padded)
    pstarts = pends - padded
    dest = pstarts[e_s] + jnp.arange(A, dtype=jnp.int32) - starts[e_s]
    buf_tok = jnp.full((P,), T, jnp.int32).at[dest].set(tok_s)
    buf_w = jnp.zeros((P,), x.dtype).at[dest].set(w_s)
    blk_e = jnp.minimum(jnp.searchsorted(pends, jnp.arange(n_blk, dtype=jnp.int32) * MOE_BLOCK, side='right'),
                        N_EXPERTS - 1).astype(jnp.int32)
    h_pad = jnp.concatenate([h, jnp.zeros((1, D), h.dtype)], axis=0)
    w_g, w_u, w_d = lp['w_gate_e'], lp['w_up_e'], lp['w_down_e']

    def run_block(args):
        tok, w, e = args
        xb = h_pad[tok]
        a = jax.nn.silu(xb @ w_g[e]) * (xb @ w_u[e])
        return (a @ w_d[e]) * w[:, None]

    yb = lax.map(run_block, (buf_tok.reshape(n_blk, MOE_BLOCK), buf_w.reshape(n_blk, MOE_BLOCK), blk_e))
    y = jnp.zeros((T + 1, D), x.dtype).at[buf_tok].add(yb.reshape(P, D))[:T]
    return x + y.reshape(B, L, D)


def setup_inputs(seed: int = 0) -> dict:
    key = jax.random.key(seed)
    ks = jax.random.split(key, 40)
    f32 = jnp.float32

    def nrm(k, shape, scale):
        return jax.random.normal(k, shape, f32) * scale

    def gain(k, n):
        return 1.0 + 0.05 * jax.random.normal(k, (DEPTH, n), f32)

    lo = abs(math.log(HY_DECAY_TARGET)) / HY_SLOW_PCT
    hi = abs(math.log(HY_DECAY_TARGET)) / HY_FAST_PCT
    decay_base = jnp.linspace(lo, hi, HY_W, dtype=f32)
    hy_decay = decay_base[None, None, :] * (1.0 + 0.05 * jax.random.normal(ks[18], (DEPTH, HY_DIRS, HY_W), f32))
    return {
        'x_prompt': nrm(ks[0], (BATCH, SEQ, D_MODEL), 1.0),
        'x_sample': nrm(ks[1], (DEC_BATCH, DEC_SEQ, D_MODEL), 1.0),
        'mem_prompt': nrm(ks[2], (BATCH, MEM_TOKENS, D_MODEL), 1.0),
        'mem_sample': nrm(ks[3], (DEC_BATCH, MEM_TOKENS, D_MODEL), 1.0),
        'norm1_g': gain(ks[4], D_MODEL),
        'w_in': nrm(ks[5], (DEPTH, D_MODEL, IN_COLS), D_MODEL ** -0.5),
        'b_gate': nrm(ks[6], (DEPTH, N_BRANCH * D_MODEL), 0.02),
        'q_norm_g': gain(ks[7], HEAD_DIM),
        'k_norm_g': gain(ks[8], HEAD_DIM),
        'hy_conv_w': nrm(ks[9], (DEPTH, HY_SHORT, 1, HY_STREAMS * HY_W), HY_SHORT ** -0.5),
        'hy_conv_b': nrm(ks[10], (DEPTH, HY_STREAMS * HY_W), 0.02),
        'hf_w1': nrm(ks[11], (DEPTH, HY_EMB, HY_FILTER_HIDDEN), HY_EMB ** -0.5),
        'hf_b1': nrm(ks[12], (DEPTH, HY_FILTER_HIDDEN), 0.02),
        'hf_w2': nrm(ks[13], (DEPTH, HY_FILTER_HIDDEN, HY_FILTER_HIDDEN), HY_FILTER_HIDDEN ** -0.5),
        'hf_b2': nrm(ks[14], (DEPTH, HY_FILTER_HIDDEN), 0.02),
        'hf_w3': nrm(ks[15], (DEPTH, HY_FILTER_HIDDEN, HY_DIRS * HY_W), HY_FILTER_HIDDEN ** -0.5),
        'hf_b3': nrm(ks[16], (DEPTH, HY_DIRS * HY_W), 0.02),
        'hf_freq': gain(ks[17], HY_FILTER_HIDDEN),
        'hy_decay': hy_decay,
        'hy_bias': nrm(ks[19], (DEPTH, HY_W), 0.5),
        'mem_norm_g': gain(ks[20], D_MODEL),
        'w_mem_kv': nrm(ks[21], (DEPTH, D_MODEL, 2 * MEM_W), D_MODEL ** -0.5),
        'mq_norm_g': gain(ks[22], HEAD_DIM),
        'mk_norm_g': gain(ks[23], HEAD_DIM),
        'w_br_attn': nrm(ks[24], (DEPTH, ATT_W, D_MODEL), ATT_W ** -0.5),
        'w_br_hyena': nrm(ks[25], (DEPTH, HY_W, D_MODEL), HY_W ** -0.5),
        'w_br_mem': nrm(ks[26], (DEPTH, MEM_W, D_MODEL), MEM_W ** -0.5),
        'w_out': nrm(ks[27], (DEPTH, D_MODEL, D_MODEL), D_MODEL ** -0.5),
        'norm2_g': gain(ks[28], D_MODEL),
        'w_router_group': nrm(ks[29], (DEPTH, D_MODEL, N_GROUPS), D_MODEL ** -0.5),
        'b_router_group': nrm(ks[30], (DEPTH, N_GROUPS), 0.01),
        'w_router_expert': nrm(ks[31], (DEPTH, D_MODEL, N_EXPERTS), D_MODEL ** -0.5),
        'b_router_expert': nrm(ks[32], (DEPTH, N_EXPERTS), 0.01),
        'w_gate_e': nrm(ks[33], (DEPTH, N_EXPERTS, D_MODEL, D_EXPERT), D_MODEL ** -0.5),
        'w_up_e': nrm(ks[34], (DEPTH, N_EXPERTS, D_MODEL, D_EXPERT), D_MODEL ** -0.5),
        'w_down_e': nrm(ks[35], (DEPTH, N_EXPERTS, D_EXPERT, D_MODEL), D_EXPERT ** -0.5),
    }


def reference(x_prompt, x_sample, mem_prompt, mem_sample, norm1_g, w_in, b_gate, q_norm_g, k_norm_g,
              hy_conv_w, hy_conv_b, hf_w1, hf_b1, hf_w2, hf_b2, hf_w3, hf_b3, hf_freq, hy_decay, hy_bias,
              mem_norm_g, w_mem_kv, mq_norm_g, mk_norm_g, w_br_attn, w_br_hyena, w_br_mem, w_out,
              norm2_g, w_router_group, b_router_group, w_router_expert, b_router_expert,
              w_gate_e, w_up_e, w_down_e):
    params = dict(norm1_g=norm1_g, w_in=w_in, b_gate=b_gate, q_norm_g=q_norm_g, k_norm_g=k_norm_g,
                  hy_conv_w=hy_conv_w, hy_conv_b=hy_conv_b, hf_w1=hf_w1, hf_b1=hf_b1, hf_w2=hf_w2,
                  hf_b2=hf_b2, hf_w3=hf_w3, hf_b3=hf_b3, hf_freq=hf_freq, hy_decay=hy_decay,
                  hy_bias=hy_bias, mem_norm_g=mem_norm_g, w_mem_kv=w_mem_kv, mq_norm_g=mq_norm_g,
                  mk_norm_g=mk_norm_g, w_br_attn=w_br_attn, w_br_hyena=w_br_hyena, w_br_mem=w_br_mem,
                  w_out=w_out, norm2_g=norm2_g, w_router_group=w_router_group,
                  b_router_group=b_router_group, w_router_expert=w_router_expert,
                  b_router_expert=b_router_expert, w_gate_e=w_gate_e, w_up_e=w_up_e, w_down_e=w_down_e)

    def trunk(x, mem):
        for i in range(DEPTH):
            lp = {name: arr[i] for name, arr in params.items()}
            x = mixer_sublayer(x, mem, lp)
            x = moe_sublayer(x, lp)
        return x

    y_prompt = trunk(x_prompt, mem_prompt)
    y_sample = trunk(x_sample, mem_sample)
    return (y_prompt, y_sample)
```

```python
import functools
import math

import jax
import jax.numpy as jnp
import numpy as np
from jax import lax
from jax.experimental import pallas as pl
from jax.experimental.pallas import tpu as pltpu

F32 = jnp.float32
BF16 = jnp.bfloat16

HEAD_DIM = 128
KV_GROUP = 4
GRID_W = 64
ROPE_THETA = 10000.0
HY_BANDS = 8
HY_EMB = 1 + 2 * HY_BANDS
HY_SHORT = 3
MEM_HEADS = 4
N_GROUPS = 4
EXPERTS_PER_GROUP = 8
N_EXPERTS = N_GROUPS * EXPERTS_PER_GROUP
TOP_K = 2
EPS = 1e-6
MASK_VALUE = -1e30

LANES = 128
MXU_DIM = 256
VMEM_LIMIT_BYTES = 52 * 1024 * 1024

ROW_TILE = 1024
COL_TILE = 512
PREP_TILE = 512
ATT_TQ = 512
ATT_TK = 1024
DFT_N2 = 256
DFT_COLS = 2048
MOE_TM = 256
GATHER_ROWS = 512
EMB_PAD = 32
ROUTER_PAD = 128


def _params(*sem):
    return pltpu.CompilerParams(dimension_semantics=sem, vmem_limit_bytes=VMEM_LIMIT_BYTES)


def _rms(x, g):
    ms = jnp.mean(x * x, axis=-1, keepdims=True)
    return x * lax.rsqrt(ms + EPS) * g


def _normed_matmul_body(x_ref, g_ref, w_ref, o_ref, xn_ref):
    @pl.when(pl.program_id(1) == 0)
    def _():
        xn_ref[...] = _rms(x_ref[...], g_ref[...]).astype(BF16)

    o_ref[...] = jnp.dot(xn_ref[...], w_ref[...], preferred_element_type=F32)


def normed_matmul(x, g, w):
    T, D = x.shape
    N = w.shape[1]
    tm = min(ROW_TILE, T)
    tn = COL_TILE
    return pl.pallas_call(
        _normed_matmul_body,
        grid=(T // tm, N // tn),
        in_specs=[pl.BlockSpec((tm, D), lambda i, j: (i, 0)),
                  pl.BlockSpec((1, D), lambda i, j: (0, 0)),
                  pl.BlockSpec((D, tn), lambda i, j: (0, j))],
        out_specs=pl.BlockSpec((tm, tn), lambda i, j: (i, j)),
        out_shape=jax.ShapeDtypeStruct((T, N), F32),
        scratch_shapes=[pltpu.VMEM((tm, D), BF16)],
        compiler_params=_params("parallel", "arbitrary"),
        name="normed_matmul",
    )(x, g.reshape(1, D), w)


def _qkv_prep_body(zq_ref, zk_ref, zv_ref, cos_ref, sin_ref, gq_ref, gk_ref, q_ref, k_ref, v_ref, *, scale):
    cosf = cos_ref[...]
    sinf = sin_ref[...]
    lane = lax.broadcasted_iota(jnp.int32, cosf.shape, 1)
    even = (lane % 2) == 0

    def prep(x, g):
        xn = _rms(x, g)
        partner = jnp.where(even, pltpu.roll(xn, LANES - 1, 1), pltpu.roll(xn, 1, 1))
        return xn * cosf + partner * sinf

    for h in range(zq_ref.shape[-1] // HEAD_DIM):
        sl = slice(h * HEAD_DIM, (h + 1) * HEAD_DIM)
        q_ref[0, :, sl] = (prep(zq_ref[0, :, sl], gq_ref[...]) * scale).astype(BF16)
    for h in range(zk_ref.shape[-1] // HEAD_DIM):
        sl = slice(h * HEAD_DIM, (h + 1) * HEAD_DIM)
        k_ref[0, :, sl] = prep(zk_ref[0, :, sl], gk_ref[...]).astype(BF16)
    v_ref[...] = zv_ref[...].astype(BF16)


def qkv_prep(z3, cosf, sinf, gq, gk, att_w, kv_w):
    B, L, _ = z3.shape
    tl = min(PREP_TILE, L)
    off_k = att_w // kv_w
    body = functools.partial(_qkv_prep_body, scale=HEAD_DIM ** -0.5)
    return pl.pallas_call(
        body,
        grid=(B, L // tl),
        in_specs=[pl.BlockSpec((1, tl, att_w), lambda b, i: (b, i, 0)),
                  pl.BlockSpec((1, tl, kv_w), lambda b, i: (b, i, off_k)),
                  pl.BlockSpec((1, tl, kv_w), lambda b, i: (b, i, off_k + 1)),
                  pl.BlockSpec((tl, HEAD_DIM), lambda b, i: (i, 0)),
                  pl.BlockSpec((tl, HEAD_DIM), lambda b, i: (i, 0)),
                  pl.BlockSpec((1, HEAD_DIM), lambda b, i: (0, 0)),
                  pl.BlockSpec((1, HEAD_DIM), lambda b, i: (0, 0))],
        out_specs=[pl.BlockSpec((1, tl, att_w), lambda b, i: (b, i, 0)),
                   pl.BlockSpec((1, tl, kv_w), lambda b, i: (b, i, 0)),
                   pl.BlockSpec((1, tl, kv_w), lambda b, i: (b, i, 0))],
        out_shape=[jax.ShapeDtypeStruct((B, L, att_w), BF16),
                   jax.ShapeDtypeStruct((B, L, kv_w), BF16),
                   jax.ShapeDtypeStruct((B, L, kv_w), BF16)],
        compiler_params=_params("parallel", "parallel"),
        name="qkv_prep",
    )(z3, z3, z3, cosf, sinf, gq.reshape(1, HEAD_DIM), gk.reshape(1, HEAD_DIM))


def _flash_body(q_ref, k_ref, v_ref, o_ref, m_ref, l_ref, acc_ref, *, n_k):
    kk = pl.program_id(3)

    @pl.when(kk == 0)
    def _():
        m_ref[...] = jnp.full(m_ref.shape, MASK_VALUE, F32)
        l_ref[...] = jnp.zeros(l_ref.shape, F32)
        acc_ref[...] = jnp.zeros(acc_ref.shape, F32)

    k = k_ref[0]
    v = v_ref[0]
    for h in range(KV_GROUP):
        q = q_ref[0, :, h * HEAD_DIM:(h + 1) * HEAD_DIM]
        s = lax.dot_general(q, k, (((1,), (1,)), ((), ())), preferred_element_type=F32)
        m_prev = m_ref[h]
        m_new = jnp.maximum(m_prev, jnp.max(s, axis=-1, keepdims=True))
        alpha = jnp.exp(m_prev - m_new)
        p = jnp.exp(s - m_new)
        l_ref[h] = alpha * l_ref[h] + jnp.sum(p, axis=-1, keepdims=True)
        acc_ref[h] = alpha * acc_ref[h] + jnp.dot(p.astype(BF16), v, preferred_element_type=F32)
        m_ref[h] = m_new

    @pl.when(kk == n_k - 1)
    def _():
        for h in range(KV_GROUP):
            o_ref[0, :, h * HEAD_DIM:(h + 1) * HEAD_DIM] = (acc_ref[h] / l_ref[h]).astype(o_ref.dtype)


def flash_attention(q, k, v):
    B, L, att_w = q.shape
    n_kv = k.shape[-1] // HEAD_DIM
    tq = min(ATT_TQ, L)
    tk = min(ATT_TK, L)
    gw = KV_GROUP * HEAD_DIM
    return pl.pallas_call(
        functools.partial(_flash_body, n_k=L // tk),
        grid=(B, n_kv, L // tq, L // tk),
        in_specs=[pl.BlockSpec((1, tq, gw), lambda b, g, i, kk: (b, i, g)),
                  pl.BlockSpec((1, tk, HEAD_DIM), lambda b, g, i, kk: (b, kk, g)),
                  pl.BlockSpec((1, tk, HEAD_DIM), lambda b, g, i, kk: (b, kk, g))],
        out_specs=pl.BlockSpec((1, tq, gw), lambda b, g, i, kk: (b, i, g)),
        out_shape=jax.ShapeDtypeStruct((B, L, att_w), BF16),
        scratch_shapes=[pltpu.VMEM((KV_GROUP, tq, 1), F32),
                        pltpu.VMEM((KV_GROUP, tq, 1), F32),
                        pltpu.VMEM((KV_GROUP, tq, HEAD_DIM), F32)],
        compiler_params=_params("parallel", "parallel", "parallel", "arbitrary"),
        name="flash_attention",
    )(q, k, v)


def _mem_attn_body(zm_ref, kv_ref, gq_ref, gk_ref, o_ref, *, scale):
    mem_w = zm_ref.shape[-1]
    for h in range(mem_w // HEAD_DIM):
        sl = slice(h * HEAD_DIM, (h + 1) * HEAD_DIM)
        q = (_rms(zm_ref[0, :, sl], gq_ref[...]) * scale).astype(BF16)
        k = _rms(kv_ref[0, :, sl], gk_ref[...]).astype(BF16)
        v = kv_ref[0, :, mem_w + h * HEAD_DIM:mem_w + (h + 1) * HEAD_DIM].astype(BF16)
        s = lax.dot_general(q, k, (((1,), (1,)), ((), ())), preferred_element_type=F32)
        p = jnp.exp(s - jnp.max(s, axis=-1, keepdims=True))
        o = jnp.dot(p.astype(BF16), v, preferred_element_type=F32)
        o_ref[0, :, sl] = (o / jnp.sum(p, axis=-1, keepdims=True)).astype(o_ref.dtype)


def mem_attention(z3, kv, gq, gk, mem_w, off_mq):
    B, L, _ = z3.shape
    M = kv.shape[1]
    tl = min(PREP_TILE, L)
    body = functools.partial(_mem_attn_body, scale=HEAD_DIM ** -0.5)
    return pl.pallas_call(
        body,
        grid=(B, L // tl),
        in_specs=[pl.BlockSpec((1, tl, mem_w), lambda b, i: (b, i, off_mq // mem_w)),
                  pl.BlockSpec((1, M, 2 * mem_w), lambda b, i: (b, 0, 0)),
                  pl.BlockSpec((1, HEAD_DIM), lambda b, i: (0, 0)),
                  pl.BlockSpec((1, HEAD_DIM), lambda b, i: (0, 0))],
        out_specs=pl.BlockSpec((1, tl, mem_w), lambda b, i: (b, i, 0)),
        out_shape=jax.ShapeDtypeStruct((B, L, mem_w), BF16),
        compiler_params=_params("parallel", "parallel"),
        name="mem_attention",
    )(z3, kv, gq.reshape(1, HEAD_DIM), gk.reshape(1, HEAD_DIM))


def _hyena_pre_body(z_ref, prev_ref, next_ref, w_ref, b_ref, vin_ref, x0_ref, *, n_tiles):
    i = pl.program_id(1)
    x = z_ref[0]
    tl, width = x.shape
    hy_w = width // 3
    row = lax.broadcasted_iota(jnp.int32, x.shape, 0)
    prev_row = jnp.where(i > 0, prev_ref[0, 7:8, :], 0.0)
    next_row = jnp.where(i < n_tiles - 1, next_ref[0, 0:1, :], 0.0)
    x_prev = jnp.where(row == 0, prev_row, pltpu.roll(x, 1, 0))
    x_next = jnp.where(row == tl - 1, next_row, pltpu.roll(x, tl - 1, 0))
    u = w_ref[0:1, :] * x_prev + w_ref[1:2, :] * x + w_ref[2:3, :] * x_next + b_ref[...]
    vin_ref[0] = u[:, 2 * hy_w:] * u[:, hy_w:2 * hy_w]
    x0_ref[0] = u[:, :hy_w]


def hyena_pre(z3, conv_w, conv_b, hy_w, off_hy):
    B, L, _ = z3.shape
    width = 3 * hy_w
    tl = min(PREP_TILE, L)
    cb = off_hy // width
    r8 = tl // 8
    last8 = L // 8 - 1
    return pl.pallas_call(
        functools.partial(_hyena_pre_body, n_tiles=L // tl),
        grid=(B, L // tl),
        in_specs=[pl.BlockSpec((1, tl, width), lambda b, i: (b, i, cb)),
                  pl.BlockSpec((1, 8, width), lambda b, i: (b, jnp.maximum(i * r8 - 1, 0), cb)),
                  pl.BlockSpec((1, 8, width), lambda b, i: (b, jnp.minimum((i + 1) * r8, last8), cb)),
                  pl.BlockSpec((HY_SHORT, width), lambda b, i: (0, 0)),
                  pl.BlockSpec((1, width), lambda b, i: (0, 0))],
        out_specs=[pl.BlockSpec((1, tl, hy_w), lambda b, i: (b, i, 0)),
                   pl.BlockSpec((1, tl, hy_w), lambda b, i: (b, i, 0))],
        out_shape=[jax.ShapeDtypeStruct((B, L, hy_w), F32),
                   jax.ShapeDtypeStruct((B, L, hy_w), F32)],
        compiler_params=_params("parallel", "parallel"),
        name="hyena_pre",
    )(z3, z3, z3, conv_w.reshape(HY_SHORT, width), conv_b.reshape(1, width))


def _hyena_filter_body(e_ref, w1_ref, b1_ref, w2_ref, b2_ref, w3_ref, b3_ref, freq_ref, dec_ref,
                       kf_ref, ss_ref):
    hp = lax.Precision.HIGHEST
    e = e_ref[...]
    t = e[:, 0:1]
    valid = e[:, HY_EMB:HY_EMB + 1]
    freq = freq_ref[...]
    h = jnp.sin(freq * (jnp.dot(e, w1_ref[...], precision=hp, preferred_element_type=F32) + b1_ref[...]))
    h = jnp.sin(freq * (jnp.dot(h, w2_ref[...], precision=hp, preferred_element_type=F32) + b2_ref[...]))
    h = jnp.dot(h, w3_ref[...], precision=hp, preferred_element_type=F32) + b3_ref[...]
    kf = h * jnp.exp(-t * jnp.abs(dec_ref[...])) * valid
    kf_ref[...] = kf

    @pl.when(pl.program_id(0) == 0)
    def _():
        ss_ref[...] = jnp.zeros(ss_ref.shape, F32)

    ss_ref[...] += jnp.sum(kf * kf, axis=0, keepdims=True)


def hyena_filter(emb2, w1p, b1, w2, b2, w3, b3, freq, decay, hy_w):
    n_rows = emb2.shape[0]
    hidden = w2.shape[0]
    tl = min(PREP_TILE, n_rows // 2)
    nblk = n_rows // tl
    half = nblk // 2

    def dirmap(i):
        return (0, jnp.where(i >= half, 1, 0))

    const = lambda i: (0, 0)
    return pl.pallas_call(
        _hyena_filter_body,
        grid=(nblk,),
        in_specs=[pl.BlockSpec((tl, EMB_PAD), lambda i: (i, 0)),
                  pl.BlockSpec((EMB_PAD, hidden), const),
                  pl.BlockSpec((1, hidden), const),
                  pl.BlockSpec((hidden, hidden), const),
                  pl.BlockSpec((1, hidden), const),
                  pl.BlockSpec((hidden, hy_w), dirmap),
                  pl.BlockSpec((1, hy_w), dirmap),
                  pl.BlockSpec((1, hidden), const),
                  pl.BlockSpec((1, hy_w), dirmap)],
        out_specs=[pl.BlockSpec((tl, hy_w), lambda i: (i, 0)),
                   pl.BlockSpec((1, hy_w), const)],
        out_shape=[jax.ShapeDtypeStruct((n_rows, hy_w), F32),
                   jax.ShapeDtypeStruct((1, hy_w), F32)],
        compiler_params=_params("arbitrary"),
        name="hyena_filter",
    )(emb2, w1p, b1.reshape(1, hidden), w2, b2.reshape(1, hidden), w3, b3.reshape(1, -1),
      freq.reshape(1, hidden), decay.reshape(1, -1))


def _left_matmul_body(m_ref, x_ref, o_ref):
    o_ref[0] = jnp.dot(m_ref[...], x_ref[0].astype(BF16), preferred_element_type=F32)


def _left_matmul_post_body(m_ref, x_ref, vin_ref, x0_ref, bias_ref, o_ref):
    y = jnp.dot(m_ref[...], x_ref[0].astype(BF16), preferred_element_type=F32)
    vin = vin_ref[0]
    o_ref[0] = ((y + vin * bias_ref[...]) * x0_ref[0]).astype(o_ref.dtype)


def left_matmul(mat, x, post=None):
    P, K, cols = x.shape
    R = mat.shape[0]
    tc = min(DFT_COLS, cols)
    in_specs = [pl.BlockSpec((R, K), lambda p, c: (0, 0)),
                pl.BlockSpec((1, K, tc), lambda p, c: (p, 0, c))]
    args = [mat, x]
    if post is None:
        body, dtype = _left_matmul_body, F32
    else:
        vin, x0, bias_row = post
        body, dtype = _left_matmul_post_body, BF16
        in_specs += [pl.BlockSpec((1, R, tc), lambda p, c: (p, 0, c)),
                     pl.BlockSpec((1, R, tc), lambda p, c: (p, 0, c)),
                     pl.BlockSpec((1, tc), lambda p, c: (0, c))]
        args += [vin, x0, bias_row]
    return pl.pallas_call(
        body,
        grid=(P, cols // tc),
        in_specs=in_specs,
        out_specs=pl.BlockSpec((1, R, tc), lambda p, c: (p, 0, c)),
        out_shape=jax.ShapeDtypeStruct((P, R, cols), dtype),
        compiler_params=_params("parallel", "parallel"),
        name="dft_outer",
    )(*args)


def _filter_spectrum_body(g_ref, a_ref, s_ref, k_ref):
    n2 = a_ref.shape[-2]
    a = jnp.concatenate([a_ref[0, 0, 0], a_ref[0, 1, 0]], axis=0).astype(BF16)
    y = jnp.dot(g_ref[0], a, preferred_element_type=F32) * s_ref[...]
    k_ref[0, 0] = y[:n2]
    k_ref[1, 0] = y[n2:]


def filter_spectrum(gs, a5, scale_row):
    _, _, n1, n2, C = a5.shape
    return pl.pallas_call(
        _filter_spectrum_body,
        grid=(n1,),
        in_specs=[pl.BlockSpec((1, 2 * n2, 2 * n2), lambda k: (k, 0, 0)),
                  pl.BlockSpec((1, 2, 1, n2, C), lambda k: (0, 0, k, 0, 0)),
                  pl.BlockSpec((1, C), lambda k: (0, 0))],
        out_specs=pl.BlockSpec((2, 1, n2, C), lambda k: (0, k, 0, 0)),
        out_shape=jax.ShapeDtypeStruct((2, n1, n2, C), F32),
        compiler_params=_params("parallel"),
        name="filter_spectrum",
    )(gs, a5, scale_row)


def _dft_inner_body(g_ref, gi_ref, a_ref, k_ref, p_ref):
    n2 = a_ref.shape[-2]
    a = jnp.concatenate([a_ref[0, 0, 0], a_ref[0, 1, 0]], axis=0).astype(BF16)
    y = jnp.dot(g_ref[0], a, preferred_element_type=F32)
    yr, yi = y[:n2], y[n2:]
    kr, ki = k_ref[0, 0], k_ref[1, 0]
    z = jnp.concatenate([yr * kr - yi * ki, yr * ki + yi * kr], axis=0).astype(BF16)
    p = jnp.dot(gi_ref[0], z, preferred_element_type=F32)
    p_ref[0, 0, 0] = p[:n2]
    p_ref[0, 1, 0] = p[n2:]


def dft_inner(gs, gis, a5, kspec):
    P, _, n1, n2, C = a5.shape
    gspec = pl.BlockSpec((1, 2 * n2, 2 * n2), lambda k, p: (k, 0, 0))
    return pl.pallas_call(
        _dft_inner_body,
        grid=(n1, P),
        in_specs=[gspec, gspec,
                  pl.BlockSpec((1, 2, 1, n2, C), lambda k, p: (p, 0, k, 0, 0)),
                  pl.BlockSpec((2, 1, n2, C), lambda k, p: (0, k, 0, 0))],
        out_specs=pl.BlockSpec((1, 2, 1, n2, C), lambda k, p: (p, 0, k, 0, 0)),
        out_shape=jax.ShapeDtypeStruct((P, 2, n1, n2, C), F32),
        compiler_params=_params("parallel", "parallel"),
        name="dft_inner",
    )(gs, gis, a5, kspec)


def _dft_tables(n1, n2):
    n = n1 * n2
    h = n1 // 2
    a = np.arange(n1)
    ang1 = -2.0 * np.pi * ((a[:, None] * a[None, :]) % n1) / n1
    f1r, f1i = np.cos(ang1), np.sin(ang1)
    fwd_sig = np.block([[f1r[:, :h], -f1i[:, :h]], [f1i[:, :h], f1r[:, :h]]])
    fwd_flt = np.concatenate([f1r, f1i], axis=0)
    inv = np.block([[f1r[:h], f1i[:h]], [-f1i[:h], f1r[:h]]])
    k1 = lax.broadcasted_iota(jnp.int32, (n1, n2, n2), 0)
    k2 = lax.broadcasted_iota(jnp.int32, (n1, n2, n2), 1)
    m2 = lax.broadcasted_iota(jnp.int32, (n1, n2, n2), 2)
    ang2 = (-2.0 * math.pi / n) * (((k1 + n1 * k2) * m2) % n).astype(F32)
    gr, gi = jnp.cos(ang2), jnp.sin(ang2)
    gs = jnp.concatenate([jnp.concatenate([gr, -gi], axis=2), jnp.concatenate([gi, gr], axis=2)], axis=1)
    gis = jnp.swapaxes(gs, 1, 2) * (1.0 / n)
    cvt = lambda m: jnp.asarray(m, dtype=F32).astype(BF16)
    return cvt(fwd_sig), cvt(fwd_flt), cvt(inv), gs.astype(BF16), gis.astype(BF16)


def _filter_embedding(L):
    t = jnp.linspace(0.0, 1.0, L, dtype=F32)
    w = 2.0 * math.pi * jnp.arange(L, dtype=F32) / L
    bands = jnp.linspace(1e-4, HY_BANDS - 1, HY_BANDS, dtype=F32)
    fw = w[:, None] * bands[None, :]
    emb = jnp.concatenate([t[:, None], jnp.cos(fw), -jnp.sin(fw)], axis=-1)
    valid = jnp.ones((L, 1), F32)
    emb = jnp.concatenate([emb, valid, jnp.zeros((L, EMB_PAD - HY_EMB - 1), F32)], axis=-1)
    back = jnp.concatenate([jnp.zeros((1, EMB_PAD), F32), emb[:0:-1]], axis=0)
    return jnp.concatenate([emb, back], axis=0)


def hyena_branch(z3, lp, hy_w, off_hy):
    B, L, _ = z3.shape
    n = 2 * L
    n2 = DFT_N2
    n1 = n // n2
    P = B // 2
    vin, x0 = hyena_pre(z3, lp['hy_conv_w'], lp['hy_conv_b'], hy_w, off_hy)
    fwd_sig, fwd_flt, inv, gs, gis = _dft_tables(n1, n2)

    hidden = lp['hf_w2'].shape[0]
    w1p = jnp.concatenate([lp['hf_w1'], jnp.zeros((EMB_PAD - HY_EMB, hidden), F32)], axis=0)
    kf, ss = hyena_filter(_filter_embedding(L), w1p, lp['hf_b1'], lp['hf_w2'], lp['hf_b2'], lp['hf_w3'],
                          lp['hf_b3'], lp['hf_freq'], lp['hy_decay'], hy_w)
    kscale = lax.rsqrt(ss + EPS)
    ka = left_matmul(fwd_flt, kf.reshape(1, n1, n2 * hy_w))
    kspec = filter_spectrum(gs, ka.reshape(1, 2, n1, n2, hy_w), kscale)

    a = left_matmul(fwd_sig, vin.reshape(P, n1, n2 * hy_w))
    pm = dft_inner(gs, gis, a.reshape(P, 2, n1, n2, hy_w), kspec)
    pstack = pm.reshape(P, 2 * n1, n2 * hy_w)
    bias_row = jnp.tile(lp['hy_bias'].reshape(1, hy_w), (1, n2))
    out = left_matmul(inv, pstack, post=(vin.reshape(P, n1, n2 * hy_w), x0.reshape(P, n1, n2 * hy_w), bias_row))
    return out.reshape(B, L, hy_w)


def _merge_body(a_ref, h_ref, m_ref, wa_ref, wh_ref, wm_ref, zg0_ref, zg1_ref, zg2_ref,
                bg0_ref, bg1_ref, bg2_ref, o_ref):
    def term(x_ref, w_ref, zg_ref, bg_ref):
        gate = jax.nn.sigmoid(zg_ref[...] + bg_ref[...])
        return gate * jnp.dot(x_ref[...], w_ref[...], preferred_element_type=F32)

    acc = term(a_ref, wa_ref, zg0_ref, bg0_ref)
    acc = acc + term(h_ref, wh_ref, zg1_ref, bg1_ref)
    acc = acc + term(m_ref, wm_ref, zg2_ref, bg2_ref)
    o_ref[...] = acc.astype(o_ref.dtype)


def gated_merge(a_out, h_out, m_out, wa, wh, wm, z2, b_gate, off_gate):
    T = a_out.shape[0]
    D = wa.shape[1]
    tm = min(ROW_TILE, T)
    tn = COL_TILE
    nj = D // tn
    gb = off_gate // tn

    def xspec(width):
        return pl.BlockSpec((tm, width), lambda i, j: (i, 0))

    def wspec(width):
        return pl.BlockSpec((width, tn), lambda i, j: (0, j))

    def zgspec(b):
        return pl.BlockSpec((tm, tn), lambda i, j: (i, gb + b * nj + j))

    def bgspec(b):
        return pl.BlockSpec((1, tn), lambda i, j: (0, b * nj + j))

    bg = b_gate.reshape(1, -1)
    return pl.pallas_call(
        _merge_body,
        grid=(T // tm, nj),
        in_specs=[xspec(a_out.shape[1]), xspec(h_out.shape[1]), xspec(m_out.shape[1]),
                  wspec(wa.shape[0]), wspec(wh.shape[0]), wspec(wm.shape[0]),
                  zgspec(0), zgspec(1), zgspec(2), bgspec(0), bgspec(1), bgspec(2)],
        out_specs=pl.BlockSpec((tm, tn), lambda i, j: (i, j)),
        out_shape=jax.ShapeDtypeStruct((T, D), BF16),
        compiler_params=_params("parallel", "parallel"),
        name="gated_merge",
    )(a_out, h_out, m_out, wa, wh, wm, z2, z2, z2, bg, bg, bg)


def _out_router_body(mg_ref, x_ref, wo_ref, g2_ref, whi_ref, wlo_ref, br_ref, x1_ref, h2_ref, lg_ref):
    x1 = x_ref[...] + jnp.dot(mg_ref[...], wo_ref[...], preferred_element_type=F32)
    x1_ref[...] = x1
    h = _rms(x1, g2_ref[...])
    h_hi = h.astype(BF16)
    h2_ref[...] = h_hi
    h_lo = (h - h_hi.astype(F32)).astype(BF16)
    lg = jnp.dot(h_hi, whi_ref[...], preferred_element_type=F32)
    lg = lg + jnp.dot(h_lo, whi_ref[...], preferred_element_type=F32)
    lg = lg + jnp.dot(h_hi, wlo_ref[...], preferred_element_type=F32)
    lg_ref[...] = lg + br_ref[...]


def out_router(merged, x2, w_out, g2, wr_hi, wr_lo, br):
    T, D = x2.shape
    tm = min(PREP_TILE, T)
    const = lambda i: (0, 0)
    row = lambda i: (i, 0)
    return pl.pallas_call(
        _out_router_body,
        grid=(T // tm,),
        in_specs=[pl.BlockSpec((tm, D), row), pl.BlockSpec((tm, D), row),
                  pl.BlockSpec((D, D), const), pl.BlockSpec((1, D), const),
                  pl.BlockSpec((D, ROUTER_PAD), const), pl.BlockSpec((D, ROUTER_PAD), const),
                  pl.BlockSpec((1, ROUTER_PAD), const)],
        out_specs=[pl.BlockSpec((tm, D), row), pl.BlockSpec((tm, D), row),
                   pl.BlockSpec((tm, ROUTER_PAD), row)],
        out_shape=[jax.ShapeDtypeStruct((T, D), F32), jax.ShapeDtypeStruct((T, D), BF16),
                   jax.ShapeDtypeStruct((T, ROUTER_PAD), F32)],
        compiler_params=_params("parallel"),
        name="out_router",
    )(merged, x2, w_out, g2.reshape(1, D), wr_hi, wr_lo, br)


def _gather_body(idx_ref, src_ref, out_ref, sem):
    base = pl.program_id(0) * GATHER_ROWS

    def copy(r):
        return pltpu.make_async_copy(src_ref.at[idx_ref[base + r]], out_ref.at[base + r], sem)

    def issue(r, carry):
        copy(r).start()
        return carry

    def drain(r, carry):
        copy(r).wait()
        return carry

    lax.fori_loop(0, GATHER_ROWS, issue, 0)
    lax.fori_loop(0, GATHER_ROWS, drain, 0)


def gather_rows(idx, src3):
    P = idx.shape[0]
    return pl.pallas_call(
        _gather_body,
        grid_spec=pltpu.PrefetchScalarGridSpec(
            num_scalar_prefetch=1,
            grid=(P // GATHER_ROWS,),
            in_specs=[pl.BlockSpec(memory_space=pl.ANY)],
            out_specs=pl.BlockSpec(memory_space=pl.ANY),
            scratch_shapes=[pltpu.SemaphoreType.DMA(())]),
        out_shape=jax.ShapeDtypeStruct((P,) + src3.shape[1:], src3.dtype),
        compiler_params=_params("arbitrary"),
        name="gather_rows",
    )(idx, src3)


def _moe_body(blk_e_ref, n_used_ref, x_ref, wg_ref, wu_ref, wd_ref, cw_ref, o_ref):
    del blk_e_ref
    i = pl.program_id(0)

    @pl.when(i < n_used_ref[0])
    def _():
        x = x_ref[...]
        g = jnp.dot(x, wg_ref[0], preferred_element_type=F32)
        u = jnp.dot(x, wu_ref[0], preferred_element_type=F32)
        a = (g * jax.nn.sigmoid(g) * u).astype(BF16)
        o_ref[...] = jnp.dot(a, wd_ref[0], preferred_element_type=F32) * cw_ref[...]

    @pl.when(i >= n_used_ref[0])
    def _():
        o_ref[...] = jnp.zeros(o_ref.shape, o_ref.dtype)


def moe_experts(blk_e, n_used, xs, wg, wu, wd, cw):
    P, D = xs.shape
    De = wg.shape[2]
    tm = MOE_TM
    return pl.pallas_call(
        _moe_body,
        grid_spec=pltpu.PrefetchScalarGridSpec(
            num_scalar_prefetch=2,
            grid=(P // tm,),
            in_specs=[pl.BlockSpec((tm, D), lambda i, be, nu: (i, 0)),
                      pl.BlockSpec((1, D, De), lambda i, be, nu: (be[i], 0, 0)),
                      pl.BlockSpec((1, D, De), lambda i, be, nu: (be[i], 0, 0)),
                      pl.BlockSpec((1, De, D), lambda i, be, nu: (be[i], 0, 0)),
                      pl.BlockSpec((tm, 1), lambda i, be, nu: (i, 0))],
            out_specs=pl.BlockSpec((tm, D), lambda i, be, nu: (i, 0))),
        out_shape=jax.ShapeDtypeStruct((P, D), F32),
        compiler_params=_params("arbitrary"),
        name="moe_experts",
    )(blk_e, n_used, xs, wg, wu, wd, cw)


def _combine_body(pos_ref, x1_ref, yb_ref, o_ref, buf0, buf1, sem):
    tg = x1_ref.shape[0]
    base = pl.program_id(0) * tg

    def copies(r):
        a = TOP_K * (base + r)
        return (pltpu.make_async_copy(yb_ref.at[pos_ref[a]], buf0.at[r], sem),
                pltpu.make_async_copy(yb_ref.at[pos_ref[a + 1]], buf1.at[r], sem))

    def issue(r, carry):
        c0, c1 = copies(r)
        c0.start()
        c1.start()
        return carry

    def drain(r, carry):
        c0, c1 = copies(r)
        c0.wait()
        c1.wait()
        return carry

    lax.fori_loop(0, tg, issue, 0)
    lax.fori_loop(0, tg, drain, 0)
    o_ref[...] = x1_ref[...] + buf0[...] + buf1[...]


def moe_combine(pos_flat, x1_3, yb3):
    T = x1_3.shape[0]
    tg = MOE_TM
    slab = x1_3.shape[1:]
    return pl.pallas_call(
        _combine_body,
        grid_spec=pltpu.PrefetchScalarGridSpec(
            num_scalar_prefetch=1,
            grid=(T // tg,),
            in_specs=[pl.BlockSpec((tg,) + slab, lambda i, pos: (i, 0, 0)),
                      pl.BlockSpec(memory_space=pl.ANY)],
            out_specs=pl.BlockSpec((tg,) + slab, lambda i, pos: (i, 0, 0)),
            scratch_shapes=[pltpu.VMEM((tg,) + slab, F32), pltpu.VMEM((tg,) + slab, F32),
                            pltpu.SemaphoreType.DMA(())]),
        out_shape=jax.ShapeDtypeStruct(x1_3.shape, F32),
        compiler_params=_params("arbitrary"),
        name="moe_combine",
    )(pos_flat, x1_3, yb3)


def _route(logits):
    T = logits.shape[0]
    g_logits = logits[:, :N_GROUPS]
    e_logits = logits[:, N_GROUPS:N_GROUPS + N_EXPERTS].reshape(T, N_GROUPS, EXPERTS_PER_GROUP)
    g_p, g_idx = lax.top_k(jax.nn.softmax(g_logits, axis=-1), 1)
    idx = jnp.broadcast_to(g_idx[:, :, None], (T, 1, EXPERTS_PER_GROUP))
    e_in = jnp.take_along_axis(e_logits, idx, axis=1)[:, 0]
    top_v, top_i = lax.top_k(e_in, TOP_K)
    comb = jax.nn.softmax(top_v, axis=-1) * g_p
    e_flat = (g_idx * EXPERTS_PER_GROUP + top_i).reshape(-1).astype(jnp.int32)
    w_flat = comb.reshape(-1)
    A = T * TOP_K
    tok_flat = jnp.repeat(jnp.arange(T, dtype=jnp.int32), TOP_K)
    unit = max(MOE_TM, GATHER_ROWS)
    P = (A + N_EXPERTS * (MOE_TM - 1) + unit - 1) // unit * unit
    n_blk = P // MOE_TM
    order = jnp.argsort(e_flat)
    e_s = e_flat[order]
    counts = jnp.bincount(e_flat, length=N_EXPERTS).astype(jnp.int32)
    starts = jnp.cumsum(counts) - counts
    padded = (counts + MOE_TM - 1) // MOE_TM * MOE_TM
    pends = jnp.cumsum(padded)
    pstarts = pends - padded
    dest = pstarts[e_s] + jnp.arange(A, dtype=jnp.int32) - starts[e_s]
    row_tok = jnp.zeros((P,), jnp.int32).at[dest].set(tok_flat[order])
    row_w = jnp.zeros((P,), F32).at[dest].set(w_flat[order])
    pos_flat = jnp.zeros((A,), jnp.int32).at[order].set(dest)
    blk_e = jnp.minimum(jnp.searchsorted(pends, jnp.arange(n_blk, dtype=jnp.int32) * MOE_TM, side='right'),
                        N_EXPERTS - 1).astype(jnp.int32)
    n_used = (pends[-1:] // MOE_TM).astype(jnp.int32)
    return row_tok, row_w, pos_flat, blk_e, n_used


def _rope_tables(L):
    rows = L // GRID_W
    pairs = HEAD_DIM // 4
    row = jnp.repeat(jnp.arange(rows, dtype=F32), GRID_W)
    col = jnp.tile(jnp.arange(GRID_W, dtype=F32), rows)
    inv = jnp.power(ROPE_THETA, -jnp.arange(pairs, dtype=F32) / pairs)
    ang = jnp.concatenate([row[:, None] * inv, col[:, None] * inv], axis=-1)
    cos, sin = jnp.cos(ang), jnp.sin(ang)
    cosf = jnp.repeat(cos, 2, axis=-1)
    sinf = jnp.stack([-sin, sin], axis=-1).reshape(L, HEAD_DIM)
    return cosf, sinf


def _trunk(x, mem, lp, wb):
    B, L, D = x.shape
    T = B * L
    att_w = lp['w_br_attn'].shape[0]
    hy_w = lp['w_br_hyena'].shape[0]
    mem_w = lp['w_br_mem'].shape[0]
    kv_w = (lp['w_in'].shape[1] - att_w - 3 * hy_w - mem_w - 3 * D) // 2
    off_hy = att_w + 2 * kv_w
    off_mq = off_hy + 3 * hy_w
    off_gate = off_mq + mem_w
    x2 = x.reshape(T, D)

    z2 = normed_matmul(x2, lp['norm1_g'], wb['w_in'])
    z3 = z2.reshape(B, L, -1)
    cosf, sinf = _rope_tables(L)
    q, k, v = qkv_prep(z3, cosf, sinf, lp['q_norm_g'], lp['k_norm_g'], att_w, kv_w)
    a_out = flash_attention(q, k, v)
    h_out = hyena_branch(z3, lp, hy_w, off_hy)
    M = mem.shape[1]
    kv = normed_matmul(mem.reshape(B * M, D), lp['mem_norm_g'], wb['w_mem_kv']).reshape(B, M, -1)
    m_out = mem_attention(z3, kv, lp['mq_norm_g'], lp['mk_norm_g'], mem_w, off_mq)
    merged = gated_merge(a_out.reshape(T, att_w), h_out.reshape(T, hy_w), m_out.reshape(T, mem_w),
                         wb['w_br_attn'], wb['w_br_hyena'], wb['w_br_mem'], z2, lp['b_gate'], off_gate)
    x1, h2, logits = out_router(merged, x2, wb['w_out'], lp['norm2_g'], wb['wr_hi'], wb['wr_lo'], wb['br'])

    row_tok, row_w, pos_flat, blk_e, n_used = _route(logits)
    slabs = D // LANES
    xs = gather_rows(row_tok, h2.reshape(T, slabs, LANES))
    P = xs.shape[0]
    yb = moe_experts(blk_e, n_used, xs.reshape(P, D), wb['w_gate_e'], wb['w_up_e'], wb['w_down_e'],
                     row_w.reshape(P, 1))
    y = moe_combine(pos_flat, x1.reshape(T, slabs, LANES), yb.reshape(P, slabs, LANES))
    return y.reshape(B, L, D)


def kernel(x_prompt, x_sample, mem_prompt, mem_sample, norm1_g, w_in, b_gate, q_norm_g, k_norm_g, hy_conv_w, hy_conv_b, hf_w1, hf_b1, hf_w2, hf_b2, hf_w3, hf_b3, hf_freq, hy_decay, hy_bias, mem_norm_g, w_mem_kv, mq_norm_g, mk_norm_g, w_br_attn, w_br_hyena, w_br_mem, w_out, norm2_g, w_router_group, b_router_group, w_router_expert, b_router_expert, w_gate_e, w_up_e, w_down_e):
    params = dict(norm1_g=norm1_g, w_in=w_in, b_gate=b_gate, q_norm_g=q_norm_g, k_norm_g=k_norm_g,
                  hy_conv_w=hy_conv_w, hy_conv_b=hy_conv_b, hf_w1=hf_w1, hf_b1=hf_b1, hf_w2=hf_w2,
                  hf_b2=hf_b2, hf_w3=hf_w3, hf_b3=hf_b3, hf_freq=hf_freq, hy_decay=hy_decay,
                  hy_bias=hy_bias, mem_norm_g=mem_norm_g, w_mem_kv=w_mem_kv, mq_norm_g=mq_norm_g,
                  mk_norm_g=mk_norm_g, w_br_attn=w_br_attn, w_br_hyena=w_br_hyena, w_br_mem=w_br_mem,
                  w_out=w_out, norm2_g=norm2_g, w_router_group=w_router_group,
                  b_router_group=b_router_group, w_router_expert=w_router_expert,
                  b_router_expert=b_router_expert, w_gate_e=w_gate_e, w_up_e=w_up_e, w_down_e=w_down_e)
    depth = w_in.shape[0]
    xp, xs = x_prompt, x_sample
    for d in range(depth):
        lp = {name: arr[d] for name, arr in params.items()}
        D = lp['w_in'].shape[0]
        wb = {name: lp[name].astype(BF16) for name in
              ('w_in', 'w_mem_kv', 'w_br_attn', 'w_br_hyena', 'w_br_mem', 'w_out',
               'w_gate_e', 'w_up_e', 'w_down_e')}
        wr = jnp.concatenate([lp['w_router_group'], lp['w_router_expert'],
                              jnp.zeros((D, ROUTER_PAD - N_GROUPS - N_EXPERTS), F32)], axis=1)
        wb['wr_hi'] = wr.astype(BF16)
        wb['wr_lo'] = (wr - wb['wr_hi'].astype(F32)).astype(BF16)
        wb['br'] = jnp.concatenate([lp['b_router_group'], lp['b_router_expert'],
                                    jnp.zeros((ROUTER_PAD - N_GROUPS - N_EXPERTS,), F32)]).reshape(1, ROUTER_PAD)
        xp = _trunk(xp, mem_prompt, lp, wb)
        xs = _trunk(xs, mem_sample, lp, wb)
    return (xp, xs)
```

```python
import functools
import math

import jax
import jax.numpy as jnp
import numpy as np
from jax import lax
from jax.experimental import pallas as pl
from jax.experimental.pallas import tpu as pltpu

F32 = jnp.float32
BF16 = jnp.bfloat16

HEAD_DIM = 128
KV_GROUP = 4
GRID_W = 64
ROPE_THETA = 10000.0
HY_BANDS = 8
HY_EMB = 1 + 2 * HY_BANDS
HY_SHORT = 3
MEM_HEADS = 4
N_GROUPS = 4
EXPERTS_PER_GROUP = 8
N_EXPERTS = N_GROUPS * EXPERTS_PER_GROUP
TOP_K = 2
EPS = 1e-6
MASK_VALUE = -1e30
EXP2_SAFE_RANGE = 120.0

LANES = 128
MXU_DIM = 256
VMEM_LIMIT_BYTES = 52 * 1024 * 1024

ROW_TILE = 1024
COL_TILE = 512
PREP_TILE = 512
ATT_TQ = 512
ATT_TK = 2048
DFT_N2 = 256
DFT_COLS = 2048
MOE_TM = 256
GATHER_ROWS = 512
EMB_PAD = 32
ROUTER_PAD = 128


def _params(*sem):
    return pltpu.CompilerParams(dimension_semantics=sem, vmem_limit_bytes=VMEM_LIMIT_BYTES)


def _rms(x, g):
    ms = jnp.mean(x * x, axis=-1, keepdims=True)
    return x * lax.rsqrt(ms + EPS) * g


def _normed_matmul_body(x_ref, g_ref, w_ref, o_ref, xn_ref):
    @pl.when(pl.program_id(1) == 0)
    def _():
        xn_ref[...] = _rms(x_ref[...], g_ref[...]).astype(BF16)

    o_ref[...] = jnp.dot(xn_ref[...], w_ref[...], preferred_element_type=F32)


def normed_matmul(x, g, w):
    T, D = x.shape
    N = w.shape[1]
    tm = min(ROW_TILE, T)
    tn = COL_TILE
    return pl.pallas_call(
        _normed_matmul_body,
        grid=(T // tm, N // tn),
        in_specs=[pl.BlockSpec((tm, D), lambda i, j: (i, 0)),
                  pl.BlockSpec((1, D), lambda i, j: (0, 0)),
                  pl.BlockSpec((D, tn), lambda i, j: (0, j))],
        out_specs=pl.BlockSpec((tm, tn), lambda i, j: (i, j)),
        out_shape=jax.ShapeDtypeStruct((T, N), F32),
        scratch_shapes=[pltpu.VMEM((tm, D), BF16)],
        compiler_params=_params("parallel", "arbitrary"),
        name="normed_matmul",
    )(x, g.reshape(1, D), w)


def _qkv_prep_body(zq_ref, zk_ref, zv_ref, cos_ref, sin_ref, gq_ref, gk_ref, q_ref, k_ref, v_ref, st_ref, *, scale):
    cosf = cos_ref[...]
    sinf = sin_ref[...]
    tl = cosf.shape[0]
    lane = lax.broadcasted_iota(jnp.int32, cosf.shape, 1)
    even = (lane % 2) == 0
    one_hot0 = jnp.where(lane == 0, 1.0, 0.0).astype(BF16)

    def prep(x, g):
        xn = _rms(x, g)
        partner = jnp.where(even, pltpu.roll(xn, LANES - 1, 1), pltpu.roll(xn, 1, 1))
        return xn * cosf + partner * sinf

    def max_norm2(xb):
        xf = xb.astype(F32)
        return jnp.max(jnp.sum(xf * xf, axis=-1, keepdims=True), axis=0, keepdims=True)

    qmax = jnp.zeros((1, 1), F32)
    for h in range(zq_ref.shape[-1] // HEAD_DIM):
        sl = slice(h * HEAD_DIM, (h + 1) * HEAD_DIM)
        qb = (prep(zq_ref[0, :, sl], gq_ref[...]) * scale).astype(BF16)
        q_ref[0, :, sl] = qb
        qmax = jnp.maximum(qmax, max_norm2(qb))
    stats = [qmax]
    for h in range(zk_ref.shape[-1] // HEAD_DIM):
        sl = slice(h * HEAD_DIM, (h + 1) * HEAD_DIM)
        kb = prep(zk_ref[0, :, sl], gk_ref[...]).astype(BF16)
        k_ref[0, :, 2 * h * HEAD_DIM:(2 * h + 1) * HEAD_DIM] = kb
        k_ref[0, :, (2 * h + 1) * HEAD_DIM:(2 * h + 2) * HEAD_DIM] = one_hot0
        v_ref[0, :, 2 * h * HEAD_DIM:(2 * h + 1) * HEAD_DIM] = zv_ref[0, :, sl].astype(BF16)
        v_ref[0, :, (2 * h + 1) * HEAD_DIM:(2 * h + 2) * HEAD_DIM] = one_hot0
        stats.append(max_norm2(kb))
    row = lax.broadcasted_iota(jnp.int32, (8, LANES), 0)
    st = jnp.zeros((8, LANES), F32)
    for r, val in enumerate(stats):
        st = jnp.where(row == r, val, st)
    st_ref[0, 0] = st


def qkv_prep(z3, cosf, sinf, gq, gk, att_w, kv_w):
    B, L, _ = z3.shape
    tl = min(PREP_TILE, L)
    off_k = att_w // kv_w
    body = functools.partial(_qkv_prep_body, scale=HEAD_DIM ** -0.5 * math.log2(math.e))
    return pl.pallas_call(
        body,
        grid=(B, L // tl),
        in_specs=[pl.BlockSpec((1, tl, att_w), lambda b, i: (b, i, 0)),
                  pl.BlockSpec((1, tl, kv_w), lambda b, i: (b, i, off_k)),
                  pl.BlockSpec((1, tl, kv_w), lambda b, i: (b, i, off_k + 1)),
                  pl.BlockSpec((tl, HEAD_DIM), lambda b, i: (i, 0)),
                  pl.BlockSpec((tl, HEAD_DIM), lambda b, i: (i, 0)),
                  pl.BlockSpec((1, HEAD_DIM), lambda b, i: (0, 0)),
                  pl.BlockSpec((1, HEAD_DIM), lambda b, i: (0, 0))],
        out_specs=[pl.BlockSpec((1, tl, att_w), lambda b, i: (b, i, 0)),
                   pl.BlockSpec((1, tl, 2 * kv_w), lambda b, i: (b, i, 0)),
                   pl.BlockSpec((1, tl, 2 * kv_w), lambda b, i: (b, i, 0)),
                   pl.BlockSpec((1, 1, 8, LANES), lambda b, i: (b, i, 0, 0))],
        out_shape=[jax.ShapeDtypeStruct((B, L, att_w), BF16),
                   jax.ShapeDtypeStruct((B, L, 2 * kv_w), BF16),
                   jax.ShapeDtypeStruct((B, L, 2 * kv_w), BF16),
                   jax.ShapeDtypeStruct((B, L // tl, 8, LANES), F32)],
        compiler_params=_params("parallel", "parallel"),
        name="qkv_prep",
    )(z3, z3, z3, cosf, sinf, gq.reshape(1, HEAD_DIM), gk.reshape(1, HEAD_DIM))


def _flash_body(fixed_ref, kmax_ref, q_ref, k_ref, v_ref, o_ref, qx_ref, acc_ref, m_ref, *, n_k, n_kv):
    b, g, kk = pl.program_id(0), pl.program_id(1), pl.program_id(3)
    nt_dims = (((1,), (1,)), ((), ()))
    tq = qx_ref.shape[1]

    @pl.when(kk == 0)
    def _():
        acc_ref[...] = jnp.zeros(acc_ref.shape, F32)
        m_ref[...] = jnp.full(m_ref.shape, MASK_VALUE, F32)
        kmax = kmax_ref[b * n_kv + g]
        lane0 = lax.broadcasted_iota(jnp.int32, (tq, HEAD_DIM), 1) == 0
        for h in range(KV_GROUP):
            q = q_ref[0, :, h * HEAD_DIM:(h + 1) * HEAD_DIM]
            qf = q.astype(F32)
            bound = jnp.sqrt(jnp.sum(qf * qf, axis=-1, keepdims=True)) * kmax
            qx_ref[h, :, :HEAD_DIM] = q
            qx_ref[h, :, HEAD_DIM:] = jnp.where(lane0, -bound, 0.0).astype(BF16)

    k = k_ref[0]
    v = v_ref[0]

    @pl.when(fixed_ref[0] == 1)
    def _():
        for h in range(KV_GROUP):
            s = lax.dot_general(qx_ref[h], k, nt_dims, preferred_element_type=F32)
            acc_ref[h] += jnp.dot(jnp.exp2(s).astype(BF16), v, preferred_element_type=F32)

    @pl.when(fixed_ref[0] == 0)
    def _():
        for h in range(KV_GROUP):
            s = lax.dot_general(qx_ref[h, :, :HEAD_DIM], k[:, :HEAD_DIM], nt_dims, preferred_element_type=F32)
            m_prev = m_ref[h]
            m_new = jnp.maximum(m_prev, jnp.max(s, axis=-1, keepdims=True))
            p = jnp.exp2(s - m_new).astype(BF16)
            acc_ref[h] = jnp.exp2(m_prev - m_new) * acc_ref[h] + jnp.dot(p, v, preferred_element_type=F32)
            m_ref[h] = m_new

    @pl.when(kk == n_k - 1)
    def _():
        for h in range(KV_GROUP):
            a = acc_ref[h]
            o_ref[0, :, h * HEAD_DIM:(h + 1) * HEAD_DIM] = (
                a[:, :HEAD_DIM] / a[:, HEAD_DIM:HEAD_DIM + 1]).astype(o_ref.dtype)


def flash_attention(q, k2, v2, stats):
    B, L, att_w = q.shape
    n_kv = k2.shape[-1] // (2 * HEAD_DIM)
    tq = min(ATT_TQ, L)
    tk = min(ATT_TK, L)
    gw = KV_GROUP * HEAD_DIM
    smax = jnp.sqrt(jnp.max(stats[:, :, :1 + n_kv, 0], axis=1))
    kmax = smax[:, 1:].reshape(B * n_kv)
    fixed = (2.0 * jnp.max(smax[:, :1] * smax[:, 1:]) < EXP2_SAFE_RANGE).astype(jnp.int32).reshape(1)
    return pl.pallas_call(
        functools.partial(_flash_body, n_k=L // tk, n_kv=n_kv),
        grid_spec=pltpu.PrefetchScalarGridSpec(
            num_scalar_prefetch=2,
            grid=(B, n_kv, L // tq, L // tk),
            in_specs=[pl.BlockSpec((1, tq, gw), lambda b, g, i, kk, *_: (b, i, g)),
                      pl.BlockSpec((1, tk, 2 * HEAD_DIM), lambda b, g, i, kk, *_: (b, kk, g)),
                      pl.BlockSpec((1, tk, 2 * HEAD_DIM), lambda b, g, i, kk, *_: (b, kk, g))],
            out_specs=pl.BlockSpec((1, tq, gw), lambda b, g, i, kk, *_: (b, i, g)),
            scratch_shapes=[pltpu.VMEM((KV_GROUP, tq, 2 * HEAD_DIM), BF16),
                            pltpu.VMEM((KV_GROUP, tq, 2 * HEAD_DIM), F32),
                            pltpu.VMEM((KV_GROUP, tq, 1), F32)]),
        out_shape=jax.ShapeDtypeStruct((B, L, att_w), BF16),
        compiler_params=_params("parallel", "parallel", "parallel", "arbitrary"),
        name="flash_attention",
    )(fixed, kmax, q, k2, v2)


def _mem_attn_body(zm_ref, kv_ref, gq_ref, gk_ref, o_ref, *, scale):
    mem_w = zm_ref.shape[-1]
    for h in range(mem_w // HEAD_DIM):
        sl = slice(h * HEAD_DIM, (h + 1) * HEAD_DIM)
        q = (_rms(zm_ref[0, :, sl], gq_ref[...]) * scale).astype(BF16)
        k = _rms(kv_ref[0, :, sl], gk_ref[...]).astype(BF16)
        v = kv_ref[0, :, mem_w + h * HEAD_DIM:mem_w + (h + 1) * HEAD_DIM].astype(BF16)
        s = lax.dot_general(q, k, (((1,), (1,)), ((), ())), preferred_element_type=F32)
        p = jnp.exp(s - jnp.max(s, axis=-1, keepdims=True))
        o = jnp.dot(p.astype(BF16), v, preferred_element_type=F32)
        o_ref[0, :, sl] = (o / jnp.sum(p, axis=-1, keepdims=True)).astype(o_ref.dtype)


def mem_attention(z3, kv, gq, gk, mem_w, off_mq):
    B, L, _ = z3.shape
    M = kv.shape[1]
    tl = min(PREP_TILE, L)
    body = functools.partial(_mem_attn_body, scale=HEAD_DIM ** -0.5)
    return pl.pallas_call(
        body,
        grid=(B, L // tl),
        in_specs=[pl.BlockSpec((1, tl, mem_w), lambda b, i: (b, i, off_mq // mem_w)),
                  pl.BlockSpec((1, M, 2 * mem_w), lambda b, i: (b, 0, 0)),
                  pl.BlockSpec((1, HEAD_DIM), lambda b, i: (0, 0)),
                  pl.BlockSpec((1, HEAD_DIM), lambda b, i: (0, 0))],
        out_specs=pl.BlockSpec((1, tl, mem_w), lambda b, i: (b, i, 0)),
        out_shape=jax.ShapeDtypeStruct((B, L, mem_w), BF16),
        compiler_params=_params("parallel", "parallel"),
        name="mem_attention",
    )(z3, kv, gq.reshape(1, HEAD_DIM), gk.reshape(1, HEAD_DIM))


def _hyena_pre_body(z_ref, prev_ref, next_ref, w_ref, b_ref, vin_ref, x0_ref, *, n_tiles):
    i = pl.program_id(1)
    x = z_ref[0]
    tl, width = x.shape
    hy_w = width // 3
    row = lax.broadcasted_iota(jnp.int32, x.shape, 0)
    prev_row = jnp.where(i > 0, prev_ref[0, 7:8, :], 0.0)
    next_row = jnp.where(i < n_tiles - 1, next_ref[0, 0:1, :], 0.0)
    x_prev = jnp.where(row == 0, prev_row, pltpu.roll(x, 1, 0))
    x_next = jnp.where(row == tl - 1, next_row, pltpu.roll(x, tl - 1, 0))
    u = w_ref[0:1, :] * x_prev + w_ref[1:2, :] * x + w_ref[2:3, :] * x_next + b_ref[...]
    vin_ref[0] = u[:, 2 * hy_w:] * u[:, hy_w:2 * hy_w]
    x0_ref[0] = u[:, :hy_w]


def hyena_pre(z3, conv_w, conv_b, hy_w, off_hy):
    B, L, _ = z3.shape
    width = 3 * hy_w
    tl = min(PREP_TILE, L)
    cb = off_hy // width
    r8 = tl // 8
    last8 = L // 8 - 1
    return pl.pallas_call(
        functools.partial(_hyena_pre_body, n_tiles=L // tl),
        grid=(B, L // tl),
        in_specs=[pl.BlockSpec((1, tl, width), lambda b, i: (b, i, cb)),
                  pl.BlockSpec((1, 8, width), lambda b, i: (b, jnp.maximum(i * r8 - 1, 0), cb)),
                  pl.BlockSpec((1, 8, width), lambda b, i: (b, jnp.minimum((i + 1) * r8, last8), cb)),
                  pl.BlockSpec((HY_SHORT, width), lambda b, i: (0, 0)),
                  pl.BlockSpec((1, width), lambda b, i: (0, 0))],
        out_specs=[pl.BlockSpec((1, tl, hy_w), lambda b, i: (b, i, 0)),
                   pl.BlockSpec((1, tl, hy_w), lambda b, i: (b, i, 0))],
        out_shape=[jax.ShapeDtypeStruct((B, L, hy_w), F32),
                   jax.ShapeDtypeStruct((B, L, hy_w), F32)],
        compiler_params=_params("parallel", "parallel"),
        name="hyena_pre",
    )(z3, z3, z3, conv_w.reshape(HY_SHORT, width), conv_b.reshape(1, width))


def _hyena_filter_body(e_ref, w1_ref, b1_ref, w2_ref, b2_ref, w3_ref, b3_ref, freq_ref, dec_ref,
                       kf_ref, ss_ref):
    hp = lax.Precision.HIGHEST
    e = e_ref[...]
    t = e[:, 0:1]
    valid = e[:, HY_EMB:HY_EMB + 1]
    freq = freq_ref[...]
    h = jnp.sin(freq * (jnp.dot(e, w1_ref[...], precision=hp, preferred_element_type=F32) + b1_ref[...]))
    h = jnp.sin(freq * (jnp.dot(h, w2_ref[...], precision=hp, preferred_element_type=F32) + b2_ref[...]))
    h = jnp.dot(h, w3_ref[...], precision=hp, preferred_element_type=F32) + b3_ref[...]
    kf = h * jnp.exp(-t * jnp.abs(dec_ref[...])) * valid
    kf_ref[...] = kf

    @pl.when(pl.program_id(0) == 0)
    def _():
        ss_ref[...] = jnp.zeros(ss_ref.shape, F32)

    ss_ref[...] += jnp.sum(kf * kf, axis=0, keepdims=True)


def hyena_filter(emb2, w1p, b1, w2, b2, w3, b3, freq, decay, hy_w):
    n_rows = emb2.shape[0]
    hidden = w2.shape[0]
    tl = min(PREP_TILE, n_rows // 2)
    nblk = n_rows // tl
    half = nblk // 2

    def dirmap(i):
        return (0, jnp.where(i >= half, 1, 0))

    const = lambda i: (0, 0)
    return pl.pallas_call(
        _hyena_filter_body,
        grid=(nblk,),
        in_specs=[pl.BlockSpec((tl, EMB_PAD), lambda i: (i, 0)),
                  pl.BlockSpec((EMB_PAD, hidden), const),
                  pl.BlockSpec((1, hidden), const),
                  pl.BlockSpec((hidden, hidden), const),
                  pl.BlockSpec((1, hidden), const),
                  pl.BlockSpec((hidden, hy_w), dirmap),
                  pl.BlockSpec((1, hy_w), dirmap),
                  pl.BlockSpec((1, hidden), const),
                  pl.BlockSpec((1, hy_w), dirmap)],
        out_specs=[pl.BlockSpec((tl, hy_w), lambda i: (i, 0)),
                   pl.BlockSpec((1, hy_w), const)],
        out_shape=[jax.ShapeDtypeStruct((n_rows, hy_w), F32),
                   jax.ShapeDtypeStruct((1, hy_w), F32)],
        compiler_params=_params("arbitrary"),
        name="hyena_filter",
    )(emb2, w1p, b1.reshape(1, hidden), w2, b2.reshape(1, hidden), w3, b3.reshape(1, -1),
      freq.reshape(1, hidden), decay.reshape(1, -1))


def _left_matmul_body(m_ref, x_ref, o_ref):
    o_ref[0] = jnp.dot(m_ref[...], x_ref[0].astype(BF16), preferred_element_type=F32)


def _left_matmul_post_body(m_ref, x_ref, vin_ref, x0_ref, bias_ref, o_ref):
    y = jnp.dot(m_ref[...], x_ref[0].astype(BF16), preferred_element_type=F32)
    vin = vin_ref[0]
    o_ref[0] = ((y + vin * bias_ref[...]) * x0_ref[0]).astype(o_ref.dtype)


def left_matmul(mat, x, post=None):
    P, K, cols = x.shape
    R = mat.shape[0]
    tc = min(DFT_COLS, cols)
    in_specs = [pl.BlockSpec((R, K), lambda p, c: (0, 0)),
                pl.BlockSpec((1, K, tc), lambda p, c: (p, 0, c))]
    args = [mat, x]
    if post is None:
        body, dtype = _left_matmul_body, F32
    else:
        vin, x0, bias_row = post
        body, dtype = _left_matmul_post_body, BF16
        in_specs += [pl.BlockSpec((1, R, tc), lambda p, c: (p, 0, c)),
                     pl.BlockSpec((1, R, tc), lambda p, c: (p, 0, c)),
                     pl.BlockSpec((1, tc), lambda p, c: (0, c))]
        args += [vin, x0, bias_row]
    return pl.pallas_call(
        body,
        grid=(P, cols // tc),
        in_specs=in_specs,
        out_specs=pl.BlockSpec((1, R, tc), lambda p, c: (p, 0, c)),
        out_shape=jax.ShapeDtypeStruct((P, R, cols), dtype),
        compiler_params=_params("parallel", "parallel"),
        name="dft_outer",
    )(*args)


def _filter_spectrum_body(g_ref, a_ref, s_ref, k_ref):
    n2 = a_ref.shape[-2]
    a = jnp.concatenate([a_ref[0, 0, 0], a_ref[0, 1, 0]], axis=0).astype(BF16)
    y = jnp.dot(g_ref[0], a, preferred_element_type=F32) * s_ref[...]
    k_ref[0, 0] = y[:n2]
    k_ref[1, 0] = y[n2:]


def filter_spectrum(gs, a5, scale_row):
    _, _, n1, n2, C = a5.shape
    return pl.pallas_call(
        _filter_spectrum_body,
        grid=(n1,),
        in_specs=[pl.BlockSpec((1, 2 * n2, 2 * n2), lambda k: (k, 0, 0)),
                  pl.BlockSpec((1, 2, 1, n2, C), lambda k: (0, 0, k, 0, 0)),
                  pl.BlockSpec((1, C), lambda k: (0, 0))],
        out_specs=pl.BlockSpec((2, 1, n2, C), lambda k: (0, k, 0, 0)),
        out_shape=jax.ShapeDtypeStruct((2, n1, n2, C), F32),
        compiler_params=_params("parallel"),
        name="filter_spectrum",
    )(gs, a5, scale_row)


def _dft_inner_body(g_ref, gi_ref, a_ref, k_ref, p_ref):
    n2 = a_ref.shape[-2]
    a = jnp.concatenate([a_ref[0, 0, 0], a_ref[0, 1, 0]], axis=0).astype(BF16)
    y = jnp.dot(g_ref[0], a, preferred_element_type=F32)
    yr, yi = y[:n2], y[n2:]
    kr, ki = k_ref[0, 0], k_ref[1, 0]
    z = jnp.concatenate([yr * kr - yi * ki, yr * ki + yi * kr], axis=0).astype(BF16)
    p = jnp.dot(gi_ref[0], z, preferred_element_type=F32)
    p_ref[0, 0, 0] = p[:n2]
    p_ref[0, 1, 0] = p[n2:]


def dft_inner(gs, gis, a5, kspec):
    P, _, n1, n2, C = a5.shape
    gspec = pl.BlockSpec((1, 2 * n2, 2 * n2), lambda k, p: (k, 0, 0))
    return pl.pallas_call(
        _dft_inner_body,
        grid=(n1, P),
        in_specs=[gspec, gspec,
                  pl.BlockSpec((1, 2, 1, n2, C), lambda k, p: (p, 0, k, 0, 0)),
                  pl.BlockSpec((2, 1, n2, C), lambda k, p: (0, k, 0, 0))],
        out_specs=pl.BlockSpec((1, 2, 1, n2, C), lambda k, p: (p, 0, k, 0, 0)),
        out_shape=jax.ShapeDtypeStruct((P, 2, n1, n2, C), F32),
        compiler_params=_params("parallel", "parallel"),
        name="dft_inner",
    )(gs, gis, a5, kspec)


def _dft_tables(n1, n2):
    n = n1 * n2
    h = n1 // 2
    a = np.arange(n1)
    ang1 = -2.0 * np.pi * ((a[:, None] * a[None, :]) % n1) / n1
    f1r, f1i = np.cos(ang1), np.sin(ang1)
    fwd_sig = np.block([[f1r[:, :h], -f1i[:, :h]], [f1i[:, :h], f1r[:, :h]]])
    fwd_flt = np.concatenate([f1r, f1i], axis=0)
    inv = np.block([[f1r[:h], f1i[:h]], [-f1i[:h], f1r[:h]]])
    b = np.arange(n2)
    ang_t = -2.0 * np.pi * ((np.arange(n1)[:, None] * b[None, :]) % n) / n
    ang_f = -2.0 * np.pi * ((b[:, None] * b[None, :]) % n2) / n2
    twr, twi = jnp.asarray(np.cos(ang_t), F32)[:, None, :], jnp.asarray(np.sin(ang_t), F32)[:, None, :]
    f2r, f2i = jnp.asarray(np.cos(ang_f), F32)[None], jnp.asarray(np.sin(ang_f), F32)[None]
    gr = twr * f2r - twi * f2i
    gi = twr * f2i + twi * f2r
    gs = jnp.concatenate([jnp.concatenate([gr, -gi], axis=2), jnp.concatenate([gi, gr], axis=2)], axis=1)
    gis = jnp.swapaxes(gs, 1, 2) * (1.0 / n)
    cvt = lambda m: jnp.asarray(m, dtype=F32).astype(BF16)
    return cvt(fwd_sig), cvt(fwd_flt), cvt(inv), gs.astype(BF16), gis.astype(BF16)


def _filter_embedding(L):
    t = jnp.linspace(0.0, 1.0, L, dtype=F32)
    w = 2.0 * math.pi * jnp.arange(L, dtype=F32) / L
    bands = jnp.linspace(1e-4, HY_BANDS - 1, HY_BANDS, dtype=F32)
    fw = w[:, None] * bands[None, :]
    emb = jnp.concatenate([t[:, None], jnp.cos(fw), -jnp.sin(fw)], axis=-1)
    valid = jnp.ones((L, 1), F32)
    emb = jnp.concatenate([emb, valid, jnp.zeros((L, EMB_PAD - HY_EMB - 1), F32)], axis=-1)
    back = jnp.concatenate([jnp.zeros((1, EMB_PAD), F32), emb[:0:-1]], axis=0)
    return jnp.concatenate([emb, back], axis=0)


def hyena_branch(z3, lp, hy_w, off_hy):
    B, L, _ = z3.shape
    n = 2 * L
    n2 = DFT_N2
    n1 = n // n2
    P = B // 2
    vin, x0 = hyena_pre(z3, lp['hy_conv_w'], lp['hy_conv_b'], hy_w, off_hy)
    fwd_sig, fwd_flt, inv, gs, gis = _dft_tables(n1, n2)

    hidden = lp['hf_w2'].shape[0]
    w1p = jnp.concatenate([lp['hf_w1'], jnp.zeros((EMB_PAD - HY_EMB, hidden), F32)], axis=0)
    kf, ss = hyena_filter(_filter_embedding(L), w1p, lp['hf_b1'], lp['hf_w2'], lp['hf_b2'], lp['hf_w3'],
                          lp['hf_b3'], lp['hf_freq'], lp['hy_decay'], hy_w)
    kscale = lax.rsqrt(ss + EPS)
    ka = left_matmul(fwd_flt, kf.reshape(1, n1, n2 * hy_w))
    kspec = filter_spectrum(gs, ka.reshape(1, 2, n1, n2, hy_w), kscale)

    a = left_matmul(fwd_sig, vin.reshape(P, n1, n2 * hy_w))
    pm = dft_inner(gs, gis, a.reshape(P, 2, n1, n2, hy_w), kspec)
    pstack = pm.reshape(P, 2 * n1, n2 * hy_w)
    bias_row = jnp.tile(lp['hy_bias'].reshape(1, hy_w), (1, n2))
    out = left_matmul(inv, pstack, post=(vin.reshape(P, n1, n2 * hy_w), x0.reshape(P, n1, n2 * hy_w), bias_row))
    return out.reshape(B, L, hy_w)


def _merge_body(a_ref, h_ref, m_ref, wa_ref, wh_ref, wm_ref, zg0_ref, zg1_ref, zg2_ref,
                bg0_ref, bg1_ref, bg2_ref, o_ref):
    def term(x_ref, w_ref, zg_ref, bg_ref):
        gate = jax.nn.sigmoid(zg_ref[...] + bg_ref[...])
        return gate * jnp.dot(x_ref[...], w_ref[...], preferred_element_type=F32)

    acc = term(a_ref, wa_ref, zg0_ref, bg0_ref)
    acc = acc + term(h_ref, wh_ref, zg1_ref, bg1_ref)
    acc = acc + term(m_ref, wm_ref, zg2_ref, bg2_ref)
    o_ref[...] = acc.astype(o_ref.dtype)


def gated_merge(a_out, h_out, m_out, wa, wh, wm, z2, b_gate, off_gate):
    T = a_out.shape[0]
    D = wa.shape[1]
    tm = min(ROW_TILE, T)
    tn = COL_TILE
    nj = D // tn
    gb = off_gate // tn

    def xspec(width):
        return pl.BlockSpec((tm, width), lambda i, j: (i, 0))

    def wspec(width):
        return pl.BlockSpec((width, tn), lambda i, j: (0, j))

    def zgspec(b):
        return pl.BlockSpec((tm, tn), lambda i, j: (i, gb + b * nj + j))

    def bgspec(b):
        return pl.BlockSpec((1, tn), lambda i, j: (0, b * nj + j))

    bg = b_gate.reshape(1, -1)
    return pl.pallas_call(
        _merge_body,
        grid=(T // tm, nj),
        in_specs=[xspec(a_out.shape[1]), xspec(h_out.shape[1]), xspec(m_out.shape[1]),
                  wspec(wa.shape[0]), wspec(wh.shape[0]), wspec(wm.shape[0]),
                  zgspec(0), zgspec(1), zgspec(2), bgspec(0), bgspec(1), bgspec(2)],
        out_specs=pl.BlockSpec((tm, tn), lambda i, j: (i, j)),
        out_shape=jax.ShapeDtypeStruct((T, D), BF16),
        compiler_params=_params("parallel", "parallel"),
        name="gated_merge",
    )(a_out, h_out, m_out, wa, wh, wm, z2, z2, z2, bg, bg, bg)


def _slab_store(ref, x):
    rows, width = x.shape
    n = width // LANES
    for s in range(n):
        ref[pl.ds(s, rows, stride=n), :] = x[:, s * LANES:(s + 1) * LANES]


def _slab_load(ref, n):
    rows = ref.shape[0] // n
    return jnp.concatenate([ref[pl.ds(s, rows, stride=n), :] for s in range(n)], axis=-1)


def _out_router_body(mg_ref, x_ref, wo_ref, g2_ref, whi_ref, wlo_ref, br_ref, x1_ref, h2_ref, lg_ref):
    x1 = x_ref[...] + jnp.dot(mg_ref[...], wo_ref[...], preferred_element_type=F32)
    x1_ref[...] = x1
    h = _rms(x1, g2_ref[...])
    h_hi = h.astype(BF16)
    _slab_store(h2_ref, h_hi.astype(F32))
    h_lo = (h - h_hi.astype(F32)).astype(BF16)
    lg = jnp.dot(h_hi, whi_ref[...], preferred_element_type=F32)
    lg = lg + jnp.dot(h_lo, whi_ref[...], preferred_element_type=F32)
    lg = lg + jnp.dot(h_hi, wlo_ref[...], preferred_element_type=F32)
    lg_ref[...] = lg + br_ref[...]


def out_router(merged, x2, w_out, g2, wr_hi, wr_lo, br):
    T, D = x2.shape
    tm = min(PREP_TILE, T)
    const = lambda i: (0, 0)
    row = lambda i: (i, 0)
    return pl.pallas_call(
        _out_router_body,
        grid=(T // tm,),
        in_specs=[pl.BlockSpec((tm, D), row), pl.BlockSpec((tm, D), row),
                  pl.BlockSpec((D, D), const), pl.BlockSpec((1, D), const),
                  pl.BlockSpec((D, ROUTER_PAD), const), pl.BlockSpec((D, ROUTER_PAD), const),
                  pl.BlockSpec((1, ROUTER_PAD), const)],
        out_specs=[pl.BlockSpec((tm, D), row), pl.BlockSpec((tm * (D // LANES), LANES), row),
                   pl.BlockSpec((tm, ROUTER_PAD), row)],
        out_shape=[jax.ShapeDtypeStruct((T, D), F32), jax.ShapeDtypeStruct((T * (D // LANES), LANES), F32),
                   jax.ShapeDtypeStruct((T, ROUTER_PAD), F32)],
        compiler_params=_params("parallel"),
        name="out_router",
    )(merged, x2, w_out, g2.reshape(1, D), wr_hi, wr_lo, br)


def _row_copy(src_ref, src_row, dst_ref, dst_row, n, sem):
    src = src_ref.at[pl.ds(pl.multiple_of(src_row * n, n), n), :]
    dst = dst_ref.at[pl.ds(pl.multiple_of(dst_row * n, n), n), :]
    return pltpu.make_async_copy(src, dst, sem)


def _gather_body(idx_ref, src_ref, o_ref, sem, *, n):
    rows = o_ref.shape[0] // n
    base = pl.program_id(0) * rows

    def copy(r):
        return _row_copy(src_ref, idx_ref[base + r], o_ref, r, n, sem)

    def issue(r, carry):
        copy(r).start()
        return carry

    def drain(r, carry):
        copy(r).wait()
        return carry

    lax.fori_loop(0, rows, issue, 0)
    lax.fori_loop(0, rows, drain, 0)


def gather_rows(idx, src_slab, n):
    P = idx.shape[0]
    rows = GATHER_ROWS
    return pl.pallas_call(
        functools.partial(_gather_body, n=n),
        grid_spec=pltpu.PrefetchScalarGridSpec(
            num_scalar_prefetch=1,
            grid=(P // rows,),
            in_specs=[pl.BlockSpec(memory_space=pl.ANY)],
            out_specs=pl.BlockSpec((rows * n, LANES), lambda i, idx: (i, 0)),
            scratch_shapes=[pltpu.SemaphoreType.DMA(())]),
        out_shape=jax.ShapeDtypeStruct((P * n, LANES), src_slab.dtype),
        compiler_params=_params("arbitrary"),
        name="gather_rows",
    )(idx, src_slab)


def _moe_body(blk_e_ref, n_used_ref, x_ref, wg_ref, wu_ref, wd_ref, cw_ref, o_ref):
    del blk_e_ref
    i = pl.program_id(0)

    n = wg_ref.shape[1] // LANES

    @pl.when(i < n_used_ref[0])
    def _():
        x = _slab_load(x_ref, n).astype(BF16)
        g = jnp.dot(x, wg_ref[0], preferred_element_type=F32)
        u = jnp.dot(x, wu_ref[0], preferred_element_type=F32)
        a = (g * jax.nn.sigmoid(g) * u).astype(BF16)
        _slab_store(o_ref, jnp.dot(a, wd_ref[0], preferred_element_type=F32) * cw_ref[...])

    @pl.when(i >= n_used_ref[0])
    def _():
        o_ref[...] = jnp.zeros(o_ref.shape, o_ref.dtype)


def moe_experts(blk_e, n_used, xs_slab, wg, wu, wd, cw):
    _, D, De = wg.shape
    n = D // LANES
    P = xs_slab.shape[0] // n
    tm = MOE_TM
    return pl.pallas_call(
        _moe_body,
        grid_spec=pltpu.PrefetchScalarGridSpec(
            num_scalar_prefetch=2,
            grid=(P // tm,),
            in_specs=[pl.BlockSpec((tm * n, LANES), lambda i, be, nu: (i, 0)),
                      pl.BlockSpec((1, D, De), lambda i, be, nu: (be[i], 0, 0)),
                      pl.BlockSpec((1, D, De), lambda i, be, nu: (be[i], 0, 0)),
                      pl.BlockSpec((1, De, D), lambda i, be, nu: (be[i], 0, 0)),
                      pl.BlockSpec((tm, 1), lambda i, be, nu: (i, 0))],
            out_specs=pl.BlockSpec((tm * n, LANES), lambda i, be, nu: (i, 0))),
        out_shape=jax.ShapeDtypeStruct((P * n, LANES), F32),
        compiler_params=_params("arbitrary"),
        name="moe_experts",
    )(blk_e, n_used, xs_slab, wg, wu, wd, cw)


def _combine_body(pos_ref, x1_ref, yb_ref, o_ref, buf0, buf1, sem):
    tg, D = x1_ref.shape
    n = D // LANES
    base = pl.program_id(0) * tg

    def copies(r):
        a = TOP_K * (base + r)
        return (_row_copy(yb_ref, pos_ref[a], buf0, r, n, sem),
                _row_copy(yb_ref, pos_ref[a + 1], buf1, r, n, sem))

    def issue(r, carry):
        c0, c1 = copies(r)
        c0.start()
        c1.start()
        return carry

    def drain(r, carry):
        c0, c1 = copies(r)
        c0.wait()
        c1.wait()
        return carry

    lax.fori_loop(0, tg, issue, 0)
    lax.fori_loop(0, tg, drain, 0)
    o_ref[...] = x1_ref[...] + _slab_load(buf0, n) + _slab_load(buf1, n)


def moe_combine(pos_flat, x1, yb_slab):
    T, D = x1.shape
    tg = MOE_TM
    n = D // LANES
    return pl.pallas_call(
        _combine_body,
        grid_spec=pltpu.PrefetchScalarGridSpec(
            num_scalar_prefetch=1,
            grid=(T // tg,),
            in_specs=[pl.BlockSpec((tg, D), lambda i, pos: (i, 0)),
                      pl.BlockSpec(memory_space=pl.ANY)],
            out_specs=pl.BlockSpec((tg, D), lambda i, pos: (i, 0)),
            scratch_shapes=[pltpu.VMEM((tg * n, LANES), F32), pltpu.VMEM((tg * n, LANES), F32),
                            pltpu.SemaphoreType.DMA(())]),
        out_shape=jax.ShapeDtypeStruct((T, D), F32),
        compiler_params=_params("arbitrary"),
        name="moe_combine",
    )(pos_flat, x1, yb_slab)


def _route(logits):
    T = logits.shape[0]
    g_logits = logits[:, :N_GROUPS]
    e_logits = logits[:, N_GROUPS:N_GROUPS + N_EXPERTS].reshape(T, N_GROUPS, EXPERTS_PER_GROUP)
    g_p, g_idx = lax.top_k(jax.nn.softmax(g_logits, axis=-1), 1)
    idx = jnp.broadcast_to(g_idx[:, :, None], (T, 1, EXPERTS_PER_GROUP))
    e_in = jnp.take_along_axis(e_logits, idx, axis=1)[:, 0]
    top_v, top_i = lax.top_k(e_in, TOP_K)
    comb = jax.nn.softmax(top_v, axis=-1) * g_p
    e_flat = (g_idx * EXPERTS_PER_GROUP + top_i).reshape(-1).astype(jnp.int32)
    w_flat = comb.reshape(-1)
    A = T * TOP_K
    tok_flat = jnp.repeat(jnp.arange(T, dtype=jnp.int32), TOP_K)
    unit = max(MOE_TM, GATHER_ROWS)
    P = (A + N_EXPERTS * (MOE_TM - 1) + unit - 1) // unit * unit
    n_blk = P // MOE_TM
    order = jnp.argsort(e_flat)
    e_s = e_flat[order]
    counts = jnp.bincount(e_flat, length=N_EXPERTS).astype(jnp.int32)
    starts = jnp.cumsum(counts) - counts
    padded = (counts + MOE_TM - 1) // MOE_TM * MOE_TM
    pends = jnp.cumsum(padded)
    pstarts = pends - padded
    dest = pstarts[e_s] + jnp.arange(A, dtype=jnp.int32) - starts[e_s]
    row_tok = jnp.zeros((P,), jnp.int32).at[dest].set(tok_flat[order])
    row_w = jnp.zeros((P,), F32).at[dest].set(w_flat[order])
    pos_flat = jnp.zeros((A,), jnp.int32).at[order].set(dest)
    blk_e = jnp.minimum(jnp.searchsorted(pends, jnp.arange(n_blk, dtype=jnp.int32) * MOE_TM, side='right'),
                        N_EXPERTS - 1).astype(jnp.int32)
    n_used = (pends[-1:] // MOE_TM).astype(jnp.int32)
    return row_tok, row_w, pos_flat, blk_e, n_used


def _rope_tables(L):
    rows = L // GRID_W
    pairs = HEAD_DIM // 4
    row = jnp.repeat(jnp.arange(rows, dtype=F32), GRID_W)
    col = jnp.tile(jnp.arange(GRID_W, dtype=F32), rows)
    inv = jnp.power(ROPE_THETA, -jnp.arange(pairs, dtype=F32) / pairs)
    ang = jnp.concatenate([row[:, None] * inv, col[:, None] * inv], axis=-1)
    cos, sin = jnp.cos(ang), jnp.sin(ang)
    cosf = jnp.repeat(cos, 2, axis=-1)
    sinf = jnp.stack([-sin, sin], axis=-1).reshape(L, HEAD_DIM)
    return cosf, sinf


def _trunk(x, mem, lp, wb):
    B, L, D = x.shape
    T = B * L
    att_w = lp['w_br_attn'].shape[0]
    hy_w = lp['w_br_hyena'].shape[0]
    mem_w = lp['w_br_mem'].shape[0]
    kv_w = (lp['w_in'].shape[1] - att_w - 3 * hy_w - mem_w - 3 * D) // 2
    off_hy = att_w + 2 * kv_w
    off_mq = off_hy + 3 * hy_w
    off_gate = off_mq + mem_w
    x2 = x.reshape(T, D)

    z2 = normed_matmul(x2, lp['norm1_g'], wb['w_in'])
    z3 = z2.reshape(B, L, -1)
    cosf, sinf = _rope_tables(L)
    q, k2, v2, stats = qkv_prep(z3, cosf, sinf, lp['q_norm_g'], lp['k_norm_g'], att_w, kv_w)
    a_out = flash_attention(q, k2, v2, stats)
    h_out = hyena_branch(z3, lp, hy_w, off_hy)
    M = mem.shape[1]
    kv = normed_matmul(mem.reshape(B * M, D), lp['mem_norm_g'], wb['w_mem_kv']).reshape(B, M, -1)
    m_out = mem_attention(z3, kv, lp['mq_norm_g'], lp['mk_norm_g'], mem_w, off_mq)
    merged = gated_merge(a_out.reshape(T, att_w), h_out.reshape(T, hy_w), m_out.reshape(T, mem_w),
                         wb['w_br_attn'], wb['w_br_hyena'], wb['w_br_mem'], z2, lp['b_gate'], off_gate)
    x1, h2, logits = out_router(merged, x2, wb['w_out'], lp['norm2_g'], wb['wr_hi'], wb['wr_lo'], wb['br'])

    row_tok, row_w, pos_flat, blk_e, n_used = _route(logits)
    xs = gather_rows(row_tok, h2, D // LANES)
    yb = moe_experts(blk_e, n_used, xs, wb['w_gate_e'], wb['w_up_e'], wb['w_down_e'], row_w.reshape(-1, 1))
    y = moe_combine(pos_flat, x1, yb)
    return y.reshape(B, L, D)


def kernel(x_prompt, x_sample, mem_prompt, mem_sample, norm1_g, w_in, b_gate, q_norm_g, k_norm_g, hy_conv_w, hy_conv_b, hf_w1, hf_b1, hf_w2, hf_b2, hf_w3, hf_b3, hf_freq, hy_decay, hy_bias, mem_norm_g, w_mem_kv, mq_norm_g, mk_norm_g, w_br_attn, w_br_hyena, w_br_mem, w_out, norm2_g, w_router_group, b_router_group, w_router_expert, b_router_expert, w_gate_e, w_up_e, w_down_e):
    params = dict(norm1_g=norm1_g, w_in=w_in, b_gate=b_gate, q_norm_g=q_norm_g, k_norm_g=k_norm_g,
                  hy_conv_w=hy_conv_w, hy_conv_b=hy_conv_b, hf_w1=hf_w1, hf_b1=hf_b1, hf_w2=hf_w2,
                  hf_b2=hf_b2, hf_w3=hf_w3, hf_b3=hf_b3, hf_freq=hf_freq, hy_decay=hy_decay,
                  hy_bias=hy_bias, mem_norm_g=mem_norm_g, w_mem_kv=w_mem_kv, mq_norm_g=mq_norm_g,
                  mk_norm_g=mk_norm_g, w_br_attn=w_br_attn, w_br_hyena=w_br_hyena, w_br_mem=w_br_mem,
                  w_out=w_out, norm2_g=norm2_g, w_router_group=w_router_group,
                  b_router_group=b_router_group, w_router_expert=w_router_expert,
                  b_router_expert=b_router_expert, w_gate_e=w_gate_e, w_up_e=w_up_e, w_down_e=w_down_e)
    depth = w_in.shape[0]
    xp, xs = x_prompt, x_sample
    for d in range(depth):
        lp = {name: arr[d] for name, arr in params.items()}
        D = lp['w_in'].shape[0]
        wb = {name: lp[name].astype(BF16) for name in
              ('w_in', 'w_mem_kv', 'w_br_attn', 'w_br_hyena', 'w_br_mem', 'w_out',
               'w_gate_e', 'w_up_e', 'w_down_e')}
        wr = jnp.concatenate([lp['w_router_group'], lp['w_router_expert'],
                              jnp.zeros((D, ROUTER_PAD - N_GROUPS - N_EXPERTS), F32)], axis=1)
        wb['wr_hi'] = wr.astype(BF16)
        wb['wr_lo'] = (wr - wb['wr_hi'].astype(F32)).astype(BF16)
        wb['br'] = jnp.concatenate([lp['b_router_group'], lp['b_router_expert'],
                                    jnp.zeros((ROUTER_PAD - N_GROUPS - N_EXPERTS,), F32)]).reshape(1, ROUTER_PAD)
        xp = _trunk(xp, mem_prompt, lp, wb)
        xs = _trunk(xs, mem_sample, lp, wb)
    return (xp, xs)
```

```python
import functools
import math

import jax
import jax.numpy as jnp
import numpy as np
from jax import lax
from jax.experimental import pallas as pl
from jax.experimental.pallas import tpu as pltpu

F32 = jnp.float32
BF16 = jnp.bfloat16

HEAD_DIM = 128
KV_GROUP = 4
GRID_W = 64
ROPE_THETA = 10000.0
HY_BANDS = 8
HY_EMB = 1 + 2 * HY_BANDS
HY_SHORT = 3
MEM_HEADS = 4
N_GROUPS = 4
EXPERTS_PER_GROUP = 8
N_EXPERTS = N_GROUPS * EXPERTS_PER_GROUP
TOP_K = 2
EPS = 1e-6
MASK_VALUE = -1e30
EXP2_SAFE_RANGE = 120.0

LANES = 128
MXU_DIM = 256
VMEM_LIMIT_BYTES = 52 * 1024 * 1024

ROW_TILE = 1024
COL_TILE = 512
PREP_TILE = 512
ATT_TQ = 512
ATT_TK = 2048
DFT_N2 = 256
DFT_COLS = 2048
MOE_TM = 256
GATHER_ROWS = 512
DMA_UNROLL = 8
EMB_PAD = 32
ROUTER_PAD = 128


def _params(*sem):
    return pltpu.CompilerParams(dimension_semantics=sem, vmem_limit_bytes=VMEM_LIMIT_BYTES)


def _rms(x, g):
    ms = jnp.mean(x * x, axis=-1, keepdims=True)
    return x * lax.rsqrt(ms + EPS) * g


def _normed_matmul_body(x_ref, g_ref, w_ref, o_ref, xn_ref):
    @pl.when(pl.program_id(1) == 0)
    def _():
        xn_ref[...] = _rms(x_ref[...], g_ref[...]).astype(BF16)

    o_ref[...] = jnp.dot(xn_ref[...], w_ref[...], preferred_element_type=F32)


def normed_matmul(x, g, w):
    T, D = x.shape
    N = w.shape[1]
    tm = min(ROW_TILE, T)
    tn = COL_TILE
    return pl.pallas_call(
        _normed_matmul_body,
        grid=(T // tm, N // tn),
        in_specs=[pl.BlockSpec((tm, D), lambda i, j: (i, 0)),
                  pl.BlockSpec((1, D), lambda i, j: (0, 0)),
                  pl.BlockSpec((D, tn), lambda i, j: (0, j))],
        out_specs=pl.BlockSpec((tm, tn), lambda i, j: (i, j)),
        out_shape=jax.ShapeDtypeStruct((T, N), F32),
        scratch_shapes=[pltpu.VMEM((tm, D), BF16)],
        compiler_params=_params("parallel", "arbitrary"),
        name="normed_matmul",
    )(x, g.reshape(1, D), w)


def _qkv_prep_body(zq_ref, zk_ref, zv_ref, cos_ref, sin_ref, gq_ref, gk_ref, q_ref, k_ref, v_ref, st_ref, *, scale):
    cosf = cos_ref[...]
    sinf = sin_ref[...]
    tl = cosf.shape[0]
    lane = lax.broadcasted_iota(jnp.int32, cosf.shape, 1)
    even = (lane % 2) == 0
    one_hot0 = jnp.where(lane == 0, 1.0, 0.0).astype(BF16)

    def prep(x, g):
        xn = _rms(x, g)
        partner = jnp.where(even, pltpu.roll(xn, LANES - 1, 1), pltpu.roll(xn, 1, 1))
        return xn * cosf + partner * sinf

    def max_norm2(xb):
        xf = xb.astype(F32)
        return jnp.max(jnp.sum(xf * xf, axis=-1, keepdims=True), axis=0, keepdims=True)

    qmax = jnp.zeros((1, 1), F32)
    for h in range(zq_ref.shape[-1] // HEAD_DIM):
        sl = slice(h * HEAD_DIM, (h + 1) * HEAD_DIM)
        qb = (prep(zq_ref[0, :, sl], gq_ref[...]) * scale).astype(BF16)
        q_ref[0, :, sl] = qb
        qmax = jnp.maximum(qmax, max_norm2(qb))
    stats = [qmax]
    for h in range(zk_ref.shape[-1] // HEAD_DIM):
        sl = slice(h * HEAD_DIM, (h + 1) * HEAD_DIM)
        kb = prep(zk_ref[0, :, sl], gk_ref[...]).astype(BF16)
        k_ref[0, :, 2 * h * HEAD_DIM:(2 * h + 1) * HEAD_DIM] = kb
        k_ref[0, :, (2 * h + 1) * HEAD_DIM:(2 * h + 2) * HEAD_DIM] = one_hot0
        v_ref[0, :, 2 * h * HEAD_DIM:(2 * h + 1) * HEAD_DIM] = zv_ref[0, :, sl].astype(BF16)
        v_ref[0, :, (2 * h + 1) * HEAD_DIM:(2 * h + 2) * HEAD_DIM] = one_hot0
        stats.append(max_norm2(kb))
    row = lax.broadcasted_iota(jnp.int32, (8, LANES), 0)
    st = jnp.zeros((8, LANES), F32)
    for r, val in enumerate(stats):
        st = jnp.where(row == r, val, st)
    st_ref[0, 0] = st


def qkv_prep(z3, cosf, sinf, gq, gk, att_w, kv_w):
    B, L, _ = z3.shape
    tl = min(PREP_TILE, L)
    off_k = att_w // kv_w
    body = functools.partial(_qkv_prep_body, scale=HEAD_DIM ** -0.5 * math.log2(math.e))
    return pl.pallas_call(
        body,
        grid=(B, L // tl),
        in_specs=[pl.BlockSpec((1, tl, att_w), lambda b, i: (b, i, 0)),
                  pl.BlockSpec((1, tl, kv_w), lambda b, i: (b, i, off_k)),
                  pl.BlockSpec((1, tl, kv_w), lambda b, i: (b, i, off_k + 1)),
                  pl.BlockSpec((tl, HEAD_DIM), lambda b, i: (i, 0)),
                  pl.BlockSpec((tl, HEAD_DIM), lambda b, i: (i, 0)),
                  pl.BlockSpec((1, HEAD_DIM), lambda b, i: (0, 0)),
                  pl.BlockSpec((1, HEAD_DIM), lambda b, i: (0, 0))],
        out_specs=[pl.BlockSpec((1, tl, att_w), lambda b, i: (b, i, 0)),
                   pl.BlockSpec((1, tl, 2 * kv_w), lambda b, i: (b, i, 0)),
                   pl.BlockSpec((1, tl, 2 * kv_w), lambda b, i: (b, i, 0)),
                   pl.BlockSpec((1, 1, 8, LANES), lambda b, i: (b, i, 0, 0))],
        out_shape=[jax.ShapeDtypeStruct((B, L, att_w), BF16),
                   jax.ShapeDtypeStruct((B, L, 2 * kv_w), BF16),
                   jax.ShapeDtypeStruct((B, L, 2 * kv_w), BF16),
                   jax.ShapeDtypeStruct((B, L // tl, 8, LANES), F32)],
        compiler_params=_params("parallel", "parallel"),
        name="qkv_prep",
    )(z3, z3, z3, cosf, sinf, gq.reshape(1, HEAD_DIM), gk.reshape(1, HEAD_DIM))


def _flash_body(fixed_ref, kmax_ref, q_ref, k_ref, v_ref, o_ref, qx_ref, acc_ref, m_ref, *, n_k, n_kv):
    b, g, kk = pl.program_id(0), pl.program_id(1), pl.program_id(3)
    nt_dims = (((1,), (1,)), ((), ()))
    tq = qx_ref.shape[1]

    @pl.when(kk == 0)
    def _():
        acc_ref[...] = jnp.zeros(acc_ref.shape, F32)
        m_ref[...] = jnp.full(m_ref.shape, MASK_VALUE, F32)
        kmax = kmax_ref[b * n_kv + g]
        lane0 = lax.broadcasted_iota(jnp.int32, (tq, HEAD_DIM), 1) == 0
        for h in range(KV_GROUP):
            q = q_ref[0, :, h * HEAD_DIM:(h + 1) * HEAD_DIM]
            qf = q.astype(F32)
            bound = jnp.sqrt(jnp.sum(qf * qf, axis=-1, keepdims=True)) * kmax
            qx_ref[h, :, :HEAD_DIM] = q
            qx_ref[h, :, HEAD_DIM:] = jnp.where(lane0, -bound, 0.0).astype(BF16)

    k = k_ref[0]
    v = v_ref[0]

    @pl.when(fixed_ref[0] == 1)
    def _():
        for h in range(KV_GROUP):
            s = lax.dot_general(qx_ref[h], k, nt_dims, preferred_element_type=F32)
            acc_ref[h] += jnp.dot(jnp.exp2(s).astype(BF16), v, preferred_element_type=F32)

    @pl.when(fixed_ref[0] == 0)
    def _():
        for h in range(KV_GROUP):
            s = lax.dot_general(qx_ref[h, :, :HEAD_DIM], k[:, :HEAD_DIM], nt_dims, preferred_element_type=F32)
            m_prev = m_ref[h]
            m_new = jnp.maximum(m_prev, jnp.max(s, axis=-1, keepdims=True))
            p = jnp.exp2(s - m_new).astype(BF16)
            acc_ref[h] = jnp.exp2(m_prev - m_new) * acc_ref[h] + jnp.dot(p, v, preferred_element_type=F32)
            m_ref[h] = m_new

    @pl.when(kk == n_k - 1)
    def _():
        for h in range(KV_GROUP):
            a = acc_ref[h]
            o_ref[0, :, h * HEAD_DIM:(h + 1) * HEAD_DIM] = (
                a[:, :HEAD_DIM] / a[:, HEAD_DIM:HEAD_DIM + 1]).astype(o_ref.dtype)


def flash_attention(q, k2, v2, stats):
    B, L, att_w = q.shape
    n_kv = k2.shape[-1] // (2 * HEAD_DIM)
    tq = min(ATT_TQ, L)
    tk = min(ATT_TK, L)
    gw = KV_GROUP * HEAD_DIM
    smax = jnp.sqrt(jnp.max(stats[:, :, :1 + n_kv, 0], axis=1))
    kmax = smax[:, 1:].reshape(B * n_kv)
    fixed = (2.0 * jnp.max(smax[:, :1] * smax[:, 1:]) < EXP2_SAFE_RANGE).astype(jnp.int32).reshape(1)
    return pl.pallas_call(
        functools.partial(_flash_body, n_k=L // tk, n_kv=n_kv),
        grid_spec=pltpu.PrefetchScalarGridSpec(
            num_scalar_prefetch=2,
            grid=(B, n_kv, L // tq, L // tk),
            in_specs=[pl.BlockSpec((1, tq, gw), lambda b, g, i, kk, *_: (b, i, g)),
                      pl.BlockSpec((1, tk, 2 * HEAD_DIM), lambda b, g, i, kk, *_: (b, kk, g)),
                      pl.BlockSpec((1, tk, 2 * HEAD_DIM), lambda b, g, i, kk, *_: (b, kk, g))],
            out_specs=pl.BlockSpec((1, tq, gw), lambda b, g, i, kk, *_: (b, i, g)),
            scratch_shapes=[pltpu.VMEM((KV_GROUP, tq, 2 * HEAD_DIM), BF16),
                            pltpu.VMEM((KV_GROUP, tq, 2 * HEAD_DIM), F32),
                            pltpu.VMEM((KV_GROUP, tq, 1), F32)]),
        out_shape=jax.ShapeDtypeStruct((B, L, att_w), BF16),
        compiler_params=_params("parallel", "parallel", "parallel", "arbitrary"),
        name="flash_attention",
    )(fixed, kmax, q, k2, v2)


def _mem_attn_body(zm_ref, kv_ref, gq_ref, gk_ref, o_ref, *, scale):
    mem_w = zm_ref.shape[-1]
    for h in range(mem_w // HEAD_DIM):
        sl = slice(h * HEAD_DIM, (h + 1) * HEAD_DIM)
        q = (_rms(zm_ref[0, :, sl], gq_ref[...]) * scale).astype(BF16)
        k = _rms(kv_ref[0, :, sl], gk_ref[...]).astype(BF16)
        v = kv_ref[0, :, mem_w + h * HEAD_DIM:mem_w + (h + 1) * HEAD_DIM].astype(BF16)
        s = lax.dot_general(q, k, (((1,), (1,)), ((), ())), preferred_element_type=F32)
        p = jnp.exp(s - jnp.max(s, axis=-1, keepdims=True))
        o = jnp.dot(p.astype(BF16), v, preferred_element_type=F32)
        o_ref[0, :, sl] = (o / jnp.sum(p, axis=-1, keepdims=True)).astype(o_ref.dtype)


def mem_attention(z3, kv, gq, gk, mem_w, off_mq):
    B, L, _ = z3.shape
    M = kv.shape[1]
    tl = min(PREP_TILE, L)
    body = functools.partial(_mem_attn_body, scale=HEAD_DIM ** -0.5)
    return pl.pallas_call(
        body,
        grid=(B, L // tl),
        in_specs=[pl.BlockSpec((1, tl, mem_w), lambda b, i: (b, i, off_mq // mem_w)),
                  pl.BlockSpec((1, M, 2 * mem_w), lambda b, i: (b, 0, 0)),
                  pl.BlockSpec((1, HEAD_DIM), lambda b, i: (0, 0)),
                  pl.BlockSpec((1, HEAD_DIM), lambda b, i: (0, 0))],
        out_specs=pl.BlockSpec((1, tl, mem_w), lambda b, i: (b, i, 0)),
        out_shape=jax.ShapeDtypeStruct((B, L, mem_w), BF16),
        compiler_params=_params("parallel", "parallel"),
        name="mem_attention",
    )(z3, kv, gq.reshape(1, HEAD_DIM), gk.reshape(1, HEAD_DIM))


def _hyena_pre_body(z_ref, prev_ref, next_ref, w_ref, b_ref, vin_ref, x0_ref, *, n_tiles):
    i = pl.program_id(1)
    x = z_ref[0]
    tl, width = x.shape
    hy_w = width // 3
    row = lax.broadcasted_iota(jnp.int32, x.shape, 0)
    prev_row = jnp.where(i > 0, prev_ref[0, 7:8, :], 0.0)
    next_row = jnp.where(i < n_tiles - 1, next_ref[0, 0:1, :], 0.0)
    x_prev = jnp.where(row == 0, prev_row, pltpu.roll(x, 1, 0))
    x_next = jnp.where(row == tl - 1, next_row, pltpu.roll(x, tl - 1, 0))
    u = w_ref[0:1, :] * x_prev + w_ref[1:2, :] * x + w_ref[2:3, :] * x_next + b_ref[...]
    vin_ref[0] = u[:, 2 * hy_w:] * u[:, hy_w:2 * hy_w]
    x0_ref[0] = u[:, :hy_w]


def hyena_pre(z3, conv_w, conv_b, hy_w, off_hy):
    B, L, _ = z3.shape
    width = 3 * hy_w
    tl = min(PREP_TILE, L)
    cb = off_hy // width
    r8 = tl // 8
    last8 = L // 8 - 1
    return pl.pallas_call(
        functools.partial(_hyena_pre_body, n_tiles=L // tl),
        grid=(B, L // tl),
        in_specs=[pl.BlockSpec((1, tl, width), lambda b, i: (b, i, cb)),
                  pl.BlockSpec((1, 8, width), lambda b, i: (b, jnp.maximum(i * r8 - 1, 0), cb)),
                  pl.BlockSpec((1, 8, width), lambda b, i: (b, jnp.minimum((i + 1) * r8, last8), cb)),
                  pl.BlockSpec((HY_SHORT, width), lambda b, i: (0, 0)),
                  pl.BlockSpec((1, width), lambda b, i: (0, 0))],
        out_specs=[pl.BlockSpec((1, tl, hy_w), lambda b, i: (b, i, 0)),
                   pl.BlockSpec((1, tl, hy_w), lambda b, i: (b, i, 0))],
        out_shape=[jax.ShapeDtypeStruct((B, L, hy_w), F32),
                   jax.ShapeDtypeStruct((B, L, hy_w), F32)],
        compiler_params=_params("parallel", "parallel"),
        name="hyena_pre",
    )(z3, z3, z3, conv_w.reshape(HY_SHORT, width), conv_b.reshape(1, width))


def _hyena_filter_body(e_ref, w1_ref, b1_ref, w2_ref, b2_ref, w3_ref, b3_ref, freq_ref, dec_ref,
                       kf_ref, ss_ref):
    hp = lax.Precision.HIGHEST
    e = e_ref[...]
    t = e[:, 0:1]
    valid = e[:, HY_EMB:HY_EMB + 1]
    freq = freq_ref[...]
    h = jnp.sin(freq * (jnp.dot(e, w1_ref[...], precision=hp, preferred_element_type=F32) + b1_ref[...]))
    h = jnp.sin(freq * (jnp.dot(h, w2_ref[...], precision=hp, preferred_element_type=F32) + b2_ref[...]))
    h = jnp.dot(h, w3_ref[...], precision=hp, preferred_element_type=F32) + b3_ref[...]
    kf = h * jnp.exp(-t * jnp.abs(dec_ref[...])) * valid
    kf_ref[...] = kf

    @pl.when(pl.program_id(0) == 0)
    def _():
        ss_ref[...] = jnp.zeros(ss_ref.shape, F32)

    ss_ref[...] += jnp.sum(kf * kf, axis=0, keepdims=True)


def hyena_filter(emb2, w1p, b1, w2, b2, w3, b3, freq, decay, hy_w):
    n_rows = emb2.shape[0]
    hidden = w2.shape[0]
    tl = min(PREP_TILE, n_rows // 2)
    nblk = n_rows // tl
    half = nblk // 2

    def dirmap(i):
        return (0, jnp.where(i >= half, 1, 0))

    const = lambda i: (0, 0)
    return pl.pallas_call(
        _hyena_filter_body,
        grid=(nblk,),
        in_specs=[pl.BlockSpec((tl, EMB_PAD), lambda i: (i, 0)),
                  pl.BlockSpec((EMB_PAD, hidden), const),
                  pl.BlockSpec((1, hidden), const),
                  pl.BlockSpec((hidden, hidden), const),
                  pl.BlockSpec((1, hidden), const),
                  pl.BlockSpec((hidden, hy_w), dirmap),
                  pl.BlockSpec((1, hy_w), dirmap),
                  pl.BlockSpec((1, hidden), const),
                  pl.BlockSpec((1, hy_w), dirmap)],
        out_specs=[pl.BlockSpec((tl, hy_w), lambda i: (i, 0)),
                   pl.BlockSpec((1, hy_w), const)],
        out_shape=[jax.ShapeDtypeStruct((n_rows, hy_w), F32),
                   jax.ShapeDtypeStruct((1, hy_w), F32)],
        compiler_params=_params("arbitrary"),
        name="hyena_filter",
    )(emb2, w1p, b1.reshape(1, hidden), w2, b2.reshape(1, hidden), w3, b3.reshape(1, -1),
      freq.reshape(1, hidden), decay.reshape(1, -1))


def _left_matmul_body(m_ref, x_ref, o_ref):
    o_ref[0] = jnp.dot(m_ref[...], x_ref[0].astype(BF16), preferred_element_type=F32)


def _left_matmul_post_body(m_ref, x_ref, vin_ref, x0_ref, bias_ref, o_ref):
    y = jnp.dot(m_ref[...], x_ref[0].astype(BF16), preferred_element_type=F32)
    vin = vin_ref[0]
    o_ref[0] = ((y + vin * bias_ref[...]) * x0_ref[0]).astype(o_ref.dtype)


def left_matmul(mat, x, post=None):
    P, K, cols = x.shape
    R = mat.shape[0]
    tc = min(DFT_COLS, cols)
    in_specs = [pl.BlockSpec((R, K), lambda p, c: (0, 0)),
                pl.BlockSpec((1, K, tc), lambda p, c: (p, 0, c))]
    args = [mat, x]
    if post is None:
        body, dtype = _left_matmul_body, F32
    else:
        vin, x0, bias_row = post
        body, dtype = _left_matmul_post_body, BF16
        in_specs += [pl.BlockSpec((1, R, tc), lambda p, c: (p, 0, c)),
                     pl.BlockSpec((1, R, tc), lambda p, c: (p, 0, c)),
                     pl.BlockSpec((1, tc), lambda p, c: (0, c))]
        args += [vin, x0, bias_row]
    return pl.pallas_call(
        body,
        grid=(P, cols // tc),
        in_specs=in_specs,
        out_specs=pl.BlockSpec((1, R, tc), lambda p, c: (p, 0, c)),
        out_shape=jax.ShapeDtypeStruct((P, R, cols), dtype),
        compiler_params=_params("parallel", "parallel"),
        name="dft_outer",
    )(*args)


def _filter_spectrum_body(g_ref, a_ref, s_ref, k_ref):
    n2 = a_ref.shape[-2]
    a = jnp.concatenate([a_ref[0, 0, 0], a_ref[0, 1, 0]], axis=0).astype(BF16)
    y = jnp.dot(g_ref[0], a, preferred_element_type=F32) * s_ref[...]
    k_ref[0, 0] = y[:n2]
    k_ref[1, 0] = y[n2:]


def filter_spectrum(gs, a5, scale_row):
    _, _, n1, n2, C = a5.shape
    return pl.pallas_call(
        _filter_spectrum_body,
        grid=(n1,),
        in_specs=[pl.BlockSpec((1, 2 * n2, 2 * n2), lambda k: (k, 0, 0)),
                  pl.BlockSpec((1, 2, 1, n2, C), lambda k: (0, 0, k, 0, 0)),
                  pl.BlockSpec((1, C), lambda k: (0, 0))],
        out_specs=pl.BlockSpec((2, 1, n2, C), lambda k: (0, k, 0, 0)),
        out_shape=jax.ShapeDtypeStruct((2, n1, n2, C), F32),
        compiler_params=_params("parallel"),
        name="filter_spectrum",
    )(gs, a5, scale_row)


def _dft_inner_body(g_ref, gi_ref, a_ref, k_ref, p_ref):
    n2 = a_ref.shape[-2]
    a = jnp.concatenate([a_ref[0, 0, 0], a_ref[0, 1, 0]], axis=0).astype(BF16)
    y = jnp.dot(g_ref[0], a, preferred_element_type=F32)
    yr, yi = y[:n2], y[n2:]
    kr, ki = k_ref[0, 0], k_ref[1, 0]
    z = jnp.concatenate([yr * kr - yi * ki, yr * ki + yi * kr], axis=0).astype(BF16)
    p = jnp.dot(gi_ref[0], z, preferred_element_type=F32)
    p_ref[0, 0, 0] = p[:n2]
    p_ref[0, 1, 0] = p[n2:]


def dft_inner(gs, gis, a5, kspec):
    P, _, n1, n2, C = a5.shape
    gspec = pl.BlockSpec((1, 2 * n2, 2 * n2), lambda k, p: (k, 0, 0))
    return pl.pallas_call(
        _dft_inner_body,
        grid=(n1, P),
        in_specs=[gspec, gspec,
                  pl.BlockSpec((1, 2, 1, n2, C), lambda k, p: (p, 0, k, 0, 0)),
                  pl.BlockSpec((2, 1, n2, C), lambda k, p: (0, k, 0, 0))],
        out_specs=pl.BlockSpec((1, 2, 1, n2, C), lambda k, p: (p, 0, k, 0, 0)),
        out_shape=jax.ShapeDtypeStruct((P, 2, n1, n2, C), F32),
        compiler_params=_params("parallel", "parallel"),
        name="dft_inner",
    )(gs, gis, a5, kspec)


def _dft_tables(n1, n2):
    n = n1 * n2
    h = n1 // 2
    a = np.arange(n1)
    ang1 = -2.0 * np.pi * ((a[:, None] * a[None, :]) % n1) / n1
    f1r, f1i = np.cos(ang1), np.sin(ang1)
    fwd_sig = np.block([[f1r[:, :h], -f1i[:, :h]], [f1i[:, :h], f1r[:, :h]]])
    fwd_flt = np.concatenate([f1r, f1i], axis=0)
    inv = np.block([[f1r[:h], f1i[:h]], [-f1i[:h], f1r[:h]]])
    b = np.arange(n2)
    ang_t = -2.0 * np.pi * ((np.arange(n1)[:, None] * b[None, :]) % n) / n
    ang_f = -2.0 * np.pi * ((b[:, None] * b[None, :]) % n2) / n2
    twr, twi = jnp.asarray(np.cos(ang_t), F32)[:, None, :], jnp.asarray(np.sin(ang_t), F32)[:, None, :]
    f2r, f2i = jnp.asarray(np.cos(ang_f), F32)[None], jnp.asarray(np.sin(ang_f), F32)[None]
    gr = twr * f2r - twi * f2i
    gi = twr * f2i + twi * f2r
    gs = jnp.concatenate([jnp.concatenate([gr, -gi], axis=2), jnp.concatenate([gi, gr], axis=2)], axis=1)
    gis = jnp.swapaxes(gs, 1, 2) * (1.0 / n)
    cvt = lambda m: jnp.asarray(m, dtype=F32).astype(BF16)
    return cvt(fwd_sig), cvt(fwd_flt), cvt(inv), gs.astype(BF16), gis.astype(BF16)


def _filter_embedding(L):
    t = jnp.linspace(0.0, 1.0, L, dtype=F32)
    w = 2.0 * math.pi * jnp.arange(L, dtype=F32) / L
    bands = jnp.linspace(1e-4, HY_BANDS - 1, HY_BANDS, dtype=F32)
    fw = w[:, None] * bands[None, :]
    emb = jnp.concatenate([t[:, None], jnp.cos(fw), -jnp.sin(fw)], axis=-1)
    valid = jnp.ones((L, 1), F32)
    emb = jnp.concatenate([emb, valid, jnp.zeros((L, EMB_PAD - HY_EMB - 1), F32)], axis=-1)
    back = jnp.concatenate([jnp.zeros((1, EMB_PAD), F32), emb[:0:-1]], axis=0)
    return jnp.concatenate([emb, back], axis=0)


def hyena_branch(z3, lp, hy_w, off_hy):
    B, L, _ = z3.shape
    n = 2 * L
    n2 = DFT_N2
    n1 = n // n2
    P = B // 2
    vin, x0 = hyena_pre(z3, lp['hy_conv_w'], lp['hy_conv_b'], hy_w, off_hy)
    fwd_sig, fwd_flt, inv, gs, gis = _dft_tables(n1, n2)

    hidden = lp['hf_w2'].shape[0]
    w1p = jnp.concatenate([lp['hf_w1'], jnp.zeros((EMB_PAD - HY_EMB, hidden), F32)], axis=0)
    kf, ss = hyena_filter(_filter_embedding(L), w1p, lp['hf_b1'], lp['hf_w2'], lp['hf_b2'], lp['hf_w3'],
                          lp['hf_b3'], lp['hf_freq'], lp['hy_decay'], hy_w)
    kscale = lax.rsqrt(ss + EPS)
    ka = left_matmul(fwd_flt, kf.reshape(1, n1, n2 * hy_w))
    kspec = filter_spectrum(gs, ka.reshape(1, 2, n1, n2, hy_w), kscale)

    a = left_matmul(fwd_sig, vin.reshape(P, n1, n2 * hy_w))
    pm = dft_inner(gs, gis, a.reshape(P, 2, n1, n2, hy_w), kspec)
    pstack = pm.reshape(P, 2 * n1, n2 * hy_w)
    bias_row = jnp.tile(lp['hy_bias'].reshape(1, hy_w), (1, n2))
    out = left_matmul(inv, pstack, post=(vin.reshape(P, n1, n2 * hy_w), x0.reshape(P, n1, n2 * hy_w), bias_row))
    return out.reshape(B, L, hy_w)


def _merge_body(a_ref, h_ref, m_ref, wa_ref, wh_ref, wm_ref, zg0_ref, zg1_ref, zg2_ref,
                bg0_ref, bg1_ref, bg2_ref, o_ref):
    def term(x_ref, w_ref, zg_ref, bg_ref):
        gate = jax.nn.sigmoid(zg_ref[...] + bg_ref[...])
        return gate * jnp.dot(x_ref[...], w_ref[...], preferred_element_type=F32)

    acc = term(a_ref, wa_ref, zg0_ref, bg0_ref)
    acc = acc + term(h_ref, wh_ref, zg1_ref, bg1_ref)
    acc = acc + term(m_ref, wm_ref, zg2_ref, bg2_ref)
    o_ref[...] = acc.astype(o_ref.dtype)


def gated_merge(a_out, h_out, m_out, wa, wh, wm, z2, b_gate, off_gate):
    T = a_out.shape[0]
    D = wa.shape[1]
    tm = min(ROW_TILE, T)
    tn = COL_TILE
    nj = D // tn
    gb = off_gate // tn

    def xspec(width):
        return pl.BlockSpec((tm, width), lambda i, j: (i, 0))

    def wspec(width):
        return pl.BlockSpec((width, tn), lambda i, j: (0, j))

    def zgspec(b):
        return pl.BlockSpec((tm, tn), lambda i, j: (i, gb + b * nj + j))

    def bgspec(b):
        return pl.BlockSpec((1, tn), lambda i, j: (0, b * nj + j))

    bg = b_gate.reshape(1, -1)
    return pl.pallas_call(
        _merge_body,
        grid=(T // tm, nj),
        in_specs=[xspec(a_out.shape[1]), xspec(h_out.shape[1]), xspec(m_out.shape[1]),
                  wspec(wa.shape[0]), wspec(wh.shape[0]), wspec(wm.shape[0]),
                  zgspec(0), zgspec(1), zgspec(2), bgspec(0), bgspec(1), bgspec(2)],
        out_specs=pl.BlockSpec((tm, tn), lambda i, j: (i, j)),
        out_shape=jax.ShapeDtypeStruct((T, D), BF16),
        compiler_params=_params("parallel", "parallel"),
        name="gated_merge",
    )(a_out, h_out, m_out, wa, wh, wm, z2, z2, z2, bg, bg, bg)


def _slab_store(ref, x):
    rows, width = x.shape
    n = width // LANES
    for s in range(n):
        ref[pl.ds(s, rows, stride=n), :] = x[:, s * LANES:(s + 1) * LANES]


def _slab_load(ref, n):
    rows = ref.shape[0] // n
    return jnp.concatenate([ref[pl.ds(s, rows, stride=n), :] for s in range(n)], axis=-1)


ROUTE_E, ROUTE_W, ROUTE_RANK = 0, 2, 4


def _route_tile(lg, carry):
    tm = lg.shape[0]
    lanef = lax.broadcasted_iota(jnp.int32, lg.shape, 1).astype(F32)
    row_max = lambda x: jnp.max(x, axis=-1, keepdims=True)
    first_at = lambda x, v: jnp.min(jnp.where(x == v, lanef, float(LANES)), axis=-1, keepdims=True)

    gl = jnp.where(lanef < N_GROUPS, lg, MASK_VALUE)
    gmax = row_max(gl)
    g_p = 1.0 / jnp.sum(jnp.where(lanef < N_GROUPS, jnp.exp(gl - gmax), 0.0), axis=-1, keepdims=True)
    lo = N_GROUPS + first_at(gl, gmax) * EXPERTS_PER_GROUP
    el = jnp.where((lanef >= lo) & (lanef < lo + EXPERTS_PER_GROUP), lg, MASK_VALUE)
    v1 = row_max(el)
    i1 = first_at(el, v1)
    el2 = jnp.where(lanef == i1, MASK_VALUE, el)
    v2 = row_max(el2)
    i2 = first_at(el2, v2)
    t = jnp.exp(v2 - v1)
    w1 = g_p / (1.0 + t)
    w2 = w1 * t
    e1, e2 = i1 - N_GROUPS, i2 - N_GROUPS

    oh1 = jnp.where(lanef == e1, 1.0, 0.0)
    oh2 = jnp.where(lanef == e2, 1.0, 0.0)
    oh = oh1 + oh2
    r = lax.broadcasted_iota(jnp.int32, (tm, tm), 0)
    c = lax.broadcasted_iota(jnp.int32, (tm, tm), 1)
    earlier = jnp.where(c < r, 1.0, 0.0).astype(BF16)
    before = jnp.dot(earlier, oh.astype(BF16), preferred_element_type=F32) + carry
    rank1 = jnp.sum(oh1 * before, axis=-1, keepdims=True)
    rank2 = jnp.sum(oh2 * before, axis=-1, keepdims=True)

    rec = jnp.zeros(lg.shape, F32)
    for lane_id, val in ((ROUTE_E, e1), (ROUTE_E + 1, e2), (ROUTE_W, w1), (ROUTE_W + 1, w2),
                         (ROUTE_RANK, rank1), (ROUTE_RANK + 1, rank2)):
        rec = jnp.where(lanef == lane_id, val, rec)
    return rec, carry + jnp.sum(oh, axis=0, keepdims=True)


def _out_router_body(mg_ref, x_ref, wo_ref, g2_ref, whi_ref, wlo_ref, br_ref, x1_ref, h2_ref, rec_ref, cnt_ref):
    @pl.when(pl.program_id(0) == 0)
    def _():
        cnt_ref[...] = jnp.zeros(cnt_ref.shape, F32)

    x1 = x_ref[...] + jnp.dot(mg_ref[...], wo_ref[...], preferred_element_type=F32)
    x1_ref[...] = x1
    h = _rms(x1, g2_ref[...])
    h_hi = h.astype(BF16)
    _slab_store(h2_ref, h_hi.astype(F32))
    h_lo = (h - h_hi.astype(F32)).astype(BF16)
    lg = jnp.dot(h_hi, whi_ref[...], preferred_element_type=F32)
    lg = lg + jnp.dot(h_lo, whi_ref[...], preferred_element_type=F32)
    lg = lg + jnp.dot(h_hi, wlo_ref[...], preferred_element_type=F32)
    rec, cnt = _route_tile(lg + br_ref[...], cnt_ref[...])
    rec_ref[...] = rec
    cnt_ref[...] = cnt


def out_router(merged, x2, w_out, g2, wr_hi, wr_lo, br):
    T, D = x2.shape
    tm = min(PREP_TILE, T)
    const = lambda i: (0, 0)
    row = lambda i: (i, 0)
    return pl.pallas_call(
        _out_router_body,
        grid=(T // tm,),
        in_specs=[pl.BlockSpec((tm, D), row), pl.BlockSpec((tm, D), row),
                  pl.BlockSpec((D, D), const), pl.BlockSpec((1, D), const),
                  pl.BlockSpec((D, ROUTER_PAD), const), pl.BlockSpec((D, ROUTER_PAD), const),
                  pl.BlockSpec((1, ROUTER_PAD), const)],
        out_specs=[pl.BlockSpec((tm, D), row), pl.BlockSpec((tm * (D // LANES), LANES), row),
                   pl.BlockSpec((tm, ROUTER_PAD), row), pl.BlockSpec((1, ROUTER_PAD), const)],
        out_shape=[jax.ShapeDtypeStruct((T, D), F32), jax.ShapeDtypeStruct((T * (D // LANES), LANES), F32),
                   jax.ShapeDtypeStruct((T, ROUTER_PAD), F32), jax.ShapeDtypeStruct((1, ROUTER_PAD), F32)],
        compiler_params=_params("arbitrary"),
        name="out_router",
    )(merged, x2, w_out, g2.reshape(1, D), wr_hi, wr_lo, br)


def _row_copy(src_ref, src_row, dst_ref, dst_row, n, sem):
    src = src_ref.at[pl.ds(pl.multiple_of(src_row * n, n), n), :]
    dst = dst_ref.at[pl.ds(pl.multiple_of(dst_row * n, n), n), :]
    return pltpu.make_async_copy(src, dst, sem)


def _for_rows(rows, fn):
    def group(gi, carry):
        for u in range(DMA_UNROLL):
            fn(gi * DMA_UNROLL + u)
        return carry

    lax.fori_loop(0, rows // DMA_UNROLL, group, 0)


def _dispatch_body(dest_ref, h_ref, xs_in_ref, xs_ref, sem, *, n):
    del xs_in_ref
    rows = h_ref.shape[0] // n
    base = pl.program_id(0) * rows

    def copies(r):
        a = TOP_K * (base + r)
        return [_row_copy(h_ref, r, xs_ref, dest_ref[a + j], n, sem) for j in range(TOP_K)]

    def issue(r):
        for c in copies(r):
            c.start()

    def drain(r):
        for c in copies(r):
            c.wait()

    _for_rows(rows, issue)
    _for_rows(rows, drain)


def dispatch_rows(dest_flat, h_slab, n, P):
    T = h_slab.shape[0] // n
    rows = min(GATHER_ROWS, T)
    xs0 = jnp.zeros((P * n, LANES), h_slab.dtype)
    return pl.pallas_call(
        functools.partial(_dispatch_body, n=n),
        grid_spec=pltpu.PrefetchScalarGridSpec(
            num_scalar_prefetch=1,
            grid=(T // rows,),
            in_specs=[pl.BlockSpec((rows * n, LANES), lambda i, dest: (i, 0)),
                      pl.BlockSpec(memory_space=pl.ANY)],
            out_specs=pl.BlockSpec(memory_space=pl.ANY),
            scratch_shapes=[pltpu.SemaphoreType.DMA(())]),
        out_shape=jax.ShapeDtypeStruct((P * n, LANES), h_slab.dtype),
        input_output_aliases={2: 0},
        compiler_params=_params("arbitrary"),
        name="dispatch_rows",
    )(dest_flat, h_slab, xs0)


def _moe_body(blk_e_ref, n_used_ref, x_ref, wg_ref, wu_ref, wd_ref, o_ref):
    del blk_e_ref
    i = pl.program_id(0)
    n = wg_ref.shape[1] // LANES

    @pl.when(i < n_used_ref[0])
    def _():
        x = _slab_load(x_ref, n).astype(BF16)
        g = jnp.dot(x, wg_ref[0], preferred_element_type=F32)
        u = jnp.dot(x, wu_ref[0], preferred_element_type=F32)
        a = (g * jax.nn.sigmoid(g) * u).astype(BF16)
        _slab_store(o_ref, jnp.dot(a, wd_ref[0], preferred_element_type=F32))

    @pl.when(i >= n_used_ref[0])
    def _():
        o_ref[...] = jnp.zeros(o_ref.shape, o_ref.dtype)


def moe_experts(blk_e, n_used, xs_slab, wg, wu, wd):
    _, D, De = wg.shape
    n = D // LANES
    P = xs_slab.shape[0] // n
    tm = MOE_TM
    return pl.pallas_call(
        _moe_body,
        grid_spec=pltpu.PrefetchScalarGridSpec(
            num_scalar_prefetch=2,
            grid=(P // tm,),
            in_specs=[pl.BlockSpec((tm * n, LANES), lambda i, be, nu: (i, 0)),
                      pl.BlockSpec((1, D, De), lambda i, be, nu: (be[i], 0, 0)),
                      pl.BlockSpec((1, D, De), lambda i, be, nu: (be[i], 0, 0)),
                      pl.BlockSpec((1, De, D), lambda i, be, nu: (be[i], 0, 0))],
            out_specs=pl.BlockSpec((tm * n, LANES), lambda i, be, nu: (i, 0))),
        out_shape=jax.ShapeDtypeStruct((P * n, LANES), F32),
        compiler_params=_params("arbitrary"),
        name="moe_experts",
    )(blk_e, n_used, xs_slab, wg, wu, wd)


def _combine_body(dest_ref, x1_ref, rec_ref, yb_ref, o_ref, *scratch):
    bufs, sem = scratch[:TOP_K], scratch[TOP_K]
    tg, D = x1_ref.shape
    n = D // LANES
    base = pl.program_id(0) * tg

    def copies(r):
        a = TOP_K * (base + r)
        return [_row_copy(yb_ref, dest_ref[a + j], bufs[j], r, n, sem) for j in range(TOP_K)]

    def issue(r):
        for c in copies(r):
            c.start()

    def drain(r):
        for c in copies(r):
            c.wait()

    _for_rows(tg, issue)
    _for_rows(tg, drain)
    w = [jnp.broadcast_to(rec_ref[:, ROUTE_W + j:ROUTE_W + j + 1], (tg, LANES)) for j in range(TOP_K)]
    for s in range(n):
        cols = slice(s * LANES, (s + 1) * LANES)
        y = x1_ref[:, cols]
        for j in range(TOP_K):
            y = y + w[j] * bufs[j][pl.ds(s, tg, stride=n), :]
        o_ref[:, cols] = y


def moe_combine(dest_flat, x1, rec, yb_slab):
    T, D = x1.shape
    tg = min(GATHER_ROWS, T)
    n = D // LANES
    return pl.pallas_call(
        _combine_body,
        grid_spec=pltpu.PrefetchScalarGridSpec(
            num_scalar_prefetch=1,
            grid=(T // tg,),
            in_specs=[pl.BlockSpec((tg, D), lambda i, dest: (i, 0)),
                      pl.BlockSpec((tg, ROUTER_PAD), lambda i, dest: (i, 0)),
                      pl.BlockSpec(memory_space=pl.ANY)],
            out_specs=pl.BlockSpec((tg, D), lambda i, dest: (i, 0)),
            scratch_shapes=[pltpu.VMEM((tg * n, LANES), F32) for _ in range(TOP_K)]
            + [pltpu.SemaphoreType.DMA(())]),
        out_shape=jax.ShapeDtypeStruct((T, D), F32),
        compiler_params=_params("arbitrary"),
        name="moe_combine",
    )(dest_flat, x1, rec, yb_slab)


def _block_layout(rec, counts_row):
    T = rec.shape[0]
    A = T * TOP_K
    P = (A + N_EXPERTS * (MOE_TM - 1) + MOE_TM - 1) // MOE_TM * MOE_TM
    n_blk = P // MOE_TM
    counts = counts_row[0, :N_EXPERTS].astype(jnp.int32)
    padded = (counts + MOE_TM - 1) // MOE_TM * MOE_TM
    pends = jnp.cumsum(padded)
    pstarts = pends - padded
    e = rec[:, ROUTE_E:ROUTE_E + TOP_K].astype(jnp.int32)
    rank = rec[:, ROUTE_RANK:ROUTE_RANK + TOP_K].astype(jnp.int32)
    one_hot = e[:, :, None] == jnp.arange(N_EXPERTS, dtype=jnp.int32)
    dest = jnp.sum(jnp.where(one_hot, pstarts, 0), axis=-1) + rank
    blk_start = jnp.arange(n_blk, dtype=jnp.int32) * MOE_TM
    blk_e = jnp.minimum(jnp.sum(blk_start[:, None] >= pends[None, :], axis=-1), N_EXPERTS - 1).astype(jnp.int32)
    n_used = (pends[-1:] // MOE_TM).astype(jnp.int32)
    return dest.reshape(A), blk_e, n_used, P


def _rope_tables(L):
    rows = L // GRID_W
    pairs = HEAD_DIM // 4
    row = jnp.repeat(jnp.arange(rows, dtype=F32), GRID_W)
    col = jnp.tile(jnp.arange(GRID_W, dtype=F32), rows)
    inv = jnp.power(ROPE_THETA, -jnp.arange(pairs, dtype=F32) / pairs)
    ang = jnp.concatenate([row[:, None] * inv, col[:, None] * inv], axis=-1)
    cos, sin = jnp.cos(ang), jnp.sin(ang)
    cosf = jnp.repeat(cos, 2, axis=-1)
    sinf = jnp.stack([-sin, sin], axis=-1).reshape(L, HEAD_DIM)
    return cosf, sinf


def _trunk(x, mem, lp, wb):
    B, L, D = x.shape
    T = B * L
    att_w = lp['w_br_attn'].shape[0]
    hy_w = lp['w_br_hyena'].shape[0]
    mem_w = lp['w_br_mem'].shape[0]
    kv_w = (lp['w_in'].shape[1] - att_w - 3 * hy_w - mem_w - 3 * D) // 2
    off_hy = att_w + 2 * kv_w
    off_mq = off_hy + 3 * hy_w
    off_gate = off_mq + mem_w
    x2 = x.reshape(T, D)

    z2 = normed_matmul(x2, lp['norm1_g'], wb['w_in'])
    z3 = z2.reshape(B, L, -1)
    cosf, sinf = _rope_tables(L)
    q, k2, v2, stats = qkv_prep(z3, cosf, sinf, lp['q_norm_g'], lp['k_norm_g'], att_w, kv_w)
    a_out = flash_attention(q, k2, v2, stats)
    h_out = hyena_branch(z3, lp, hy_w, off_hy)
    M = mem.shape[1]
    kv = normed_matmul(mem.reshape(B * M, D), lp['mem_norm_g'], wb['w_mem_kv']).reshape(B, M, -1)
    m_out = mem_attention(z3, kv, lp['mq_norm_g'], lp['mk_norm_g'], mem_w, off_mq)
    merged = gated_merge(a_out.reshape(T, att_w), h_out.reshape(T, hy_w), m_out.reshape(T, mem_w),
                         wb['w_br_attn'], wb['w_br_hyena'], wb['w_br_mem'], z2, lp['b_gate'], off_gate)
    x1, h2, rec, counts = out_router(merged, x2, wb['w_out'], lp['norm2_g'], wb['wr_hi'], wb['wr_lo'], wb['br'])

    dest_flat, blk_e, n_used, P = _block_layout(rec, counts)
    xs = dispatch_rows(dest_flat, h2, D // LANES, P)
    yb = moe_experts(blk_e, n_used, xs, wb['w_gate_e'], wb['w_up_e'], wb['w_down_e'])
    y = moe_combine(dest_flat, x1, rec, yb)
    return y.reshape(B, L, D)


def kernel(x_prompt, x_sample, mem_prompt, mem_sample, norm1_g, w_in, b_gate, q_norm_g, k_norm_g, hy_conv_w, hy_conv_b, hf_w1, hf_b1, hf_w2, hf_b2, hf_w3, hf_b3, hf_freq, hy_decay, hy_bias, mem_norm_g, w_mem_kv, mq_norm_g, mk_norm_g, w_br_attn, w_br_hyena, w_br_mem, w_out, norm2_g, w_router_group, b_router_group, w_router_expert, b_router_expert, w_gate_e, w_up_e, w_down_e):
    params = dict(norm1_g=norm1_g, w_in=w_in, b_gate=b_gate, q_norm_g=q_norm_g, k_norm_g=k_norm_g,
                  hy_conv_w=hy_conv_w, hy_conv_b=hy_conv_b, hf_w1=hf_w1, hf_b1=hf_b1, hf_w2=hf_w2,
                  hf_b2=hf_b2, hf_w3=hf_w3, hf_b3=hf_b3, hf_freq=hf_freq, hy_decay=hy_decay,
                  hy_bias=hy_bias, mem_norm_g=mem_norm_g, w_mem_kv=w_mem_kv, mq_norm_g=mq_norm_g,
                  mk_norm_g=mk_norm_g, w_br_attn=w_br_attn, w_br_hyena=w_br_hyena, w_br_mem=w_br_mem,
                  w_out=w_out, norm2_g=norm2_g, w_router_group=w_router_group,
                  b_router_group=b_router_group, w_router_expert=w_router_expert,
                  b_router_expert=b_router_expert, w_gate_e=w_gate_e, w_up_e=w_up_e, w_down_e=w_down_e)
    depth = w_in.shape[0]
    xp, xs = x_prompt, x_sample
    for d in range(depth):
        lp = {name: arr[d] for name, arr in params.items()}
        D = lp['w_in'].shape[0]
        wb = {name: lp[name].astype(BF16) for name in
              ('w_in', 'w_mem_kv', 'w_br_attn', 'w_br_hyena', 'w_br_mem', 'w_out',
               'w_gate_e', 'w_up_e', 'w_down_e')}
        wr = jnp.concatenate([lp['w_router_group'], lp['w_router_expert'],
                              jnp.zeros((D, ROUTER_PAD - N_GROUPS - N_EXPERTS), F32)], axis=1)
        wb['wr_hi'] = wr.astype(BF16)
        wb['wr_lo'] = (wr - wb['wr_hi'].astype(F32)).astype(BF16)
        wb['br'] = jnp.concatenate([lp['b_router_group'], lp['b_router_expert'],
                                    jnp.zeros((ROUTER_PAD - N_GROUPS - N_EXPERTS,), F32)]).reshape(1, ROUTER_PAD)
        xp = _trunk(xp, mem_prompt, lp, wb)
        xs = _trunk(xs, mem_sample, lp, wb)
    return (xp, xs)
```

```python
import functools
import math

import jax
import jax.numpy as jnp
import numpy as np
from jax import lax
from jax.experimental import pallas as pl
from jax.experimental.pallas import tpu as pltpu

F32 = jnp.float32
BF16 = jnp.bfloat16

HEAD_DIM = 128
KV_GROUP = 4
GRID_W = 64
ROPE_THETA = 10000.0
HY_BANDS = 8
HY_EMB = 1 + 2 * HY_BANDS
HY_SHORT = 3
MEM_HEADS = 4
N_GROUPS = 4
EXPERTS_PER_GROUP = 8
N_EXPERTS = N_GROUPS * EXPERTS_PER_GROUP
TOP_K = 2
EPS = 1e-6
MASK_VALUE = -1e30
EXP2_SAFE_RANGE = 120.0
LANES = 128
MXU_DIM = 256
VMEM_LIMIT_BYTES = 52 * 1024 * 1024

ROW_TILE = 1024
COL_TILE = 512
PREP_TILE = 512
ATT_TQ = 512
ATT_TK = 2048
DFT_N2 = 256
DFT_COLS = 2048
MOE_TM = 256
GATHER_ROWS = 512
DMA_UNROLL = 8
EMB_PAD = 32
ROUTER_PAD = 128


def _params(*sem):
    return pltpu.CompilerParams(dimension_semantics=sem, vmem_limit_bytes=VMEM_LIMIT_BYTES)


def _rms(x, g):
    ms = jnp.mean(x * x, axis=-1, keepdims=True)
    return x * lax.rsqrt(ms + EPS) * g


def _normed_matmul_body(x_ref, g_ref, w_ref, o_ref, xn_ref):
    @pl.when(pl.program_id(1) == 0)
    def _():
        xn_ref[...] = _rms(x_ref[...], g_ref[...]).astype(BF16)

    o_ref[...] = jnp.dot(xn_ref[...], w_ref[...], preferred_element_type=F32).astype(o_ref.dtype)


def normed_matmul(x, g, w, out_dtype):
    T, D = x.shape
    N = w.shape[1]
    tm = min(ROW_TILE, T)
    tn = COL_TILE
    return pl.pallas_call(
        _normed_matmul_body,
        grid=(T // tm, N // tn),
        in_specs=[pl.BlockSpec((tm, D), lambda i, j: (i, 0)),
                  pl.BlockSpec((1, D), lambda i, j: (0, 0)),
                  pl.BlockSpec((D, tn), lambda i, j: (0, j))],
        out_specs=pl.BlockSpec((tm, tn), lambda i, j: (i, j)),
        out_shape=jax.ShapeDtypeStruct((T, N), out_dtype),
        scratch_shapes=[pltpu.VMEM((tm, D), BF16)],
        compiler_params=_params("parallel", "arbitrary"),
        name="normed_matmul",
    )(x, g.reshape(1, D), w)


def _qkv_prep_body(zq_ref, zk_ref, zv_ref, cos_ref, sin_ref, gq_ref, gk_ref, q_ref, k_ref, v_ref, st_ref, *, scale):
    cosf = cos_ref[...]
    sinf = sin_ref[...]
    tl = cosf.shape[0]
    lane = lax.broadcasted_iota(jnp.int32, cosf.shape, 1)
    even = (lane % 2) == 0
    one_hot0 = jnp.where(lane == 0, 1.0, 0.0).astype(BF16)

    def prep(x, g):
        xn = _rms(x, g)
        partner = jnp.where(even, pltpu.roll(xn, LANES - 1, 1), pltpu.roll(xn, 1, 1))
        return xn * cosf + partner * sinf

    def max_norm2(xb):
        xf = xb.astype(F32)
        return jnp.max(jnp.sum(xf * xf, axis=-1, keepdims=True), axis=0, keepdims=True)

    qmax = jnp.zeros((1, 1), F32)
    for h in range(zq_ref.shape[-1] // HEAD_DIM):
        sl = slice(h * HEAD_DIM, (h + 1) * HEAD_DIM)
        qb = (prep(zq_ref[0, :, sl].astype(F32), gq_ref[...]) * scale).astype(BF16)
        q_ref[0, :, sl] = qb
        qmax = jnp.maximum(qmax, max_norm2(qb))
    stats = [qmax]
    for h in range(zk_ref.shape[-1] // HEAD_DIM):
        sl = slice(h * HEAD_DIM, (h + 1) * HEAD_DIM)
        kb = prep(zk_ref[0, :, sl].astype(F32), gk_ref[...]).astype(BF16)
        k_ref[0, :, 2 * h * HEAD_DIM:(2 * h + 1) * HEAD_DIM] = kb
        k_ref[0, :, (2 * h + 1) * HEAD_DIM:(2 * h + 2) * HEAD_DIM] = one_hot0
        v_ref[0, :, 2 * h * HEAD_DIM:(2 * h + 1) * HEAD_DIM] = zv_ref[0, :, sl].astype(BF16)
        v_ref[0, :, (2 * h + 1) * HEAD_DIM:(2 * h + 2) * HEAD_DIM] = one_hot0
        stats.append(max_norm2(kb))
    row = lax.broadcasted_iota(jnp.int32, (8, LANES), 0)
    st = jnp.zeros((8, LANES), F32)
    for r, val in enumerate(stats):
        st = jnp.where(row == r, val, st)
    st_ref[0, 0] = st


def qkv_prep(z3, cosf, sinf, gq, gk, att_w, kv_w):
    B, L, _ = z3.shape
    tl = min(PREP_TILE, L)
    off_k = att_w // kv_w
    body = functools.partial(_qkv_prep_body, scale=HEAD_DIM ** -0.5 * math.log2(math.e))
    return pl.pallas_call(
        body,
        grid=(B, L // tl),
        in_specs=[pl.BlockSpec((1, tl, att_w), lambda b, i: (b, i, 0)),
                  pl.BlockSpec((1, tl, kv_w), lambda b, i: (b, i, off_k)),
                  pl.BlockSpec((1, tl, kv_w), lambda b, i: (b, i, off_k + 1)),
                  pl.BlockSpec((tl, HEAD_DIM), lambda b, i: (i, 0)),
                  pl.BlockSpec((tl, HEAD_DIM), lambda b, i: (i, 0)),
                  pl.BlockSpec((1, HEAD_DIM), lambda b, i: (0, 0)),
                  pl.BlockSpec((1, HEAD_DIM), lambda b, i: (0, 0))],
        out_specs=[pl.BlockSpec((1, tl, att_w), lambda b, i: (b, i, 0)),
                   pl.BlockSpec((1, tl, 2 * kv_w), lambda b, i: (b, i, 0)),
                   pl.BlockSpec((1, tl, 2 * kv_w), lambda b, i: (b, i, 0)),
                   pl.BlockSpec((1, 1, 8, LANES), lambda b, i: (b, i, 0, 0))],
        out_shape=[jax.ShapeDtypeStruct((B, L, att_w), BF16),
                   jax.ShapeDtypeStruct((B, L, 2 * kv_w), BF16),
                   jax.ShapeDtypeStruct((B, L, 2 * kv_w), BF16),
                   jax.ShapeDtypeStruct((B, L // tl, 8, LANES), F32)],
        compiler_params=_params("parallel", "parallel"),
        name="qkv_prep",
    )(z3, z3, z3, cosf, sinf, gq.reshape(1, HEAD_DIM), gk.reshape(1, HEAD_DIM))


def _flash_body(fixed_ref, kmax_ref, q_ref, k_ref, v_ref, o_ref, qx_ref, acc_ref, m_ref, *, n_k, n_kv):
    b, g, kk = pl.program_id(0), pl.program_id(1), pl.program_id(3)
    nt_dims = (((1,), (1,)), ((), ()))
    tq = qx_ref.shape[1]

    @pl.when(kk == 0)
    def _():
        acc_ref[...] = jnp.zeros(acc_ref.shape, F32)
        m_ref[...] = jnp.full(m_ref.shape, MASK_VALUE, F32)
        kmax = kmax_ref[b * n_kv + g]
        lane0 = lax.broadcasted_iota(jnp.int32, (tq, HEAD_DIM), 1) == 0
        for h in range(KV_GROUP):
            q = q_ref[0, :, h * HEAD_DIM:(h + 1) * HEAD_DIM]
            qf = q.astype(F32)
            bound = jnp.sqrt(jnp.sum(qf * qf, axis=-1, keepdims=True)) * kmax
            qx_ref[h, :, :HEAD_DIM] = q
            qx_ref[h, :, HEAD_DIM:] = jnp.where(lane0, -bound, 0.0).astype(BF16)

    k = k_ref[0]
    v = v_ref[0]

    @pl.when(fixed_ref[0] == 1)
    def _():
        for h in range(KV_GROUP):
            s = lax.dot_general(qx_ref[h], k, nt_dims, preferred_element_type=F32)
            acc_ref[h] += jnp.dot(jnp.exp2(s).astype(BF16), v, preferred_element_type=F32)

    @pl.when(fixed_ref[0] == 0)
    def _():
        for h in range(KV_GROUP):
            s = lax.dot_general(qx_ref[h, :, :HEAD_DIM], k[:, :HEAD_DIM], nt_dims, preferred_element_type=F32)
            m_prev = m_ref[h]
            m_new = jnp.maximum(m_prev, jnp.max(s, axis=-1, keepdims=True))
            p = jnp.exp2(s - m_new).astype(BF16)
            acc_ref[h] = jnp.exp2(m_prev - m_new) * acc_ref[h] + jnp.dot(p, v, preferred_element_type=F32)
            m_ref[h] = m_new

    @pl.when(kk == n_k - 1)
    def _():
        for h in range(KV_GROUP):
            a = acc_ref[h]
            o_ref[0, :, h * HEAD_DIM:(h + 1) * HEAD_DIM] = (
                a[:, :HEAD_DIM] / a[:, HEAD_DIM:HEAD_DIM + 1]).astype(o_ref.dtype)


def flash_attention(q, k2, v2, stats):
    B, L, att_w = q.shape
    n_kv = k2.shape[-1] // (2 * HEAD_DIM)
    tq = min(ATT_TQ, L)
    tk = min(ATT_TK, L)
    gw = KV_GROUP * HEAD_DIM
    smax = jnp.sqrt(jnp.max(stats[:, :, :1 + n_kv, 0], axis=1))
    kmax = smax[:, 1:].reshape(B * n_kv)
    fixed = (2.0 * jnp.max(smax[:, :1] * smax[:, 1:]) < EXP2_SAFE_RANGE).astype(jnp.int32).reshape(1)
    return pl.pallas_call(
        functools.partial(_flash_body, n_k=L // tk, n_kv=n_kv),
        grid_spec=pltpu.PrefetchScalarGridSpec(
            num_scalar_prefetch=2,
            grid=(B, n_kv, L // tq, L // tk),
            in_specs=[pl.BlockSpec((1, tq, gw), lambda b, g, i, kk, *_: (b, i, g)),
                      pl.BlockSpec((1, tk, 2 * HEAD_DIM), lambda b, g, i, kk, *_: (b, kk, g)),
                      pl.BlockSpec((1, tk, 2 * HEAD_DIM), lambda b, g, i, kk, *_: (b, kk, g))],
            out_specs=pl.BlockSpec((1, tq, gw), lambda b, g, i, kk, *_: (b, i, g)),
            scratch_shapes=[pltpu.VMEM((KV_GROUP, tq, 2 * HEAD_DIM), BF16),
                            pltpu.VMEM((KV_GROUP, tq, 2 * HEAD_DIM), F32),
                            pltpu.VMEM((KV_GROUP, tq, 1), F32)]),
        out_shape=jax.ShapeDtypeStruct((B, L, att_w), BF16),
        compiler_params=_params("parallel", "parallel", "parallel", "arbitrary"),
        name="flash_attention",
    )(fixed, kmax, q, k2, v2)


def _mem_attn_body(zm_ref, kv_ref, gq_ref, gk_ref, o_ref, *, scale):
    mem_w = zm_ref.shape[-1]
    for h in range(mem_w // HEAD_DIM):
        sl = slice(h * HEAD_DIM, (h + 1) * HEAD_DIM)
        q = (_rms(zm_ref[0, :, sl].astype(F32), gq_ref[...]) * scale).astype(BF16)
        k = _rms(kv_ref[0, :, sl], gk_ref[...]).astype(BF16)
        v = kv_ref[0, :, mem_w + h * HEAD_DIM:mem_w + (h + 1) * HEAD_DIM].astype(BF16)
        s = lax.dot_general(q, k, (((1,), (1,)), ((), ())), preferred_element_type=F32)
        p = jnp.exp(s - jnp.max(s, axis=-1, keepdims=True))
        o = jnp.dot(p.astype(BF16), v, preferred_element_type=F32)
        o_ref[0, :, sl] = (o / jnp.sum(p, axis=-1, keepdims=True)).astype(o_ref.dtype)


def mem_attention(z3, kv, gq, gk, mem_w, off_mq):
    B, L, _ = z3.shape
    M = kv.shape[1]
    tl = min(PREP_TILE, L)
    body = functools.partial(_mem_attn_body, scale=HEAD_DIM ** -0.5)
    return pl.pallas_call(
        body,
        grid=(B, L // tl),
        in_specs=[pl.BlockSpec((1, tl, mem_w), lambda b, i: (b, i, off_mq // mem_w)),
                  pl.BlockSpec((1, M, 2 * mem_w), lambda b, i: (b, 0, 0)),
                  pl.BlockSpec((1, HEAD_DIM), lambda b, i: (0, 0)),
                  pl.BlockSpec((1, HEAD_DIM), lambda b, i: (0, 0))],
        out_specs=pl.BlockSpec((1, tl, mem_w), lambda b, i: (b, i, 0)),
        out_shape=jax.ShapeDtypeStruct((B, L, mem_w), BF16),
        compiler_params=_params("parallel", "parallel"),
        name="mem_attention",
    )(z3, kv, gq.reshape(1, HEAD_DIM), gk.reshape(1, HEAD_DIM))


def _hyena_pre_body(z_ref, prev_ref, next_ref, w_ref, b_ref, vin_ref, x0_ref, *, n_tiles):
    i = pl.program_id(1)
    x = z_ref[0].astype(F32)
    tl, width = x.shape
    hy_w = width // 3
    row = lax.broadcasted_iota(jnp.int32, x.shape, 0)
    halo = prev_ref.shape[1]
    prev_row = jnp.where(i > 0, prev_ref[0, halo - 1:halo, :].astype(F32), 0.0)
    next_row = jnp.where(i < n_tiles - 1, next_ref[0, 0:1, :].astype(F32), 0.0)
    x_prev = jnp.where(row == 0, prev_row, pltpu.roll(x, 1, 0))
    x_next = jnp.where(row == tl - 1, next_row, pltpu.roll(x, tl - 1, 0))
    u = w_ref[0:1, :] * x_prev + w_ref[1:2, :] * x + w_ref[2:3, :] * x_next + b_ref[...]
    vin_ref[0] = u[:, 2 * hy_w:] * u[:, hy_w:2 * hy_w]
    x0_ref[0] = u[:, :hy_w]


def hyena_pre(z3, conv_w, conv_b, hy_w, off_hy):
    B, L, _ = z3.shape
    width = 3 * hy_w
    tl = min(PREP_TILE, L)
    cb = off_hy // width
    halo = 16
    r8 = tl // halo
    last8 = L // halo - 1
    return pl.pallas_call(
        functools.partial(_hyena_pre_body, n_tiles=L // tl),
        grid=(B, L // tl),
        in_specs=[pl.BlockSpec((1, tl, width), lambda b, i: (b, i, cb)),
                  pl.BlockSpec((1, halo, width), lambda b, i: (b, jnp.maximum(i * r8 - 1, 0), cb)),
                  pl.BlockSpec((1, halo, width), lambda b, i: (b, jnp.minimum((i + 1) * r8, last8), cb)),
                  pl.BlockSpec((HY_SHORT, width), lambda b, i: (0, 0)),
                  pl.BlockSpec((1, width), lambda b, i: (0, 0))],
        out_specs=[pl.BlockSpec((1, tl, hy_w), lambda b, i: (b, i, 0)),
                   pl.BlockSpec((1, tl, hy_w), lambda b, i: (b, i, 0))],
        out_shape=[jax.ShapeDtypeStruct((B, L, hy_w), F32),
                   jax.ShapeDtypeStruct((B, L, hy_w), F32)],
        compiler_params=_params("parallel", "parallel"),
        name="hyena_pre",
    )(z3, z3, z3, conv_w.reshape(HY_SHORT, width), conv_b.reshape(1, width))


def _hyena_filter_body(e_ref, w1_ref, b1_ref, w2_ref, b2_ref, w3_ref, b3_ref, freq_ref, dec_ref,
                       kf_ref, ss_ref):
    hp = lax.Precision.HIGHEST
    e = e_ref[...]
    t = e[:, 0:1]
    valid = e[:, HY_EMB:HY_EMB + 1]
    freq = freq_ref[...]
    h = jnp.sin(freq * (jnp.dot(e, w1_ref[...], precision=hp, preferred_element_type=F32) + b1_ref[...]))
    h = jnp.sin(freq * (jnp.dot(h, w2_ref[...], precision=hp, preferred_element_type=F32) + b2_ref[...]))
    h = jnp.dot(h, w3_ref[...], precision=hp, preferred_element_type=F32) + b3_ref[...]
    kf = h * jnp.exp(-t * jnp.abs(dec_ref[...])) * valid
    kf_ref[...] = kf

    @pl.when(pl.program_id(0) == 0)
    def _():
        ss_ref[...] = jnp.zeros(ss_ref.shape, F32)

    ss_ref[...] += jnp.sum(kf * kf, axis=0, keepdims=True)


def hyena_filter(emb2, w1p, b1, w2, b2, w3, b3, freq, decay, hy_w):
    n_rows = emb2.shape[0]
    hidden = w2.shape[0]
    tl = min(PREP_TILE, n_rows // 2)
    nblk = n_rows // tl
    half = nblk // 2

    def dirmap(i):
        return (0, jnp.where(i >= half, 1, 0))

    const = lambda i: (0, 0)
    return pl.pallas_call(
        _hyena_filter_body,
        grid=(nblk,),
        in_specs=[pl.BlockSpec((tl, EMB_PAD), lambda i: (i, 0)),
                  pl.BlockSpec((EMB_PAD, hidden), const),
                  pl.BlockSpec((1, hidden), const),
                  pl.BlockSpec((hidden, hidden), const),
                  pl.BlockSpec((1, hidden), const),
                  pl.BlockSpec((hidden, hy_w), dirmap),
                  pl.BlockSpec((1, hy_w), dirmap),
                  pl.BlockSpec((1, hidden), const),
                  pl.BlockSpec((1, hy_w), dirmap)],
        out_specs=[pl.BlockSpec((tl, hy_w), lambda i: (i, 0)),
                   pl.BlockSpec((1, hy_w), const)],
        out_shape=[jax.ShapeDtypeStruct((n_rows, hy_w), F32),
                   jax.ShapeDtypeStruct((1, hy_w), F32)],
        compiler_params=_params("arbitrary"),
        name="hyena_filter",
    )(emb2, w1p, b1.reshape(1, hidden), w2, b2.reshape(1, hidden), w3, b3.reshape(1, -1),
      freq.reshape(1, hidden), decay.reshape(1, -1))


def _left_matmul_body(m_ref, x_ref, o_ref):
    o_ref[0] = jnp.dot(m_ref[...], x_ref[0].astype(BF16), preferred_element_type=F32).astype(o_ref.dtype)


def _left_matmul_post_body(m_ref, x_ref, vin_ref, x0_ref, bias_ref, o_ref):
    y = jnp.dot(m_ref[...], x_ref[0].astype(BF16), preferred_element_type=F32)
    vin = vin_ref[0]
    o_ref[0] = ((y + vin * bias_ref[...]) * x0_ref[0]).astype(o_ref.dtype)


def left_matmul(mat, x, post=None):
    P, K, cols = x.shape
    R = mat.shape[0]
    tc = min(DFT_COLS, cols)
    in_specs = [pl.BlockSpec((R, K), lambda p, c: (0, 0)),
                pl.BlockSpec((1, K, tc), lambda p, c: (p, 0, c))]
    args = [mat, x]
    if post is None:
        body, dtype = _left_matmul_body, BF16
    else:
        vin, x0, bias_row = post
        body, dtype = _left_matmul_post_body, BF16
        in_specs += [pl.BlockSpec((1, R, tc), lambda p, c: (p, 0, c)),
                     pl.BlockSpec((1, R, tc), lambda p, c: (p, 0, c)),
                     pl.BlockSpec((1, tc), lambda p, c: (0, c))]
        args += [vin, x0, bias_row]
    return pl.pallas_call(
        body,
        grid=(P, cols // tc),
        in_specs=in_specs,
        out_specs=pl.BlockSpec((1, R, tc), lambda p, c: (p, 0, c)),
        out_shape=jax.ShapeDtypeStruct((P, R, cols), dtype),
        compiler_params=_params("parallel", "parallel"),
        name="dft_outer",
    )(*args)


def _filter_spectrum_body(g_ref, a_ref, s_ref, k_ref):
    n2 = a_ref.shape[-2]
    a = jnp.concatenate([a_ref[0, 0, 0], a_ref[0, 1, 0]], axis=0).astype(BF16)
    y = jnp.dot(g_ref[0], a, preferred_element_type=F32) * s_ref[...]
    k_ref[0, 0] = y[:n2]
    k_ref[1, 0] = y[n2:]


def filter_spectrum(gs, a5, scale_row):
    _, _, n1, n2, C = a5.shape
    return pl.pallas_call(
        _filter_spectrum_body,
        grid=(n1,),
        in_specs=[pl.BlockSpec((1, 2 * n2, 2 * n2), lambda k: (k, 0, 0)),
                  pl.BlockSpec((1, 2, 1, n2, C), lambda k: (0, 0, k, 0, 0)),
                  pl.BlockSpec((1, C), lambda k: (0, 0))],
        out_specs=pl.BlockSpec((2, 1, n2, C), lambda k: (0, k, 0, 0)),
        out_shape=jax.ShapeDtypeStruct((2, n1, n2, C), F32),
        compiler_params=_params("parallel"),
        name="filter_spectrum",
    )(gs, a5, scale_row)


def _dft_inner_body(g_ref, gi_ref, a_ref, k_ref, p_ref):
    n2 = a_ref.shape[-2]
    a = jnp.concatenate([a_ref[0, 0, 0], a_ref[0, 1, 0]], axis=0).astype(BF16)
    y = jnp.dot(g_ref[0], a, preferred_element_type=F32)
    yr, yi = y[:n2], y[n2:]
    kr, ki = k_ref[0, 0], k_ref[1, 0]
    z = jnp.concatenate([yr * kr - yi * ki, yr * ki + yi * kr], axis=0).astype(BF16)
    p = jnp.dot(gi_ref[0], z, preferred_element_type=F32)
    p_ref[0, 0, 0] = p[:n2].astype(p_ref.dtype)
    p_ref[0, 1, 0] = p[n2:].astype(p_ref.dtype)


def dft_inner(gs, gis, a5, kspec):
    P, _, n1, n2, C = a5.shape
    gspec = pl.BlockSpec((1, 2 * n2, 2 * n2), lambda k, p: (k, 0, 0))
    return pl.pallas_call(
        _dft_inner_body,
        grid=(n1, P),
        in_specs=[gspec, gspec,
                  pl.BlockSpec((1, 2, 1, n2, C), lambda k, p: (p, 0, k, 0, 0)),
                  pl.BlockSpec((2, 1, n2, C), lambda k, p: (0, k, 0, 0))],
        out_specs=pl.BlockSpec((1, 2, 1, n2, C), lambda k, p: (p, 0, k, 0, 0)),
        out_shape=jax.ShapeDtypeStruct((P, 2, n1, n2, C), BF16),
        compiler_params=_params("parallel", "parallel"),
        name="dft_inner",
    )(gs, gis, a5, kspec)


def _dft_tables(n1, n2):
    n = n1 * n2
    h = n1 // 2
    a = np.arange(n1)
    ang1 = -2.0 * np.pi * ((a[:, None] * a[None, :]) % n1) / n1
    f1r, f1i = np.cos(ang1), np.sin(ang1)
    fwd_sig = np.block([[f1r[:, :h], -f1i[:, :h]], [f1i[:, :h], f1r[:, :h]]])
    fwd_flt = np.concatenate([f1r, f1i], axis=0)
    inv = np.block([[f1r[:h], f1i[:h]], [-f1i[:h], f1r[:h]]])
    b = np.arange(n2)
    ang_t = -2.0 * np.pi * ((np.arange(n1)[:, None] * b[None, :]) % n) / n
    ang_f = -2.0 * np.pi * ((b[:, None] * b[None, :]) % n2) / n2
    twr, twi = jnp.asarray(np.cos(ang_t), F32)[:, None, :], jnp.asarray(np.sin(ang_t), F32)[:, None, :]
    f2r, f2i = jnp.asarray(np.cos(ang_f), F32)[None], jnp.asarray(np.sin(ang_f), F32)[None]
    gr = twr * f2r - twi * f2i
    gi = twr * f2i + twi * f2r
    gs = jnp.concatenate([jnp.concatenate([gr, -gi], axis=2), jnp.concatenate([gi, gr], axis=2)], axis=1)
    gis = jnp.swapaxes(gs, 1, 2) * (1.0 / n)
    cvt = lambda m: jnp.asarray(m, dtype=F32).astype(BF16)
    return cvt(fwd_sig), cvt(fwd_flt), cvt(inv), gs.astype(BF16), gis.astype(BF16)


def _filter_embedding(L):
    t = jnp.linspace(0.0, 1.0, L, dtype=F32)
    w = 2.0 * math.pi * jnp.arange(L, dtype=F32) / L
    bands = jnp.linspace(1e-4, HY_BANDS - 1, HY_BANDS, dtype=F32)
    fw = w[:, None] * bands[None, :]
    emb = jnp.concatenate([t[:, None], jnp.cos(fw), -jnp.sin(fw)], axis=-1)
    valid = jnp.ones((L, 1), F32)
    emb = jnp.concatenate([emb, valid, jnp.zeros((L, EMB_PAD - HY_EMB - 1), F32)], axis=-1)
    back = jnp.concatenate([jnp.zeros((1, EMB_PAD), F32), emb[:0:-1]], axis=0)
    return jnp.concatenate([emb, back], axis=0)


def hyena_branch(z3, lp, hy_w, off_hy):
    B, L, _ = z3.shape
    n = 2 * L
    n2 = DFT_N2
    n1 = n // n2
    P = B // 2
    vin, x0 = hyena_pre(z3, lp['hy_conv_w'], lp['hy_conv_b'], hy_w, off_hy)
    fwd_sig, fwd_flt, inv, gs, gis = _dft_tables(n1, n2)

    hidden = lp['hf_w2'].shape[0]
    w1p = jnp.concatenate([lp['hf_w1'], jnp.zeros((EMB_PAD - HY_EMB, hidden), F32)], axis=0)
    kf, ss = hyena_filter(_filter_embedding(L), w1p, lp['hf_b1'], lp['hf_w2'], lp['hf_b2'], lp['hf_w3'],
                          lp['hf_b3'], lp['hf_freq'], lp['hy_decay'], hy_w)
    kscale = lax.rsqrt(ss + EPS)
    ka = left_matmul(fwd_flt, kf.reshape(1, n1, n2 * hy_w))
    kspec = filter_spectrum(gs, ka.reshape(1, 2, n1, n2, hy_w), kscale)

    a = left_matmul(fwd_sig, vin.reshape(P, n1, n2 * hy_w))
    pm = dft_inner(gs, gis, a.reshape(P, 2, n1, n2, hy_w), kspec)
    pstack = pm.reshape(P, 2 * n1, n2 * hy_w)
    bias_row = jnp.tile(lp['hy_bias'].reshape(1, hy_w), (1, n2))
    out = left_matmul(inv, pstack, post=(vin.reshape(P, n1, n2 * hy_w), x0.reshape(P, n1, n2 * hy_w), bias_row))
    return out.reshape(B, L, hy_w)


def _merge_body(a_ref, h_ref, m_ref, wa_ref, wh_ref, wm_ref, zg0_ref, zg1_ref, zg2_ref,
                bg0_ref, bg1_ref, bg2_ref, o_ref):
    def term(x_ref, w_ref, zg_ref, bg_ref):
        gate = jax.nn.sigmoid(zg_ref[...].astype(F32) + bg_ref[...])
        return gate * jnp.dot(x_ref[...], w_ref[...], preferred_element_type=F32)

    acc = term(a_ref, wa_ref, zg0_ref, bg0_ref)
    acc = acc + term(h_ref, wh_ref, zg1_ref, bg1_ref)
    acc = acc + term(m_ref, wm_ref, zg2_ref, bg2_ref)
    o_ref[...] = acc.astype(o_ref.dtype)


def gated_merge(a_out, h_out, m_out, wa, wh, wm, z2, b_gate, off_gate):
    T = a_out.shape[0]
    D = wa.shape[1]
    tm = min(ROW_TILE, T)
    tn = COL_TILE
    nj = D // tn
    gb = off_gate // tn

    def xspec(width):
        return pl.BlockSpec((tm, width), lambda i, j: (i, 0))

    def wspec(width):
        return pl.BlockSpec((width, tn), lambda i, j: (0, j))

    def zgspec(b):
        return pl.BlockSpec((tm, tn), lambda i, j: (i, gb + b * nj + j))

    def bgspec(b):
        return pl.BlockSpec((1, tn), lambda i, j: (0, b * nj + j))

    bg = b_gate.reshape(1, -1)
    return pl.pallas_call(
        _merge_body,
        grid=(T // tm, nj),
        in_specs=[xspec(a_out.shape[1]), xspec(h_out.shape[1]), xspec(m_out.shape[1]),
                  wspec(wa.shape[0]), wspec(wh.shape[0]), wspec(wm.shape[0]),
                  zgspec(0), zgspec(1), zgspec(2), bgspec(0), bgspec(1), bgspec(2)],
        out_specs=pl.BlockSpec((tm, tn), lambda i, j: (i, j)),
        out_shape=jax.ShapeDtypeStruct((T, D), BF16),
        compiler_params=_params("parallel", "parallel"),
        name="gated_merge",
    )(a_out, h_out, m_out, wa, wh, wm, z2, z2, z2, bg, bg, bg)


def _slab_store(ref, x):
    rows, width = x.shape
    n = width // LANES
    for s in range(n):
        ref[pl.ds(s, rows, stride=n), :] = x[:, s * LANES:(s + 1) * LANES]


def _slab_load(ref, n):
    rows = ref.shape[0] // n
    return jnp.concatenate([ref[pl.ds(s, rows, stride=n), :] for s in range(n)], axis=-1)


ROUTE_E, ROUTE_W, ROUTE_RANK = 0, 2, 4


def _route_tile(lg, carry):
    tm = lg.shape[0]
    lanef = lax.broadcasted_iota(jnp.int32, lg.shape, 1).astype(F32)
    row_max = lambda x: jnp.max(x, axis=-1, keepdims=True)
    first_at = lambda x, v: jnp.min(jnp.where(x == v, lanef, float(LANES)), axis=-1, keepdims=True)

    gl = jnp.where(lanef < N_GROUPS, lg, MASK_VALUE)
    gmax = row_max(gl)
    g_p = 1.0 / jnp.sum(jnp.where(lanef < N_GROUPS, jnp.exp(gl - gmax), 0.0), axis=-1, keepdims=True)
    lo = N_GROUPS + first_at(gl, gmax) * EXPERTS_PER_GROUP
    el = jnp.where((lanef >= lo) & (lanef < lo + EXPERTS_PER_GROUP), lg, MASK_VALUE)
    v1 = row_max(el)
    i1 = first_at(el, v1)
    el2 = jnp.where(lanef == i1, MASK_VALUE, el)
    v2 = row_max(el2)
    i2 = first_at(el2, v2)
    t = jnp.exp(v2 - v1)
    w1 = g_p / (1.0 + t)
    w2 = w1 * t
    e1, e2 = i1 - N_GROUPS, i2 - N_GROUPS

    oh1 = jnp.where(lanef == e1, 1.0, 0.0)
    oh2 = jnp.where(lanef == e2, 1.0, 0.0)
    oh = oh1 + oh2
    r = lax.broadcasted_iota(jnp.int32, (tm, tm), 0)
    c = lax.broadcasted_iota(jnp.int32, (tm, tm), 1)
    earlier = jnp.where(c < r, 1.0, 0.0).astype(BF16)
    before = jnp.dot(earlier, oh.astype(BF16), preferred_element_type=F32) + carry
    rank1 = jnp.sum(oh1 * before, axis=-1, keepdims=True)
    rank2 = jnp.sum(oh2 * before, axis=-1, keepdims=True)

    rec = jnp.zeros(lg.shape, F32)
    for lane_id, val in ((ROUTE_E, e1), (ROUTE_E + 1, e2), (ROUTE_W, w1), (ROUTE_W + 1, w2),
                         (ROUTE_RANK, rank1), (ROUTE_RANK + 1, rank2)):
        rec = jnp.where(lanef == lane_id, val, rec)
    return rec, carry + jnp.sum(oh, axis=0, keepdims=True)


def _out_router_body(mg_ref, x_ref, wo_ref, g2_ref, whi_ref, wlo_ref, br_ref, x1_ref, h2_ref, rec_ref, cnt_ref):
    @pl.when(pl.program_id(0) == 0)
    def _():
        cnt_ref[...] = jnp.zeros(cnt_ref.shape, F32)

    x1 = x_ref[...] + jnp.dot(mg_ref[...], wo_ref[...], preferred_element_type=F32)
    x1_ref[...] = x1
    h = _rms(x1, g2_ref[...])
    h_hi = h.astype(BF16)
    _slab_store(h2_ref, h_hi.astype(F32))
    h_lo = (h - h_hi.astype(F32)).astype(BF16)
    lg = jnp.dot(h_hi, whi_ref[...], preferred_element_type=F32)
    lg = lg + jnp.dot(h_lo, whi_ref[...], preferred_element_type=F32)
    lg = lg + jnp.dot(h_hi, wlo_ref[...], preferred_element_type=F32)
    rec, cnt = _route_tile(lg + br_ref[...], cnt_ref[...])
    rec_ref[...] = rec
    cnt_ref[...] = cnt


def out_router(merged, x2, w_out, g2, wr_hi, wr_lo, br):
    T, D = x2.shape
    tm = min(PREP_TILE, T)
    const = lambda i: (0, 0)
    row = lambda i: (i, 0)
    return pl.pallas_call(
        _out_router_body,
        grid=(T // tm,),
        in_specs=[pl.BlockSpec((tm, D), row), pl.BlockSpec((tm, D), row),
                  pl.BlockSpec((D, D), const), pl.BlockSpec((1, D), const),
                  pl.BlockSpec((D, ROUTER_PAD), const), pl.BlockSpec((D, ROUTER_PAD), const),
                  pl.BlockSpec((1, ROUTER_PAD), const)],
        out_specs=[pl.BlockSpec((tm, D), row), pl.BlockSpec((tm * (D // LANES), LANES), row),
                   pl.BlockSpec((tm, ROUTER_PAD), row), pl.BlockSpec((1, ROUTER_PAD), const)],
        out_shape=[jax.ShapeDtypeStruct((T, D), F32), jax.ShapeDtypeStruct((T * (D // LANES), LANES), F32),
                   jax.ShapeDtypeStruct((T, ROUTER_PAD), F32), jax.ShapeDtypeStruct((1, ROUTER_PAD), F32)],
        compiler_params=_params("arbitrary"),
        name="out_router",
    )(merged, x2, w_out, g2.reshape(1, D), wr_hi, wr_lo, br)


def _row_copy(src_ref, src_row, dst_ref, dst_row, n, sem):
    src = src_ref.at[pl.ds(pl.multiple_of(src_row * n, n), n), :]
    dst = dst_ref.at[pl.ds(pl.multiple_of(dst_row * n, n), n), :]
    return pltpu.make_async_copy(src, dst, sem)


def _for_rows(rows, fn):
    def group(gi, carry):
        for u in range(DMA_UNROLL):
            fn(gi * DMA_UNROLL + u)
        return carry

    lax.fori_loop(0, rows // DMA_UNROLL, group, 0)


def _dispatch_body(dest_ref, h_ref, xs_in_ref, xs_ref, sem, *, n):
    del xs_in_ref
    rows = h_ref.shape[0] // n
    base = pl.program_id(0) * rows

    def copies(r):
        a = TOP_K * (base + r)
        return [_row_copy(h_ref, r, xs_ref, dest_ref[a + j], n, sem) for j in range(TOP_K)]

    def issue(r):
        for c in copies(r):
            c.start()

    def drain(r):
        for c in copies(r):
            c.wait()

    _for_rows(rows, issue)
    _for_rows(rows, drain)


def dispatch_rows(dest_flat, h_slab, n, P):
    T = h_slab.shape[0] // n
    rows = min(GATHER_ROWS, T)
    xs0 = jnp.zeros((P * n, LANES), h_slab.dtype)
    return pl.pallas_call(
        functools.partial(_dispatch_body, n=n),
        grid_spec=pltpu.PrefetchScalarGridSpec(
            num_scalar_prefetch=1,
            grid=(T // rows,),
            in_specs=[pl.BlockSpec((rows * n, LANES), lambda i, dest: (i, 0)),
                      pl.BlockSpec(memory_space=pl.ANY)],
            out_specs=pl.BlockSpec(memory_space=pl.ANY),
            scratch_shapes=[pltpu.SemaphoreType.DMA(())]),
        out_shape=jax.ShapeDtypeStruct((P * n, LANES), h_slab.dtype),
        input_output_aliases={2: 0},
        compiler_params=_params("arbitrary"),
        name="dispatch_rows",
    )(dest_flat, h_slab, xs0)


def _moe_body(blk_e_ref, n_used_ref, x_ref, wg_ref, wu_ref, wd_ref, o_ref):
    del blk_e_ref
    i = pl.program_id(0)
    n = wg_ref.shape[1] // LANES

    @pl.when(i < n_used_ref[0])
    def _():
        x = _slab_load(x_ref, n).astype(BF16)
        g = jnp.dot(x, wg_ref[0], preferred_element_type=F32)
        u = jnp.dot(x, wu_ref[0], preferred_element_type=F32)
        a = (g * jax.nn.sigmoid(g) * u).astype(BF16)
        _slab_store(o_ref, jnp.dot(a, wd_ref[0], preferred_element_type=F32))

    @pl.when(i >= n_used_ref[0])
    def _():
        o_ref[...] = jnp.zeros(o_ref.shape, o_ref.dtype)


def moe_experts(blk_e, n_used, xs_slab, wg, wu, wd):
    _, D, De = wg.shape
    n = D // LANES
    P = xs_slab.shape[0] // n
    tm = MOE_TM
    return pl.pallas_call(
        _moe_body,
        grid_spec=pltpu.PrefetchScalarGridSpec(
            num_scalar_prefetch=2,
            grid=(P // tm,),
            in_specs=[pl.BlockSpec((tm * n, LANES), lambda i, be, nu: (i, 0)),
                      pl.BlockSpec((1, D, De), lambda i, be, nu: (be[i], 0, 0)),
                      pl.BlockSpec((1, D, De), lambda i, be, nu: (be[i], 0, 0)),
                      pl.BlockSpec((1, De, D), lambda i, be, nu: (be[i], 0, 0))],
            out_specs=pl.BlockSpec((tm * n, LANES), lambda i, be, nu: (i, 0))),
        out_shape=jax.ShapeDtypeStruct((P * n, LANES), F32),
        compiler_params=_params("arbitrary"),
        name="moe_experts",
    )(blk_e, n_used, xs_slab, wg, wu, wd)


def _combine_body(dest_ref, x1_ref, rec_ref, yb_ref, o_ref, *scratch):
    bufs, sem = scratch[:TOP_K], scratch[TOP_K]
    tg, D = x1_ref.shape
    n = D // LANES
    base = pl.program_id(0) * tg

    def copies(r):
        a = TOP_K * (base + r)
        return [_row_copy(yb_ref, dest_ref[a + j], bufs[j], r, n, sem) for j in range(TOP_K)]

    def issue(r):
        for c in copies(r):
            c.start()

    def drain(r):
        for c in copies(r):
            c.wait()

    _for_rows(tg, issue)
    _for_rows(tg, drain)
    w = [jnp.broadcast_to(rec_ref[:, ROUTE_W + j:ROUTE_W + j + 1], (tg, LANES)) for j in range(TOP_K)]
    for s in range(n):
        cols = slice(s * LANES, (s + 1) * LANES)
        y = x1_ref[:, cols]
        for j in range(TOP_K):
            y = y + w[j] * bufs[j][pl.ds(s, tg, stride=n), :]
        o_ref[:, cols] = y


def moe_combine(dest_flat, x1, rec, yb_slab):
    T, D = x1.shape
    tg = min(GATHER_ROWS, T)
    n = D // LANES
    return pl.pallas_call(
        _combine_body,
        grid_spec=pltpu.PrefetchScalarGridSpec(
            num_scalar_prefetch=1,
            grid=(T // tg,),
            in_specs=[pl.BlockSpec((tg, D), lambda i, dest: (i, 0)),
                      pl.BlockSpec((tg, ROUTER_PAD), lambda i, dest: (i, 0)),
                      pl.BlockSpec(memory_space=pl.ANY)],
            out_specs=pl.BlockSpec((tg, D), lambda i, dest: (i, 0)),
            scratch_shapes=[pltpu.VMEM((tg * n, LANES), F32) for _ in range(TOP_K)]
            + [pltpu.SemaphoreType.DMA(())]),
        out_shape=jax.ShapeDtypeStruct((T, D), F32),
        compiler_params=_params("arbitrary"),
        name="moe_combine",
    )(dest_flat, x1, rec, yb_slab)


def _block_layout(rec, counts_row):
    T = rec.shape[0]
    A = T * TOP_K
    P = (A + N_EXPERTS * (MOE_TM - 1) + MOE_TM - 1) // MOE_TM * MOE_TM
    n_blk = P // MOE_TM
    counts = counts_row[0, :N_EXPERTS].astype(jnp.int32)
    padded = (counts + MOE_TM - 1) // MOE_TM * MOE_TM
    pends = jnp.cumsum(padded)
    pstarts = pends - padded
    e = rec[:, ROUTE_E:ROUTE_E + TOP_K].astype(jnp.int32)
    rank = rec[:, ROUTE_RANK:ROUTE_RANK + TOP_K].astype(jnp.int32)
    one_hot = e[:, :, None] == jnp.arange(N_EXPERTS, dtype=jnp.int32)
    dest = jnp.sum(jnp.where(one_hot, pstarts, 0), axis=-1) + rank
    blk_start = jnp.arange(n_blk, dtype=jnp.int32) * MOE_TM
    blk_e = jnp.minimum(jnp.sum(blk_start[:, None] >= pends[None, :], axis=-1), N_EXPERTS - 1).astype(jnp.int32)
    n_used = (pends[-1:] // MOE_TM).astype(jnp.int32)
    return dest.reshape(A), blk_e, n_used, P


def _rope_tables(L):
    rows = L // GRID_W
    pairs = HEAD_DIM // 4
    row = jnp.repeat(jnp.arange(rows, dtype=F32), GRID_W)
    col = jnp.tile(jnp.arange(GRID_W, dtype=F32), rows)
    inv = jnp.power(ROPE_THETA, -jnp.arange(pairs, dtype=F32) / pairs)
    ang = jnp.concatenate([row[:, None] * inv, col[:, None] * inv], axis=-1)
    cos, sin = jnp.cos(ang), jnp.sin(ang)
    cosf = jnp.repeat(cos, 2, axis=-1)
    sinf = jnp.stack([-sin, sin], axis=-1).reshape(L, HEAD_DIM)
    return cosf, sinf


def _trunk(x, mem, lp, wb):
    B, L, D = x.shape
    T = B * L
    att_w = lp['w_br_attn'].shape[0]
    hy_w = lp['w_br_hyena'].shape[0]
    mem_w = lp['w_br_mem'].shape[0]
    kv_w = (lp['w_in'].shape[1] - att_w - 3 * hy_w - mem_w - 3 * D) // 2
    off_hy = att_w + 2 * kv_w
    off_mq = off_hy + 3 * hy_w
    off_gate = off_mq + mem_w
    x2 = x.reshape(T, D)

    z2 = normed_matmul(x2, lp['norm1_g'], wb['w_in'], BF16)
    z3 = z2.reshape(B, L, -1)
    cosf, sinf = _rope_tables(L)
    q, k2, v2, stats = qkv_prep(z3, cosf, sinf, lp['q_norm_g'], lp['k_norm_g'], att_w, kv_w)
    a_out = flash_attention(q, k2, v2, stats)
    h_out = hyena_branch(z3, lp, hy_w, off_hy)
    M = mem.shape[1]
    kv = normed_matmul(mem.reshape(B * M, D), lp['mem_norm_g'], wb['w_mem_kv'], F32).reshape(B, M, -1)
    m_out = mem_attention(z3, kv, lp['mq_norm_g'], lp['mk_norm_g'], mem_w, off_mq)
    merged = gated_merge(a_out.reshape(T, att_w), h_out.reshape(T, hy_w), m_out.reshape(T, mem_w),
                         wb['w_br_attn'], wb['w_br_hyena'], wb['w_br_mem'], z2, lp['b_gate'], off_gate)
    x1, h2, rec, counts = out_router(merged, x2, wb['w_out'], lp['norm2_g'], wb['wr_hi'], wb['wr_lo'], wb['br'])

    dest_flat, blk_e, n_used, P = _block_layout(rec, counts)
    xs = dispatch_rows(dest_flat, h2, D // LANES, P)
    yb = moe_experts(blk_e, n_used, xs, wb['w_gate_e'], wb['w_up_e'], wb['w_down_e'])
    y = moe_combine(dest_flat, x1, rec, yb)
    return y.reshape(B, L, D)


def kernel(x_prompt, x_sample, mem_prompt, mem_sample, norm1_g, w_in, b_gate, q_norm_g, k_norm_g, hy_conv_w, hy_conv_b, hf_w1, hf_b1, hf_w2, hf_b2, hf_w3, hf_b3, hf_freq, hy_decay, hy_bias, mem_norm_g, w_mem_kv, mq_norm_g, mk_norm_g, w_br_attn, w_br_hyena, w_br_mem, w_out, norm2_g, w_router_group, b_router_group, w_router_expert, b_router_expert, w_gate_e, w_up_e, w_down_e):
    params = dict(norm1_g=norm1_g, w_in=w_in, b_gate=b_gate, q_norm_g=q_norm_g, k_norm_g=k_norm_g,
                  hy_conv_w=hy_conv_w, hy_conv_b=hy_conv_b, hf_w1=hf_w1, hf_b1=hf_b1, hf_w2=hf_w2,
                  hf_b2=hf_b2, hf_w3=hf_w3, hf_b3=hf_b3, hf_freq=hf_freq, hy_decay=hy_decay,
                  hy_bias=hy_bias, mem_norm_g=mem_norm_g, w_mem_kv=w_mem_kv, mq_norm_g=mq_norm_g,
                  mk_norm_g=mk_norm_g, w_br_attn=w_br_attn, w_br_hyena=w_br_hyena, w_br_mem=w_br_mem,
                  w_out=w_out, norm2_g=norm2_g, w_router_group=w_router_group,
                  b_router_group=b_router_group, w_router_expert=w_router_expert,
                  b_router_expert=b_router_expert, w_gate_e=w_gate_e, w_up_e=w_up_e, w_down_e=w_down_e)
    depth = w_in.shape[0]
    xp, xs = x_prompt, x_sample
    for d in range(depth):
        lp = {name: arr[d] for name, arr in params.items()}
        D = lp['w_in'].shape[0]
        wb = {name: lp[name].astype(BF16) for name in
              ('w_in', 'w_mem_kv', 'w_br_attn', 'w_br_hyena', 'w_br_mem', 'w_out',
               'w_gate_e', 'w_up_e', 'w_down_e')}
        wr = jnp.concatenate([lp['w_router_group'], lp['w_router_expert'],
                              jnp.zeros((D, ROUTER_PAD - N_GROUPS - N_EXPERTS), F32)], axis=1)
        wb['wr_hi'] = wr.astype(BF16)
        wb['wr_lo'] = (wr - wb['wr_hi'].astype(F32)).astype(BF16)
        wb['br'] = jnp.concatenate([lp['b_router_group'], lp['b_router_expert'],
                                    jnp.zeros((ROUTER_PAD - N_GROUPS - N_EXPERTS,), F32)]).reshape(1, ROUTER_PAD)
        xp = _trunk(xp, mem_prompt, lp, wb)
        xs = _trunk(xs, mem_sample, lp, wb)
    return (xp, xs)
```

```python
import functools
import math

import jax
import jax.numpy as jnp
import numpy as np
from jax import lax
from jax.experimental import pallas as pl
from jax.experimental.pallas import tpu as pltpu

F32 = jnp.float32
BF16 = jnp.bfloat16

HEAD_DIM = 128
KV_GROUP = 4
GRID_W = 64
ROPE_THETA = 10000.0
HY_BANDS = 8
HY_EMB = 1 + 2 * HY_BANDS
HY_SHORT = 3
MEM_HEADS = 4
N_GROUPS = 4
EXPERTS_PER_GROUP = 8
N_EXPERTS = N_GROUPS * EXPERTS_PER_GROUP
TOP_K = 2
EPS = 1e-6
MASK_VALUE = -1e30
EXP2_SAFE_RANGE = 120.0
LANES = 128
MXU_DIM = 256
VMEM_LIMIT_BYTES = 52 * 1024 * 1024

ROW_TILE = 1024
COL_TILE = 512
PREP_TILE = 512
ATT_TQ = 1024
ATT_TK = 2048
DFT_N2 = 256
DFT_COLS = 2048
MOE_TM = 256
GATHER_ROWS = 512
DMA_UNROLL = 8
EMB_PAD = 32
ROUTER_PAD = 128


def _params(*sem):
    return pltpu.CompilerParams(dimension_semantics=sem, vmem_limit_bytes=VMEM_LIMIT_BYTES)


def _rms(x, g):
    ms = jnp.mean(x * x, axis=-1, keepdims=True)
    return x * lax.rsqrt(ms + EPS) * g


def _normed_matmul_body(x_ref, g_ref, w_ref, o_ref, xn_ref):
    @pl.when(pl.program_id(1) == 0)
    def _():
        xn_ref[...] = _rms(x_ref[...], g_ref[...]).astype(BF16)

    o_ref[...] = jnp.dot(xn_ref[...], w_ref[...], preferred_element_type=F32).astype(o_ref.dtype)


def normed_matmul(x, g, w, out_dtype):
    T, D = x.shape
    N = w.shape[1]
    tm = min(ROW_TILE, T)
    tn = COL_TILE
    return pl.pallas_call(
        _normed_matmul_body,
        grid=(T // tm, N // tn),
        in_specs=[pl.BlockSpec((tm, D), lambda i, j: (i, 0)),
                  pl.BlockSpec((1, D), lambda i, j: (0, 0)),
                  pl.BlockSpec((D, tn), lambda i, j: (0, j))],
        out_specs=pl.BlockSpec((tm, tn), lambda i, j: (i, j)),
        out_shape=jax.ShapeDtypeStruct((T, N), out_dtype),
        scratch_shapes=[pltpu.VMEM((tm, D), BF16)],
        compiler_params=_params("parallel", "arbitrary"),
        name="normed_matmul",
    )(x, g.reshape(1, D), w)


def _qkv_prep_body(zq_ref, zk_ref, zv_ref, cos_ref, sin_ref, gq_ref, gk_ref, q_ref, k_ref, v_ref, st_ref, *, scale):
    cosf = cos_ref[...]
    sinf = sin_ref[...]
    tl = cosf.shape[0]
    lane = lax.broadcasted_iota(jnp.int32, cosf.shape, 1)
    even = (lane % 2) == 0
    one_hot0 = jnp.where(lane == 0, 1.0, 0.0).astype(BF16)

    def prep(x, g):
        xn = _rms(x, g)
        partner = jnp.where(even, pltpu.roll(xn, LANES - 1, 1), pltpu.roll(xn, 1, 1))
        return xn * cosf + partner * sinf

    def max_norm2(xb):
        xf = xb.astype(F32)
        return jnp.max(jnp.sum(xf * xf, axis=-1, keepdims=True), axis=0, keepdims=True)

    qmax = jnp.zeros((1, 1), F32)
    for h in range(zq_ref.shape[-1] // HEAD_DIM):
        sl = slice(h * HEAD_DIM, (h + 1) * HEAD_DIM)
        qb = (prep(zq_ref[0, :, sl].astype(F32), gq_ref[...]) * scale).astype(BF16)
        q_ref[0, :, sl] = qb
        qmax = jnp.maximum(qmax, max_norm2(qb))
    stats = [qmax]
    for h in range(zk_ref.shape[-1] // HEAD_DIM):
        sl = slice(h * HEAD_DIM, (h + 1) * HEAD_DIM)
        kb = prep(zk_ref[0, :, sl].astype(F32), gk_ref[...]).astype(BF16)
        k_ref[0, :, 2 * h * HEAD_DIM:(2 * h + 1) * HEAD_DIM] = kb
        k_ref[0, :, (2 * h + 1) * HEAD_DIM:(2 * h + 2) * HEAD_DIM] = one_hot0
        v_ref[0, :, 2 * h * HEAD_DIM:(2 * h + 1) * HEAD_DIM] = zv_ref[0, :, sl].astype(BF16)
        v_ref[0, :, (2 * h + 1) * HEAD_DIM:(2 * h + 2) * HEAD_DIM] = one_hot0
        stats.append(max_norm2(kb))
    row = lax.broadcasted_iota(jnp.int32, (8, LANES), 0)
    st = jnp.zeros((8, LANES), F32)
    for r, val in enumerate(stats):
        st = jnp.where(row == r, val, st)
    st_ref[0, 0] = st


def qkv_prep(z3, cosf, sinf, gq, gk, att_w, kv_w):
    B, L, _ = z3.shape
    tl = min(PREP_TILE, L)
    off_k = att_w // kv_w
    body = functools.partial(_qkv_prep_body, scale=HEAD_DIM ** -0.5 * math.log2(math.e))
    return pl.pallas_call(
        body,
        grid=(B, L // tl),
        in_specs=[pl.BlockSpec((1, tl, att_w), lambda b, i: (b, i, 0)),
                  pl.BlockSpec((1, tl, kv_w), lambda b, i: (b, i, off_k)),
                  pl.BlockSpec((1, tl, kv_w), lambda b, i: (b, i, off_k + 1)),
                  pl.BlockSpec((tl, HEAD_DIM), lambda b, i: (i, 0)),
                  pl.BlockSpec((tl, HEAD_DIM), lambda b, i: (i, 0)),
                  pl.BlockSpec((1, HEAD_DIM), lambda b, i: (0, 0)),
                  pl.BlockSpec((1, HEAD_DIM), lambda b, i: (0, 0))],
        out_specs=[pl.BlockSpec((1, tl, att_w), lambda b, i: (b, i, 0)),
                   pl.BlockSpec((1, tl, 2 * kv_w), lambda b, i: (b, i, 0)),
                   pl.BlockSpec((1, tl, 2 * kv_w), lambda b, i: (b, i, 0)),
                   pl.BlockSpec((1, 1, 8, LANES), lambda b, i: (b, i, 0, 0))],
        out_shape=[jax.ShapeDtypeStruct((B, L, att_w), BF16),
                   jax.ShapeDtypeStruct((B, L, 2 * kv_w), BF16),
                   jax.ShapeDtypeStruct((B, L, 2 * kv_w), BF16),
                   jax.ShapeDtypeStruct((B, L // tl, 8, LANES), F32)],
        compiler_params=_params("parallel", "parallel"),
        name="qkv_prep",
    )(z3, z3, z3, cosf, sinf, gq.reshape(1, HEAD_DIM), gk.reshape(1, HEAD_DIM))


def _flash_body(fixed_ref, kmax_ref, q_ref, k_ref, v_ref, o_ref, qx_ref, acc_ref, m_ref, *, n_k, n_kv):
    b, g, kk = pl.program_id(0), pl.program_id(1), pl.program_id(3)
    nt_dims = (((1,), (1,)), ((), ()))
    tq = qx_ref.shape[1]

    @pl.when(kk == 0)
    def _():
        acc_ref[...] = jnp.zeros(acc_ref.shape, F32)
        m_ref[...] = jnp.full(m_ref.shape, MASK_VALUE, F32)
        kmax = kmax_ref[b * n_kv + g]
        lane0 = lax.broadcasted_iota(jnp.int32, (tq, HEAD_DIM), 1) == 0
        for h in range(KV_GROUP):
            q = q_ref[0, :, h * HEAD_DIM:(h + 1) * HEAD_DIM]
            qf = q.astype(F32)
            bound = jnp.sqrt(jnp.sum(qf * qf, axis=-1, keepdims=True)) * kmax
            qx_ref[h, :, :HEAD_DIM] = q
            qx_ref[h, :, HEAD_DIM:] = jnp.where(lane0, -bound, 0.0).astype(BF16)

    k = k_ref[0]
    v = v_ref[0]

    @pl.when(fixed_ref[0] == 1)
    def _():
        for h in range(KV_GROUP):
            s = lax.dot_general(qx_ref[h], k, nt_dims, preferred_element_type=F32)
            acc_ref[h] += jnp.dot(jnp.exp2(s).astype(BF16), v, preferred_element_type=F32)

    @pl.when(fixed_ref[0] == 0)
    def _():
        for h in range(KV_GROUP):
            s = lax.dot_general(qx_ref[h, :, :HEAD_DIM], k[:, :HEAD_DIM], nt_dims, preferred_element_type=F32)
            m_prev = m_ref[h]
            m_new = jnp.maximum(m_prev, jnp.max(s, axis=-1, keepdims=True))
            p = jnp.exp2(s - m_new).astype(BF16)
            acc_ref[h] = jnp.exp2(m_prev - m_new) * acc_ref[h] + jnp.dot(p, v, preferred_element_type=F32)
            m_ref[h] = m_new

    @pl.when(kk == n_k - 1)
    def _():
        for h in range(KV_GROUP):
            a = acc_ref[h]
            o_ref[0, :, h * HEAD_DIM:(h + 1) * HEAD_DIM] = (
                a[:, :HEAD_DIM] / a[:, HEAD_DIM:HEAD_DIM + 1]).astype(o_ref.dtype)


def flash_attention(q, k2, v2, stats):
    B, L, att_w = q.shape
    n_kv = k2.shape[-1] // (2 * HEAD_DIM)
    tq = min(ATT_TQ, L)
    tk = min(ATT_TK, L)
    gw = KV_GROUP * HEAD_DIM
    smax = jnp.sqrt(jnp.max(stats[:, :, :1 + n_kv, 0], axis=1))
    kmax = smax[:, 1:].reshape(B * n_kv)
    fixed = (2.0 * jnp.max(smax[:, :1] * smax[:, 1:]) < EXP2_SAFE_RANGE).astype(jnp.int32).reshape(1)
    return pl.pallas_call(
        functools.partial(_flash_body, n_k=L // tk, n_kv=n_kv),
        grid_spec=pltpu.PrefetchScalarGridSpec(
            num_scalar_prefetch=2,
            grid=(B, n_kv, L // tq, L // tk),
            in_specs=[pl.BlockSpec((1, tq, gw), lambda b, g, i, kk, *_: (b, i, g)),
                      pl.BlockSpec((1, tk, 2 * HEAD_DIM), lambda b, g, i, kk, *_: (b, kk, g)),
                      pl.BlockSpec((1, tk, 2 * HEAD_DIM), lambda b, g, i, kk, *_: (b, kk, g))],
            out_specs=pl.BlockSpec((1, tq, gw), lambda b, g, i, kk, *_: (b, i, g)),
            scratch_shapes=[pltpu.VMEM((KV_GROUP, tq, 2 * HEAD_DIM), BF16),
                            pltpu.VMEM((KV_GROUP, tq, 2 * HEAD_DIM), F32),
                            pltpu.VMEM((KV_GROUP, tq, 1), F32)]),
        out_shape=jax.ShapeDtypeStruct((B, L, att_w), BF16),
        compiler_params=_params("parallel", "parallel", "parallel", "arbitrary"),
        name="flash_attention",
    )(fixed, kmax, q, k2, v2)


def _mem_attn_body(zm_ref, kv_ref, gq_ref, gk_ref, o_ref, *, scale):
    mem_w = zm_ref.shape[-1]
    for h in range(mem_w // HEAD_DIM):
        sl = slice(h * HEAD_DIM, (h + 1) * HEAD_DIM)
        q = (_rms(zm_ref[0, :, sl].astype(F32), gq_ref[...]) * scale).astype(BF16)
        k = _rms(kv_ref[0, :, sl], gk_ref[...]).astype(BF16)
        v = kv_ref[0, :, mem_w + h * HEAD_DIM:mem_w + (h + 1) * HEAD_DIM].astype(BF16)
        s = lax.dot_general(q, k, (((1,), (1,)), ((), ())), preferred_element_type=F32)
        p = jnp.exp(s - jnp.max(s, axis=-1, keepdims=True))
        o = jnp.dot(p.astype(BF16), v, preferred_element_type=F32)
        o_ref[0, :, sl] = (o / jnp.sum(p, axis=-1, keepdims=True)).astype(o_ref.dtype)


def mem_attention(z3, kv, gq, gk, mem_w, off_mq):
    B, L, _ = z3.shape
    M = kv.shape[1]
    tl = min(PREP_TILE, L)
    body = functools.partial(_mem_attn_body, scale=HEAD_DIM ** -0.5)
    return pl.pallas_call(
        body,
        grid=(B, L // tl),
        in_specs=[pl.BlockSpec((1, tl, mem_w), lambda b, i: (b, i, off_mq // mem_w)),
                  pl.BlockSpec((1, M, 2 * mem_w), lambda b, i: (b, 0, 0)),
                  pl.BlockSpec((1, HEAD_DIM), lambda b, i: (0, 0)),
                  pl.BlockSpec((1, HEAD_DIM), lambda b, i: (0, 0))],
        out_specs=pl.BlockSpec((1, tl, mem_w), lambda b, i: (b, i, 0)),
        out_shape=jax.ShapeDtypeStruct((B, L, mem_w), BF16),
        compiler_params=_params("parallel", "parallel"),
        name="mem_attention",
    )(z3, kv, gq.reshape(1, HEAD_DIM), gk.reshape(1, HEAD_DIM))


def _hyena_pre_body(z_ref, prev_ref, next_ref, w_ref, b_ref, vin_ref, x0_ref, *, n_tiles):
    i = pl.program_id(1)
    x = z_ref[0].astype(F32)
    tl, width = x.shape
    hy_w = width // 3
    row = lax.broadcasted_iota(jnp.int32, x.shape, 0)
    halo = prev_ref.shape[1]
    prev_row = jnp.where(i > 0, prev_ref[0, halo - 1:halo, :].astype(F32), 0.0)
    next_row = jnp.where(i < n_tiles - 1, next_ref[0, 0:1, :].astype(F32), 0.0)
    x_prev = jnp.where(row == 0, prev_row, pltpu.roll(x, 1, 0))
    x_next = jnp.where(row == tl - 1, next_row, pltpu.roll(x, tl - 1, 0))
    u = w_ref[0:1, :] * x_prev + w_ref[1:2, :] * x + w_ref[2:3, :] * x_next + b_ref[...]
    vin_ref[0] = u[:, 2 * hy_w:] * u[:, hy_w:2 * hy_w]
    x0_ref[0] = u[:, :hy_w]


def hyena_pre(z3, conv_w, conv_b, hy_w, off_hy):
    B, L, _ = z3.shape
    width = 3 * hy_w
    tl = min(PREP_TILE, L)
    cb = off_hy // width
    halo = 16
    r8 = tl // halo
    last8 = L // halo - 1
    return pl.pallas_call(
        functools.partial(_hyena_pre_body, n_tiles=L // tl),
        grid=(B, L // tl),
        in_specs=[pl.BlockSpec((1, tl, width), lambda b, i: (b, i, cb)),
                  pl.BlockSpec((1, halo, width), lambda b, i: (b, jnp.maximum(i * r8 - 1, 0), cb)),
                  pl.BlockSpec((1, halo, width), lambda b, i: (b, jnp.minimum((i + 1) * r8, last8), cb)),
                  pl.BlockSpec((HY_SHORT, width), lambda b, i: (0, 0)),
                  pl.BlockSpec((1, width), lambda b, i: (0, 0))],
        out_specs=[pl.BlockSpec((1, tl, hy_w), lambda b, i: (b, i, 0)),
                   pl.BlockSpec((1, tl, hy_w), lambda b, i: (b, i, 0))],
        out_shape=[jax.ShapeDtypeStruct((B, L, hy_w), F32),
                   jax.ShapeDtypeStruct((B, L, hy_w), F32)],
        compiler_params=_params("parallel", "parallel"),
        name="hyena_pre",
    )(z3, z3, z3, conv_w.reshape(HY_SHORT, width), conv_b.reshape(1, width))


def _hyena_filter_body(e_ref, w1_ref, b1_ref, w2_ref, b2_ref, w3_ref, b3_ref, freq_ref, dec_ref,
                       kf_ref, ss_ref):
    hp = lax.Precision.HIGHEST
    e = e_ref[...]
    t = e[:, 0:1]
    valid = e[:, HY_EMB:HY_EMB + 1]
    freq = freq_ref[...]
    h = jnp.sin(freq * (jnp.dot(e, w1_ref[...], precision=hp, preferred_element_type=F32) + b1_ref[...]))
    h = jnp.sin(freq * (jnp.dot(h, w2_ref[...], precision=hp, preferred_element_type=F32) + b2_ref[...]))
    h = jnp.dot(h, w3_ref[...], precision=hp, preferred_element_type=F32) + b3_ref[...]
    kf = h * jnp.exp(-t * jnp.abs(dec_ref[...])) * valid
    kf_ref[...] = kf

    @pl.when(pl.program_id(0) == 0)
    def _():
        ss_ref[...] = jnp.zeros(ss_ref.shape, F32)

    ss_ref[...] += jnp.sum(kf * kf, axis=0, keepdims=True)


def hyena_filter(emb2, w1p, b1, w2, b2, w3, b3, freq, decay, hy_w):
    n_rows = emb2.shape[0]
    hidden = w2.shape[0]
    tl = min(PREP_TILE, n_rows // 2)
    nblk = n_rows // tl
    half = nblk // 2

    def dirmap(i):
        return (0, jnp.where(i >= half, 1, 0))

    const = lambda i: (0, 0)
    return pl.pallas_call(
        _hyena_filter_body,
        grid=(nblk,),
        in_specs=[pl.BlockSpec((tl, EMB_PAD), lambda i: (i, 0)),
                  pl.BlockSpec((EMB_PAD, hidden), const),
                  pl.BlockSpec((1, hidden), const),
                  pl.BlockSpec((hidden, hidden), const),
                  pl.BlockSpec((1, hidden), const),
                  pl.BlockSpec((hidden, hy_w), dirmap),
                  pl.BlockSpec((1, hy_w), dirmap),
                  pl.BlockSpec((1, hidden), const),
                  pl.BlockSpec((1, hy_w), dirmap)],
        out_specs=[pl.BlockSpec((tl, hy_w), lambda i: (i, 0)),
                   pl.BlockSpec((1, hy_w), const)],
        out_shape=[jax.ShapeDtypeStruct((n_rows, hy_w), F32),
                   jax.ShapeDtypeStruct((1, hy_w), F32)],
        compiler_params=_params("arbitrary"),
        name="hyena_filter",
    )(emb2, w1p, b1.reshape(1, hidden), w2, b2.reshape(1, hidden), w3, b3.reshape(1, -1),
      freq.reshape(1, hidden), decay.reshape(1, -1))


def _left_matmul_body(m_ref, x_ref, o_ref):
    o_ref[0] = jnp.dot(m_ref[...], x_ref[0].astype(BF16), preferred_element_type=F32).astype(o_ref.dtype)


def _left_matmul_post_body(m_ref, x_ref, vin_ref, x0_ref, bias_ref, o_ref):
    y = jnp.dot(m_ref[...], x_ref[0].astype(BF16), preferred_element_type=F32)
    vin = vin_ref[0]
    o_ref[0] = ((y + vin * bias_ref[...]) * x0_ref[0]).astype(o_ref.dtype)


def left_matmul(mat, x, post=None):
    P, K, cols = x.shape
    R = mat.shape[0]
    tc = min(DFT_COLS, cols)
    in_specs = [pl.BlockSpec((R, K), lambda p, c: (0, 0)),
                pl.BlockSpec((1, K, tc), lambda p, c: (p, 0, c))]
    args = [mat, x]
    if post is None:
        body, dtype = _left_matmul_body, BF16
    else:
        vin, x0, bias_row = post
        body, dtype = _left_matmul_post_body, BF16
        in_specs += [pl.BlockSpec((1, R, tc), lambda p, c: (p, 0, c)),
                     pl.BlockSpec((1, R, tc), lambda p, c: (p, 0, c)),
                     pl.BlockSpec((1, tc), lambda p, c: (0, c))]
        args += [vin, x0, bias_row]
    return pl.pallas_call(
        body,
        grid=(P, cols // tc),
        in_specs=in_specs,
        out_specs=pl.BlockSpec((1, R, tc), lambda p, c: (p, 0, c)),
        out_shape=jax.ShapeDtypeStruct((P, R, cols), dtype),
        compiler_params=_params("parallel", "parallel"),
        name="dft_outer",
    )(*args)


def _filter_spectrum_body(g_ref, a_ref, s_ref, k_ref):
    n2 = a_ref.shape[-2]
    a = jnp.concatenate([a_ref[0, 0, 0], a_ref[0, 1, 0]], axis=0).astype(BF16)
    y = jnp.dot(g_ref[0], a, preferred_element_type=F32) * s_ref[...]
    k_ref[0, 0] = y[:n2]
    k_ref[1, 0] = y[n2:]


def filter_spectrum(gs, a5, scale_row):
    _, _, n1, n2, C = a5.shape
    return pl.pallas_call(
        _filter_spectrum_body,
        grid=(n1,),
        in_specs=[pl.BlockSpec((1, 2 * n2, 2 * n2), lambda k: (k, 0, 0)),
                  pl.BlockSpec((1, 2, 1, n2, C), lambda k: (0, 0, k, 0, 0)),
                  pl.BlockSpec((1, C), lambda k: (0, 0))],
        out_specs=pl.BlockSpec((2, 1, n2, C), lambda k: (0, k, 0, 0)),
        out_shape=jax.ShapeDtypeStruct((2, n1, n2, C), F32),
        compiler_params=_params("parallel"),
        name="filter_spectrum",
    )(gs, a5, scale_row)


def _dft_inner_body(g_ref, gi_ref, a_ref, k_ref, p_ref):
    n2 = a_ref.shape[-2]
    a = jnp.concatenate([a_ref[0, 0, 0], a_ref[0, 1, 0]], axis=0).astype(BF16)
    y = jnp.dot(g_ref[0], a, preferred_element_type=F32)
    yr, yi = y[:n2], y[n2:]
    kr, ki = k_ref[0, 0], k_ref[1, 0]
    z = jnp.concatenate([yr * kr - yi * ki, yr * ki + yi * kr], axis=0).astype(BF16)
    p = jnp.dot(gi_ref[0], z, preferred_element_type=F32)
    p_ref[0, 0, 0] = p[:n2].astype(p_ref.dtype)
    p_ref[0, 1, 0] = p[n2:].astype(p_ref.dtype)


def dft_inner(gs, gis, a5, kspec):
    P, _, n1, n2, C = a5.shape
    gspec = pl.BlockSpec((1, 2 * n2, 2 * n2), lambda k, p: (k, 0, 0))
    return pl.pallas_call(
        _dft_inner_body,
        grid=(n1, P),
        in_specs=[gspec, gspec,
                  pl.BlockSpec((1, 2, 1, n2, C), lambda k, p: (p, 0, k, 0, 0)),
                  pl.BlockSpec((2, 1, n2, C), lambda k, p: (0, k, 0, 0))],
        out_specs=pl.BlockSpec((1, 2, 1, n2, C), lambda k, p: (p, 0, k, 0, 0)),
        out_shape=jax.ShapeDtypeStruct((P, 2, n1, n2, C), BF16),
        compiler_params=_params("parallel", "parallel"),
        name="dft_inner",
    )(gs, gis, a5, kspec)


def _dft_tables(n1, n2):
    n = n1 * n2
    h = n1 // 2
    a = np.arange(n1)
    ang1 = -2.0 * np.pi * ((a[:, None] * a[None, :]) % n1) / n1
    f1r, f1i = np.cos(ang1), np.sin(ang1)
    fwd_sig = np.block([[f1r[:, :h], -f1i[:, :h]], [f1i[:, :h], f1r[:, :h]]])
    fwd_flt = np.concatenate([f1r, f1i], axis=0)
    inv = np.block([[f1r[:h], f1i[:h]], [-f1i[:h], f1r[:h]]])
    b = np.arange(n2)
    ang_t = -2.0 * np.pi * ((np.arange(n1)[:, None] * b[None, :]) % n) / n
    ang_f = -2.0 * np.pi * ((b[:, None] * b[None, :]) % n2) / n2
    twr, twi = jnp.asarray(np.cos(ang_t), F32)[:, None, :], jnp.asarray(np.sin(ang_t), F32)[:, None, :]
    f2r, f2i = jnp.asarray(np.cos(ang_f), F32)[None], jnp.asarray(np.sin(ang_f), F32)[None]
    gr = twr * f2r - twi * f2i
    gi = twr * f2i + twi * f2r
    gs = jnp.concatenate([jnp.concatenate([gr, -gi], axis=2), jnp.concatenate([gi, gr], axis=2)], axis=1)
    gis = jnp.swapaxes(gs, 1, 2) * (1.0 / n)
    cvt = lambda m: jnp.asarray(m, dtype=F32).astype(BF16)
    return cvt(fwd_sig), cvt(fwd_flt), cvt(inv), gs.astype(BF16), gis.astype(BF16)


def _filter_embedding(L):
    t = jnp.linspace(0.0, 1.0, L, dtype=F32)
    w = 2.0 * math.pi * jnp.arange(L, dtype=F32) / L
    bands = jnp.linspace(1e-4, HY_BANDS - 1, HY_BANDS, dtype=F32)
    fw = w[:, None] * bands[None, :]
    emb = jnp.concatenate([t[:, None], jnp.cos(fw), -jnp.sin(fw)], axis=-1)
    valid = jnp.ones((L, 1), F32)
    emb = jnp.concatenate([emb, valid, jnp.zeros((L, EMB_PAD - HY_EMB - 1), F32)], axis=-1)
    back = jnp.concatenate([jnp.zeros((1, EMB_PAD), F32), emb[:0:-1]], axis=0)
    return jnp.concatenate([emb, back], axis=0)


def hyena_branch(z3, lp, hy_w, off_hy):
    B, L, _ = z3.shape
    n = 2 * L
    n2 = DFT_N2
    n1 = n // n2
    P = B // 2
    vin, x0 = hyena_pre(z3, lp['hy_conv_w'], lp['hy_conv_b'], hy_w, off_hy)
    fwd_sig, fwd_flt, inv, gs, gis = _dft_tables(n1, n2)

    hidden = lp['hf_w2'].shape[0]
    w1p = jnp.concatenate([lp['hf_w1'], jnp.zeros((EMB_PAD - HY_EMB, hidden), F32)], axis=0)
    kf, ss = hyena_filter(_filter_embedding(L), w1p, lp['hf_b1'], lp['hf_w2'], lp['hf_b2'], lp['hf_w3'],
                          lp['hf_b3'], lp['hf_freq'], lp['hy_decay'], hy_w)
    kscale = lax.rsqrt(ss + EPS)
    ka = left_matmul(fwd_flt, kf.reshape(1, n1, n2 * hy_w))
    kspec = filter_spectrum(gs, ka.reshape(1, 2, n1, n2, hy_w), kscale)

    a = left_matmul(fwd_sig, vin.reshape(P, n1, n2 * hy_w))
    pm = dft_inner(gs, gis, a.reshape(P, 2, n1, n2, hy_w), kspec)
    pstack = pm.reshape(P, 2 * n1, n2 * hy_w)
    bias_row = jnp.tile(lp['hy_bias'].reshape(1, hy_w), (1, n2))
    out = left_matmul(inv, pstack, post=(vin.reshape(P, n1, n2 * hy_w), x0.reshape(P, n1, n2 * hy_w), bias_row))
    return out.reshape(B, L, hy_w)


def _merge_body(a_ref, h_ref, m_ref, wa_ref, wh_ref, wm_ref, zg0_ref, zg1_ref, zg2_ref,
                bg0_ref, bg1_ref, bg2_ref, o_ref):
    def term(x_ref, w_ref, zg_ref, bg_ref):
        gate = jax.nn.sigmoid(zg_ref[...].astype(F32) + bg_ref[...])
        return gate * jnp.dot(x_ref[...], w_ref[...], preferred_element_type=F32)

    acc = term(a_ref, wa_ref, zg0_ref, bg0_ref)
    acc = acc + term(h_ref, wh_ref, zg1_ref, bg1_ref)
    acc = acc + term(m_ref, wm_ref, zg2_ref, bg2_ref)
    o_ref[...] = acc.astype(o_ref.dtype)


def gated_merge(a_out, h_out, m_out, wa, wh, wm, z2, b_gate, off_gate):
    T = a_out.shape[0]
    D = wa.shape[1]
    tm = min(ROW_TILE, T)
    tn = COL_TILE
    nj = D // tn
    gb = off_gate // tn

    def xspec(width):
        return pl.BlockSpec((tm, width), lambda i, j: (i, 0))

    def wspec(width):
        return pl.BlockSpec((width, tn), lambda i, j: (0, j))

    def zgspec(b):
        return pl.BlockSpec((tm, tn), lambda i, j: (i, gb + b * nj + j))

    def bgspec(b):
        return pl.BlockSpec((1, tn), lambda i, j: (0, b * nj + j))

    bg = b_gate.reshape(1, -1)
    return pl.pallas_call(
        _merge_body,
        grid=(T // tm, nj),
        in_specs=[xspec(a_out.shape[1]), xspec(h_out.shape[1]), xspec(m_out.shape[1]),
                  wspec(wa.shape[0]), wspec(wh.shape[0]), wspec(wm.shape[0]),
                  zgspec(0), zgspec(1), zgspec(2), bgspec(0), bgspec(1), bgspec(2)],
        out_specs=pl.BlockSpec((tm, tn), lambda i, j: (i, j)),
        out_shape=jax.ShapeDtypeStruct((T, D), BF16),
        compiler_params=_params("parallel", "parallel"),
        name="gated_merge",
    )(a_out, h_out, m_out, wa, wh, wm, z2, z2, z2, bg, bg, bg)


def _slab_store(ref, x):
    rows, width = x.shape
    n = width // LANES
    for s in range(n):
        ref[pl.ds(s, rows, stride=n), :] = x[:, s * LANES:(s + 1) * LANES]


def _slab_load(ref, n):
    rows = ref.shape[0] // n
    return jnp.concatenate([ref[pl.ds(s, rows, stride=n), :] for s in range(n)], axis=-1)


ROUTE_E, ROUTE_W, ROUTE_RANK = 0, 2, 4


def _route_tile(lg, carry):
    tm = lg.shape[0]
    lanef = lax.broadcasted_iota(jnp.int32, lg.shape, 1).astype(F32)
    row_max = lambda x: jnp.max(x, axis=-1, keepdims=True)
    first_at = lambda x, v: jnp.min(jnp.where(x == v, lanef, float(LANES)), axis=-1, keepdims=True)

    gl = jnp.where(lanef < N_GROUPS, lg, MASK_VALUE)
    gmax = row_max(gl)
    g_p = 1.0 / jnp.sum(jnp.where(lanef < N_GROUPS, jnp.exp(gl - gmax), 0.0), axis=-1, keepdims=True)
    lo = N_GROUPS + first_at(gl, gmax) * EXPERTS_PER_GROUP
    el = jnp.where((lanef >= lo) & (lanef < lo + EXPERTS_PER_GROUP), lg, MASK_VALUE)
    v1 = row_max(el)
    i1 = first_at(el, v1)
    el2 = jnp.where(lanef == i1, MASK_VALUE, el)
    v2 = row_max(el2)
    i2 = first_at(el2, v2)
    t = jnp.exp(v2 - v1)
    w1 = g_p / (1.0 + t)
    w2 = w1 * t
    e1, e2 = i1 - N_GROUPS, i2 - N_GROUPS

    oh1 = jnp.where(lanef == e1, 1.0, 0.0)
    oh2 = jnp.where(lanef == e2, 1.0, 0.0)
    oh = oh1 + oh2
    r = lax.broadcasted_iota(jnp.int32, (tm, tm), 0)
    c = lax.broadcasted_iota(jnp.int32, (tm, tm), 1)
    earlier = jnp.where(c < r, 1.0, 0.0).astype(BF16)
    before = jnp.dot(earlier, oh.astype(BF16), preferred_element_type=F32) + carry
    rank1 = jnp.sum(oh1 * before, axis=-1, keepdims=True)
    rank2 = jnp.sum(oh2 * before, axis=-1, keepdims=True)

    rec = jnp.zeros(lg.shape, F32)
    for lane_id, val in ((ROUTE_E, e1), (ROUTE_E + 1, e2), (ROUTE_W, w1), (ROUTE_W + 1, w2),
                         (ROUTE_RANK, rank1), (ROUTE_RANK + 1, rank2)):
        rec = jnp.where(lanef == lane_id, val, rec)
    return rec, carry + jnp.sum(oh, axis=0, keepdims=True)


def _pack_bf16_pairs(x):
    w = x.shape[1] // 2
    bits = pltpu.bitcast(x.astype(F32), jnp.uint32)
    return (bits[:, :w] & jnp.uint32(0xFFFF0000)) | (bits[:, w:] >> jnp.uint32(16))


def _unpack_bf16_pairs(words):
    hi = pltpu.bitcast(words & jnp.uint32(0xFFFF0000), F32)
    lo = pltpu.bitcast(words << jnp.uint32(16), F32)
    return jnp.concatenate([hi, lo], axis=-1).astype(BF16)


def _out_router_body(mg_ref, x_ref, wo_ref, g2_ref, w2_ref, br_ref, x1_ref, h2_ref, rec_ref, cnt_ref):
    @pl.when(pl.program_id(0) == 0)
    def _():
        cnt_ref[...] = jnp.zeros(cnt_ref.shape, F32)

    x1 = x_ref[...] + jnp.dot(mg_ref[...], wo_ref[...], preferred_element_type=F32)
    x1_ref[...] = x1
    h = _rms(x1, g2_ref[...])
    h_hi = h.astype(BF16)
    _slab_store(h2_ref, _pack_bf16_pairs(h_hi))
    h_lo = (h - h_hi.astype(F32)).astype(BF16)
    pa = jnp.dot(h_hi, w2_ref[...], preferred_element_type=F32)
    pb = jnp.dot(h_lo, w2_ref[...], preferred_element_type=F32)
    lg = (pa[:, :ROUTER_PAD] + pb[:, :ROUTER_PAD]) + (pa[:, ROUTER_PAD:] + pb[:, ROUTER_PAD:])
    rec, cnt = _route_tile(lg + br_ref[...], cnt_ref[...])
    rec_ref[...] = rec
    cnt_ref[...] = cnt


def out_router(merged, x2, w_out, g2, wr2, br):
    T, D = x2.shape
    tm = min(PREP_TILE, T)
    const = lambda i: (0, 0)
    row = lambda i: (i, 0)
    return pl.pallas_call(
        _out_router_body,
        grid=(T // tm,),
        in_specs=[pl.BlockSpec((tm, D), row), pl.BlockSpec((tm, D), row),
                  pl.BlockSpec((D, D), const), pl.BlockSpec((1, D), const),
                  pl.BlockSpec((D, 2 * ROUTER_PAD), const), pl.BlockSpec((1, ROUTER_PAD), const)],
        out_specs=[pl.BlockSpec((tm, D), row), pl.BlockSpec((tm * (D // (2 * LANES)), LANES), row),
                   pl.BlockSpec((tm, ROUTER_PAD), row), pl.BlockSpec((1, ROUTER_PAD), const)],
        out_shape=[jax.ShapeDtypeStruct((T, D), F32),
                   jax.ShapeDtypeStruct((T * (D // (2 * LANES)), LANES), jnp.uint32),
                   jax.ShapeDtypeStruct((T, ROUTER_PAD), F32), jax.ShapeDtypeStruct((1, ROUTER_PAD), F32)],
        compiler_params=_params("arbitrary"),
        name="out_router",
    )(merged, x2, w_out, g2.reshape(1, D), wr2, br)


def _row_copy(src_ref, src_row, dst_ref, dst_row, n, sem):
    src = src_ref.at[pl.ds(pl.multiple_of(src_row * n, n), n), :]
    dst = dst_ref.at[pl.ds(pl.multiple_of(dst_row * n, n), n), :]
    return pltpu.make_async_copy(src, dst, sem)


def _for_rows(rows, fn):
    def group(gi, carry):
        for u in range(DMA_UNROLL):
            fn(gi * DMA_UNROLL + u)
        return carry

    lax.fori_loop(0, rows // DMA_UNROLL, group, 0)


def _dispatch_body(dest_ref, h_ref, xs_in_ref, xs_ref, sem, *, n):
    del xs_in_ref
    rows = h_ref.shape[0] // n
    base = pl.program_id(0) * rows

    def copies(r):
        a = TOP_K * (base + r)
        return [_row_copy(h_ref, r, xs_ref, dest_ref[a + j], n, sem) for j in range(TOP_K)]

    def issue(r):
        for c in copies(r):
            c.start()

    def drain(r):
        for c in copies(r):
            c.wait()

    _for_rows(rows, issue)
    _for_rows(rows, drain)


def dispatch_rows(dest_flat, h_slab, n, P):
    T = h_slab.shape[0] // n
    rows = min(GATHER_ROWS, T)
    xs0 = jnp.zeros((P * n, LANES), h_slab.dtype)
    return pl.pallas_call(
        functools.partial(_dispatch_body, n=n),
        grid_spec=pltpu.PrefetchScalarGridSpec(
            num_scalar_prefetch=1,
            grid=(T // rows,),
            in_specs=[pl.BlockSpec((rows * n, LANES), lambda i, dest: (i, 0)),
                      pl.BlockSpec(memory_space=pl.ANY)],
            out_specs=pl.BlockSpec(memory_space=pl.ANY),
            scratch_shapes=[pltpu.SemaphoreType.DMA(())]),
        out_shape=jax.ShapeDtypeStruct((P * n, LANES), h_slab.dtype),
        input_output_aliases={2: 0},
        compiler_params=_params("arbitrary"),
        name="dispatch_rows",
    )(dest_flat, h_slab, xs0)


def _moe_body(blk_e_ref, n_used_ref, x_ref, wg_ref, wu_ref, wd_ref, o_ref):
    del blk_e_ref
    i = pl.program_id(0)
    n = wg_ref.shape[1] // LANES

    @pl.when(i < n_used_ref[0])
    def _():
        x = _unpack_bf16_pairs(_slab_load(x_ref, n // 2))
        g = jnp.dot(x, wg_ref[0], preferred_element_type=F32)
        u = jnp.dot(x, wu_ref[0], preferred_element_type=F32)
        a = (g * jax.nn.sigmoid(g) * u).astype(BF16)
        _slab_store(o_ref, jnp.dot(a, wd_ref[0], preferred_element_type=F32))

    @pl.when(i >= n_used_ref[0])
    def _():
        o_ref[...] = jnp.zeros(o_ref.shape, o_ref.dtype)


def moe_experts(blk_e, n_used, xs_slab, wg, wu, wd):
    _, D, De = wg.shape
    n = D // LANES
    P = xs_slab.shape[0] // (n // 2)
    tm = MOE_TM
    return pl.pallas_call(
        _moe_body,
        grid_spec=pltpu.PrefetchScalarGridSpec(
            num_scalar_prefetch=2,
            grid=(P // tm,),
            in_specs=[pl.BlockSpec((tm * n // 2, LANES), lambda i, be, nu: (i, 0)),
                      pl.BlockSpec((1, D, De), lambda i, be, nu: (be[i], 0, 0)),
                      pl.BlockSpec((1, D, De), lambda i, be, nu: (be[i], 0, 0)),
                      pl.BlockSpec((1, De, D), lambda i, be, nu: (be[i], 0, 0))],
            out_specs=pl.BlockSpec((tm * n, LANES), lambda i, be, nu: (i, 0))),
        out_shape=jax.ShapeDtypeStruct((P * n, LANES), F32),
        compiler_params=_params("arbitrary"),
        name="moe_experts",
    )(blk_e, n_used, xs_slab, wg, wu, wd)


def _combine_body(dest_ref, x1_ref, rec_ref, yb_ref, o_ref, *scratch):
    bufs, sem = scratch[:TOP_K], scratch[TOP_K]
    tg, D = x1_ref.shape
    n = D // LANES
    base = pl.program_id(0) * tg

    def copies(r):
        a = TOP_K * (base + r)
        return [_row_copy(yb_ref, dest_ref[a + j], bufs[j], r, n, sem) for j in range(TOP_K)]

    def issue(r):
        for c in copies(r):
            c.start()

    def drain(r):
        for c in copies(r):
            c.wait()

    _for_rows(tg, issue)
    _for_rows(tg, drain)
    w = [jnp.broadcast_to(rec_ref[:, ROUTE_W + j:ROUTE_W + j + 1], (tg, LANES)) for j in range(TOP_K)]
    for s in range(n):
        cols = slice(s * LANES, (s + 1) * LANES)
        y = x1_ref[:, cols]
        for j in range(TOP_K):
            y = y + w[j] * bufs[j][pl.ds(s, tg, stride=n), :]
        o_ref[:, cols] = y


def moe_combine(dest_flat, x1, rec, yb_slab):
    T, D = x1.shape
    tg = min(GATHER_ROWS, T)
    n = D // LANES
    return pl.pallas_call(
        _combine_body,
        grid_spec=pltpu.PrefetchScalarGridSpec(
            num_scalar_prefetch=1,
            grid=(T // tg,),
            in_specs=[pl.BlockSpec((tg, D), lambda i, dest: (i, 0)),
                      pl.BlockSpec((tg, ROUTER_PAD), lambda i, dest: (i, 0)),
                      pl.BlockSpec(memory_space=pl.ANY)],
            out_specs=pl.BlockSpec((tg, D), lambda i, dest: (i, 0)),
            scratch_shapes=[pltpu.VMEM((tg * n, LANES), F32) for _ in range(TOP_K)]
            + [pltpu.SemaphoreType.DMA(())]),
        out_shape=jax.ShapeDtypeStruct((T, D), F32),
        compiler_params=_params("arbitrary"),
        name="moe_combine",
    )(dest_flat, x1, rec, yb_slab)


def _block_layout(rec, counts_row):
    T = rec.shape[0]
    A = T * TOP_K
    P = (A + N_EXPERTS * (MOE_TM - 1) + MOE_TM - 1) // MOE_TM * MOE_TM
    n_blk = P // MOE_TM
    counts = counts_row[0, :N_EXPERTS].astype(jnp.int32)
    padded = (counts + MOE_TM - 1) // MOE_TM * MOE_TM
    pends = jnp.cumsum(padded)
    pstarts = pends - padded
    e = rec[:, ROUTE_E:ROUTE_E + TOP_K].astype(jnp.int32)
    rank = rec[:, ROUTE_RANK:ROUTE_RANK + TOP_K].astype(jnp.int32)
    one_hot = e[:, :, None] == jnp.arange(N_EXPERTS, dtype=jnp.int32)
    dest = jnp.sum(jnp.where(one_hot, pstarts, 0), axis=-1) + rank
    blk_start = jnp.arange(n_blk, dtype=jnp.int32) * MOE_TM
    blk_e = jnp.minimum(jnp.sum(blk_start[:, None] >= pends[None, :], axis=-1), N_EXPERTS - 1).astype(jnp.int32)
    n_used = (pends[-1:] // MOE_TM).astype(jnp.int32)
    return dest.reshape(A), blk_e, n_used, P


def _rope_tables(L):
    rows = L // GRID_W
    pairs = HEAD_DIM // 4
    row = jnp.repeat(jnp.arange(rows, dtype=F32), GRID_W)
    col = jnp.tile(jnp.arange(GRID_W, dtype=F32), rows)
    inv = jnp.power(ROPE_THETA, -jnp.arange(pairs, dtype=F32) / pairs)
    ang = jnp.concatenate([row[:, None] * inv, col[:, None] * inv], axis=-1)
    cos, sin = jnp.cos(ang), jnp.sin(ang)
    cosf = jnp.repeat(cos, 2, axis=-1)
    sinf = jnp.stack([-sin, sin], axis=-1).reshape(L, HEAD_DIM)
    return cosf, sinf


def _trunk(x, mem, lp, wb):
    B, L, D = x.shape
    T = B * L
    att_w = lp['w_br_attn'].shape[0]
    hy_w = lp['w_br_hyena'].shape[0]
    mem_w = lp['w_br_mem'].shape[0]
    kv_w = (lp['w_in'].shape[1] - att_w - 3 * hy_w - mem_w - 3 * D) // 2
    off_hy = att_w + 2 * kv_w
    off_mq = off_hy + 3 * hy_w
    off_gate = off_mq + mem_w
    x2 = x.reshape(T, D)

    z2 = normed_matmul(x2, lp['norm1_g'], wb['w_in'], BF16)
    z3 = z2.reshape(B, L, -1)
    cosf, sinf = _rope_tables(L)
    q, k2, v2, stats = qkv_prep(z3, cosf, sinf, lp['q_norm_g'], lp['k_norm_g'], att_w, kv_w)
    a_out = flash_attention(q, k2, v2, stats)
    h_out = hyena_branch(z3, lp, hy_w, off_hy)
    M = mem.shape[1]
    kv = normed_matmul(mem.reshape(B * M, D), lp['mem_norm_g'], wb['w_mem_kv'], F32).reshape(B, M, -1)
    m_out = mem_attention(z3, kv, lp['mq_norm_g'], lp['mk_norm_g'], mem_w, off_mq)
    merged = gated_merge(a_out.reshape(T, att_w), h_out.reshape(T, hy_w), m_out.reshape(T, mem_w),
                         wb['w_br_attn'], wb['w_br_hyena'], wb['w_br_mem'], z2, lp['b_gate'], off_gate)
    x1, h2, rec, counts = out_router(merged, x2, wb['w_out'], lp['norm2_g'], wb['wr2'], wb['br'])

    dest_flat, blk_e, n_used, P = _block_layout(rec, counts)
    xs = dispatch_rows(dest_flat, h2, D // (2 * LANES), P)
    yb = moe_experts(blk_e, n_used, xs, wb['w_gate_e'], wb['w_up_e'], wb['w_down_e'])
    y = moe_combine(dest_flat, x1, rec, yb)
    return y.reshape(B, L, D)


def kernel(x_prompt, x_sample, mem_prompt, mem_sample, norm1_g, w_in, b_gate, q_norm_g, k_norm_g, hy_conv_w, hy_conv_b, hf_w1, hf_b1, hf_w2, hf_b2, hf_w3, hf_b3, hf_freq, hy_decay, hy_bias, mem_norm_g, w_mem_kv, mq_norm_g, mk_norm_g, w_br_attn, w_br_hyena, w_br_mem, w_out, norm2_g, w_router_group, b_router_group, w_router_expert, b_router_expert, w_gate_e, w_up_e, w_down_e):
    params = dict(norm1_g=norm1_g, w_in=w_in, b_gate=b_gate, q_norm_g=q_norm_g, k_norm_g=k_norm_g,
                  hy_conv_w=hy_conv_w, hy_conv_b=hy_conv_b, hf_w1=hf_w1, hf_b1=hf_b1, hf_w2=hf_w2,
                  hf_b2=hf_b2, hf_w3=hf_w3, hf_b3=hf_b3, hf_freq=hf_freq, hy_decay=hy_decay,
                  hy_bias=hy_bias, mem_norm_g=mem_norm_g, w_mem_kv=w_mem_kv, mq_norm_g=mq_norm_g,
                  mk_norm_g=mk_norm_g, w_br_attn=w_br_attn, w_br_hyena=w_br_hyena, w_br_mem=w_br_mem,
                  w_out=w_out, norm2_g=norm2_g, w_router_group=w_router_group,
                  b_router_group=b_router_group, w_router_expert=w_router_expert,
                  b_router_expert=b_router_expert, w_gate_e=w_gate_e, w_up_e=w_up_e, w_down_e=w_down_e)
    depth = w_in.shape[0]
    xp, xs = x_prompt, x_sample
    for d in range(depth):
        lp = {name: arr[d] for name, arr in params.items()}
        D = lp['w_in'].shape[0]
        wb = {name: lp[name].astype(BF16) for name in
              ('w_in', 'w_mem_kv', 'w_br_attn', 'w_br_hyena', 'w_br_mem', 'w_out',
               'w_gate_e', 'w_up_e', 'w_down_e')}
        wr = jnp.concatenate([lp['w_router_group'], lp['w_router_expert'],
                              jnp.zeros((D, ROUTER_PAD - N_GROUPS - N_EXPERTS), F32)], axis=1)
        wr_hi = wr.astype(BF16)
        wb['wr2'] = jnp.concatenate([wr_hi, (wr - wr_hi.astype(F32)).astype(BF16)], axis=1)
        wb['br'] = jnp.concatenate([lp['b_router_group'], lp['b_router_expert'],
                                    jnp.zeros((ROUTER_PAD - N_GROUPS - N_EXPERTS,), F32)]).reshape(1, ROUTER_PAD)
        xp = _trunk(xp, mem_prompt, lp, wb)
        xs = _trunk(xs, mem_sample, lp, wb)
    return (xp, xs)
```

```python
import functools
import math

import jax
import jax.numpy as jnp
import numpy as np
from jax import lax
from jax.experimental import pallas as pl
from jax.experimental.pallas import tpu as pltpu

F32 = jnp.float32
BF16 = jnp.bfloat16

HEAD_DIM = 128
KV_GROUP = 4
GRID_W = 64
ROPE_THETA = 10000.0
HY_BANDS = 8
HY_EMB = 1 + 2 * HY_BANDS
HY_SHORT = 3
MEM_HEADS = 4
N_GROUPS = 4
EXPERTS_PER_GROUP = 8
N_EXPERTS = N_GROUPS * EXPERTS_PER_GROUP
TOP_K = 2
EPS = 1e-6
MASK_VALUE = -1e30
EXP2_SAFE_RANGE = 120.0
LANES = 128
MXU_DIM = 256
VMEM_LIMIT_BYTES = 52 * 1024 * 1024

ROW_TILE = 1024
COL_TILE = 512
PREP_TILE = 512
ATT_TQ = 1024
ATT_TK = 2048
DFT_N2 = 256
DFT_COLS = 2048
MOE_TM = 256
GATHER_ROWS = 512
DMA_UNROLL = 8
EMB_PAD = 32
ROUTER_PAD = 128


def _params(*sem):
    return pltpu.CompilerParams(dimension_semantics=sem, vmem_limit_bytes=VMEM_LIMIT_BYTES)


def _rms(x, g):
    ms = jnp.mean(x * x, axis=-1, keepdims=True)
    return x * lax.rsqrt(ms + EPS) * g


def _normed_matmul_body(x_ref, g_ref, w_ref, o_ref, xn_ref):
    @pl.when(pl.program_id(1) == 0)
    def _():
        xn_ref[...] = _rms(x_ref[...], g_ref[...]).astype(BF16)

    o_ref[...] = jnp.dot(xn_ref[...], w_ref[...], preferred_element_type=F32).astype(o_ref.dtype)


def normed_matmul(x, g, w, out_dtype):
    T, D = x.shape
    N = w.shape[1]
    tm = min(ROW_TILE, T)
    tn = COL_TILE
    return pl.pallas_call(
        _normed_matmul_body,
        grid=(T // tm, N // tn),
        in_specs=[pl.BlockSpec((tm, D), lambda i, j: (i, 0)),
                  pl.BlockSpec((1, D), lambda i, j: (0, 0)),
                  pl.BlockSpec((D, tn), lambda i, j: (0, j))],
        out_specs=pl.BlockSpec((tm, tn), lambda i, j: (i, j)),
        out_shape=jax.ShapeDtypeStruct((T, N), out_dtype),
        scratch_shapes=[pltpu.VMEM((tm, D), BF16)],
        compiler_params=_params("parallel", "arbitrary"),
        name="normed_matmul",
    )(x, g.reshape(1, D), w)


def _qkv_prep_body(zq_ref, zk_ref, zv_ref, cos_ref, sin_ref, gq_ref, gk_ref, q_ref, k_ref, v_ref, st_ref, *, scale):
    cosf = cos_ref[...]
    sinf = sin_ref[...]
    tl = cosf.shape[0]
    lane = lax.broadcasted_iota(jnp.int32, cosf.shape, 1)
    even = (lane % 2) == 0
    one_hot0 = jnp.where(lane == 0, 1.0, 0.0).astype(BF16)

    def prep(x, g):
        xn = _rms(x, g)
        partner = jnp.where(even, pltpu.roll(xn, LANES - 1, 1), pltpu.roll(xn, 1, 1))
        return xn * cosf + partner * sinf

    def max_norm2(xb):
        xf = xb.astype(F32)
        return jnp.max(jnp.sum(xf * xf, axis=-1, keepdims=True), axis=0, keepdims=True)

    qmax = jnp.zeros((1, 1), F32)
    for h in range(zq_ref.shape[-1] // HEAD_DIM):
        sl = slice(h * HEAD_DIM, (h + 1) * HEAD_DIM)
        qb = (prep(zq_ref[0, :, sl].astype(F32), gq_ref[...]) * scale).astype(BF16)
        q_ref[0, :, sl] = qb
        qmax = jnp.maximum(qmax, max_norm2(qb))
    stats = [qmax]
    for h in range(zk_ref.shape[-1] // HEAD_DIM):
        sl = slice(h * HEAD_DIM, (h + 1) * HEAD_DIM)
        kb = prep(zk_ref[0, :, sl].astype(F32), gk_ref[...]).astype(BF16)
        k_ref[0, :, 2 * h * HEAD_DIM:(2 * h + 1) * HEAD_DIM] = kb
        k_ref[0, :, (2 * h + 1) * HEAD_DIM:(2 * h + 2) * HEAD_DIM] = one_hot0
        v_ref[0, :, 2 * h * HEAD_DIM:(2 * h + 1) * HEAD_DIM] = zv_ref[0, :, sl].astype(BF16)
        v_ref[0, :, (2 * h + 1) * HEAD_DIM:(2 * h + 2) * HEAD_DIM] = one_hot0
        stats.append(max_norm2(kb))
    row = lax.broadcasted_iota(jnp.int32, (8, LANES), 0)
    st = jnp.zeros((8, LANES), F32)
    for r, val in enumerate(stats):
        st = jnp.where(row == r, val, st)
    st_ref[0, 0] = st


def qkv_prep(z3, cosf, sinf, gq, gk, att_w, kv_w):
    B, L, _ = z3.shape
    tl = min(PREP_TILE, L)
    off_k = att_w // kv_w
    body = functools.partial(_qkv_prep_body, scale=HEAD_DIM ** -0.5 * math.log2(math.e))
    return pl.pallas_call(
        body,
        grid=(B, L // tl),
        in_specs=[pl.BlockSpec((1, tl, att_w), lambda b, i: (b, i, 0)),
                  pl.BlockSpec((1, tl, kv_w), lambda b, i: (b, i, off_k)),
                  pl.BlockSpec((1, tl, kv_w), lambda b, i: (b, i, off_k + 1)),
                  pl.BlockSpec((tl, HEAD_DIM), lambda b, i: (i, 0)),
                  pl.BlockSpec((tl, HEAD_DIM), lambda b, i: (i, 0)),
                  pl.BlockSpec((1, HEAD_DIM), lambda b, i: (0, 0)),
                  pl.BlockSpec((1, HEAD_DIM), lambda b, i: (0, 0))],
        out_specs=[pl.BlockSpec((1, tl, att_w), lambda b, i: (b, i, 0)),
                   pl.BlockSpec((1, tl, 2 * kv_w), lambda b, i: (b, i, 0)),
                   pl.BlockSpec((1, tl, 2 * kv_w), lambda b, i: (b, i, 0)),
                   pl.BlockSpec((1, 1, 8, LANES), lambda b, i: (b, i, 0, 0))],
        out_shape=[jax.ShapeDtypeStruct((B, L, att_w), BF16),
                   jax.ShapeDtypeStruct((B, L, 2 * kv_w), BF16),
                   jax.ShapeDtypeStruct((B, L, 2 * kv_w), BF16),
                   jax.ShapeDtypeStruct((B, L // tl, 8, LANES), F32)],
        compiler_params=_params("parallel", "parallel"),
        name="qkv_prep",
    )(z3, z3, z3, cosf, sinf, gq.reshape(1, HEAD_DIM), gk.reshape(1, HEAD_DIM))


def _flash_body(fixed_ref, kmax_ref, q_ref, k_ref, v_ref, o_ref, qx_ref, acc_ref, m_ref, *, n_k, n_kv):
    b, g, kk = pl.program_id(0), pl.program_id(1), pl.program_id(3)
    nt_dims = (((1,), (1,)), ((), ()))
    tq = qx_ref.shape[1]

    @pl.when(kk == 0)
    def _():
        acc_ref[...] = jnp.zeros(acc_ref.shape, F32)
        m_ref[...] = jnp.full(m_ref.shape, MASK_VALUE, F32)
        kmax = kmax_ref[b * n_kv + g]
        lane0 = lax.broadcasted_iota(jnp.int32, (tq, HEAD_DIM), 1) == 0
        for h in range(KV_GROUP):
            q = q_ref[0, :, h * HEAD_DIM:(h + 1) * HEAD_DIM]
            qf = q.astype(F32)
            bound = jnp.sqrt(jnp.sum(qf * qf, axis=-1, keepdims=True)) * kmax
            qx_ref[h, :, :HEAD_DIM] = q
            qx_ref[h, :, HEAD_DIM:] = jnp.where(lane0, -bound, 0.0).astype(BF16)

    k = k_ref[0]
    v = v_ref[0]

    @pl.when(fixed_ref[0] == 1)
    def _():
        for h in range(KV_GROUP):
            s = lax.dot_general(qx_ref[h], k, nt_dims, preferred_element_type=F32)
            acc_ref[h] += jnp.dot(jnp.exp2(s).astype(BF16), v, preferred_element_type=F32)

    @pl.when(fixed_ref[0] == 0)
    def _():
        for h in range(KV_GROUP):
            s = lax.dot_general(qx_ref[h, :, :HEAD_DIM], k[:, :HEAD_DIM], nt_dims, preferred_element_type=F32)
            m_prev = m_ref[h]
            m_new = jnp.maximum(m_prev, jnp.max(s, axis=-1, keepdims=True))
            p = jnp.exp2(s - m_new).astype(BF16)
            acc_ref[h] = jnp.exp2(m_prev - m_new) * acc_ref[h] + jnp.dot(p, v, preferred_element_type=F32)
            m_ref[h] = m_new

    @pl.when(kk == n_k - 1)
    def _():
        for h in range(KV_GROUP):
            a = acc_ref[h]
            o_ref[0, :, h * HEAD_DIM:(h + 1) * HEAD_DIM] = (
                a[:, :HEAD_DIM] / a[:, HEAD_DIM:HEAD_DIM + 1]).astype(o_ref.dtype)


def flash_attention(q, k2, v2, stats):
    B, L, att_w = q.shape
    n_kv = k2.shape[-1] // (2 * HEAD_DIM)
    tq = min(ATT_TQ, L)
    tk = min(ATT_TK, L)
    gw = KV_GROUP * HEAD_DIM
    smax = jnp.sqrt(jnp.max(stats[:, :, :1 + n_kv, 0], axis=1))
    kmax = smax[:, 1:].reshape(B * n_kv)
    fixed = (2.0 * jnp.max(smax[:, :1] * smax[:, 1:]) < EXP2_SAFE_RANGE).astype(jnp.int32).reshape(1)
    return pl.pallas_call(
        functools.partial(_flash_body, n_k=L // tk, n_kv=n_kv),
        grid_spec=pltpu.PrefetchScalarGridSpec(
            num_scalar_prefetch=2,
            grid=(B, n_kv, L // tq, L // tk),
            in_specs=[pl.BlockSpec((1, tq, gw), lambda b, g, i, kk, *_: (b, i, g)),
                      pl.BlockSpec((1, tk, 2 * HEAD_DIM), lambda b, g, i, kk, *_: (b, kk, g)),
                      pl.BlockSpec((1, tk, 2 * HEAD_DIM), lambda b, g, i, kk, *_: (b, kk, g))],
            out_specs=pl.BlockSpec((1, tq, gw), lambda b, g, i, kk, *_: (b, i, g)),
            scratch_shapes=[pltpu.VMEM((KV_GROUP, tq, 2 * HEAD_DIM), BF16),
                            pltpu.VMEM((KV_GROUP, tq, 2 * HEAD_DIM), F32),
                            pltpu.VMEM((KV_GROUP, tq, 1), F32)]),
        out_shape=jax.ShapeDtypeStruct((B, L, att_w), BF16),
        compiler_params=_params("parallel", "parallel", "parallel", "arbitrary"),
        name="flash_attention",
    )(fixed, kmax, q, k2, v2)


def _mem_attn_body(zm_ref, kv_ref, gq_ref, gk_ref, o_ref, *, scale):
    mem_w = zm_ref.shape[-1]
    for h in range(mem_w // HEAD_DIM):
        sl = slice(h * HEAD_DIM, (h + 1) * HEAD_DIM)
        q = (_rms(zm_ref[0, :, sl].astype(F32), gq_ref[...]) * scale).astype(BF16)
        k = _rms(kv_ref[0, :, sl], gk_ref[...]).astype(BF16)
        v = kv_ref[0, :, mem_w + h * HEAD_DIM:mem_w + (h + 1) * HEAD_DIM].astype(BF16)
        s = lax.dot_general(q, k, (((1,), (1,)), ((), ())), preferred_element_type=F32)
        p = jnp.exp(s - jnp.max(s, axis=-1, keepdims=True))
        o = jnp.dot(p.astype(BF16), v, preferred_element_type=F32)
        o_ref[0, :, sl] = (o / jnp.sum(p, axis=-1, keepdims=True)).astype(o_ref.dtype)


def mem_attention(z3, kv, gq, gk, mem_w, off_mq):
    B, L, _ = z3.shape
    M = kv.shape[1]
    tl = min(PREP_TILE, L)
    body = functools.partial(_mem_attn_body, scale=HEAD_DIM ** -0.5)
    return pl.pallas_call(
        body,
        grid=(B, L // tl),
        in_specs=[pl.BlockSpec((1, tl, mem_w), lambda b, i: (b, i, off_mq // mem_w)),
                  pl.BlockSpec((1, M, 2 * mem_w), lambda b, i: (b, 0, 0)),
                  pl.BlockSpec((1, HEAD_DIM), lambda b, i: (0, 0)),
                  pl.BlockSpec((1, HEAD_DIM), lambda b, i: (0, 0))],
        out_specs=pl.BlockSpec((1, tl, mem_w), lambda b, i: (b, i, 0)),
        out_shape=jax.ShapeDtypeStruct((B, L, mem_w), BF16),
        compiler_params=_params("parallel", "parallel"),
        name="mem_attention",
    )(z3, kv, gq.reshape(1, HEAD_DIM), gk.reshape(1, HEAD_DIM))


def _hyena_pre_body(z_ref, prev_ref, next_ref, w_ref, b_ref, vin_ref, x0_ref, *, n_tiles):
    i = pl.program_id(1)
    x = z_ref[0].astype(F32)
    tl, width = x.shape
    hy_w = width // 3
    row = lax.broadcasted_iota(jnp.int32, x.shape, 0)
    halo = prev_ref.shape[1]
    prev_row = jnp.where(i > 0, prev_ref[0, halo - 1:halo, :].astype(F32), 0.0)
    next_row = jnp.where(i < n_tiles - 1, next_ref[0, 0:1, :].astype(F32), 0.0)
    x_prev = jnp.where(row == 0, prev_row, pltpu.roll(x, 1, 0))
    x_next = jnp.where(row == tl - 1, next_row, pltpu.roll(x, tl - 1, 0))
    u = w_ref[0:1, :] * x_prev + w_ref[1:2, :] * x + w_ref[2:3, :] * x_next + b_ref[...]
    vin_ref[0] = u[:, 2 * hy_w:] * u[:, hy_w:2 * hy_w]
    x0_ref[0] = u[:, :hy_w]


def hyena_pre(z3, conv_w, conv_b, hy_w, off_hy):
    B, L, _ = z3.shape
    width = 3 * hy_w
    tl = min(PREP_TILE, L)
    cb = off_hy // width
    halo = 16
    r8 = tl // halo
    last8 = L // halo - 1
    return pl.pallas_call(
        functools.partial(_hyena_pre_body, n_tiles=L // tl),
        grid=(B, L // tl),
        in_specs=[pl.BlockSpec((1, tl, width), lambda b, i: (b, i, cb)),
                  pl.BlockSpec((1, halo, width), lambda b, i: (b, jnp.maximum(i * r8 - 1, 0), cb)),
                  pl.BlockSpec((1, halo, width), lambda b, i: (b, jnp.minimum((i + 1) * r8, last8), cb)),
                  pl.BlockSpec((HY_SHORT, width), lambda b, i: (0, 0)),
                  pl.BlockSpec((1, width), lambda b, i: (0, 0))],
        out_specs=[pl.BlockSpec((1, tl, hy_w), lambda b, i: (b, i, 0)),
                   pl.BlockSpec((1, tl, hy_w), lambda b, i: (b, i, 0))],
        out_shape=[jax.ShapeDtypeStruct((B, L, hy_w), F32),
                   jax.ShapeDtypeStruct((B, L, hy_w), F32)],
        compiler_params=_params("parallel", "parallel"),
        name="hyena_pre",
    )(z3, z3, z3, conv_w.reshape(HY_SHORT, width), conv_b.reshape(1, width))


def _hyena_filter_body(e_ref, w1_ref, b1_ref, w2_ref, b2_ref, w3_ref, b3_ref, freq_ref, dec_ref,
                       kf_ref, ss_ref):
    hp = lax.Precision.HIGHEST
    e = e_ref[...]
    t = e[:, 0:1]
    valid = e[:, HY_EMB:HY_EMB + 1]
    freq = freq_ref[...]
    h = jnp.sin(freq * (jnp.dot(e, w1_ref[...], precision=hp, preferred_element_type=F32) + b1_ref[...]))
    h = jnp.sin(freq * (jnp.dot(h, w2_ref[...], precision=hp, preferred_element_type=F32) + b2_ref[...]))
    h = jnp.dot(h, w3_ref[...], precision=hp, preferred_element_type=F32) + b3_ref[...]
    kf = h * jnp.exp(-t * jnp.abs(dec_ref[...])) * valid
    kf_ref[...] = kf

    @pl.when(pl.program_id(0) == 0)
    def _():
        ss_ref[...] = jnp.zeros(ss_ref.shape, F32)

    ss_ref[...] += jnp.sum(kf * kf, axis=0, keepdims=True)


def hyena_filter(emb2, w1p, b1, w2, b2, w3, b3, freq, decay, hy_w):
    n_rows = emb2.shape[0]
    hidden = w2.shape[0]
    tl = min(PREP_TILE, n_rows // 2)
    nblk = n_rows // tl
    half = nblk // 2

    def dirmap(i):
        return (0, jnp.where(i >= half, 1, 0))

    const = lambda i: (0, 0)
    return pl.pallas_call(
        _hyena_filter_body,
        grid=(nblk,),
        in_specs=[pl.BlockSpec((tl, EMB_PAD), lambda i: (i, 0)),
                  pl.BlockSpec((EMB_PAD, hidden), const),
                  pl.BlockSpec((1, hidden), const),
                  pl.BlockSpec((hidden, hidden), const),
                  pl.BlockSpec((1, hidden), const),
                  pl.BlockSpec((hidden, hy_w), dirmap),
                  pl.BlockSpec((1, hy_w), dirmap),
                  pl.BlockSpec((1, hidden), const),
                  pl.BlockSpec((1, hy_w), dirmap)],
        out_specs=[pl.BlockSpec((tl, hy_w), lambda i: (i, 0)),
                   pl.BlockSpec((1, hy_w), const)],
        out_shape=[jax.ShapeDtypeStruct((n_rows, hy_w), F32),
                   jax.ShapeDtypeStruct((1, hy_w), F32)],
        compiler_params=_params("arbitrary"),
        name="hyena_filter",
    )(emb2, w1p, b1.reshape(1, hidden), w2, b2.reshape(1, hidden), w3, b3.reshape(1, -1),
      freq.reshape(1, hidden), decay.reshape(1, -1))


def _left_matmul_body(m_ref, x_ref, o_ref):
    o_ref[0] = jnp.dot(m_ref[...], x_ref[0].astype(BF16), preferred_element_type=F32).astype(o_ref.dtype)


def _left_matmul_post_body(m_ref, x_ref, vin_ref, x0_ref, bias_ref, o_ref):
    y = jnp.dot(m_ref[...], x_ref[0].astype(BF16), preferred_element_type=F32)
    vin = vin_ref[0]
    o_ref[0] = ((y + vin * bias_ref[...]) * x0_ref[0]).astype(o_ref.dtype)


def left_matmul(mat, x, post=None):
    P, K, cols = x.shape
    R = mat.shape[0]
    tc = min(DFT_COLS, cols)
    in_specs = [pl.BlockSpec((R, K), lambda p, c: (0, 0)),
                pl.BlockSpec((1, K, tc), lambda p, c: (p, 0, c))]
    args = [mat, x]
    if post is None:
        body, dtype = _left_matmul_body, BF16
    else:
        vin, x0, bias_row = post
        body, dtype = _left_matmul_post_body, BF16
        in_specs += [pl.BlockSpec((1, R, tc), lambda p, c: (p, 0, c)),
                     pl.BlockSpec((1, R, tc), lambda p, c: (p, 0, c)),
                     pl.BlockSpec((1, tc), lambda p, c: (0, c))]
        args += [vin, x0, bias_row]
    return pl.pallas_call(
        body,
        grid=(P, cols // tc),
        in_specs=in_specs,
        out_specs=pl.BlockSpec((1, R, tc), lambda p, c: (p, 0, c)),
        out_shape=jax.ShapeDtypeStruct((P, R, cols), dtype),
        compiler_params=_params("parallel", "parallel"),
        name="dft_outer",
    )(*args)


def _filter_spectrum_body(g_ref, a_ref, s_ref, k_ref):
    n2 = a_ref.shape[-2]
    a = jnp.concatenate([a_ref[0, 0, 0], a_ref[0, 1, 0]], axis=0).astype(BF16)
    y = jnp.dot(g_ref[0], a, preferred_element_type=F32) * s_ref[...]
    k_ref[0, 0] = y[:n2]
    k_ref[1, 0] = y[n2:]


def filter_spectrum(gs, a5, scale_row):
    _, _, n1, n2, C = a5.shape
    return pl.pallas_call(
        _filter_spectrum_body,
        grid=(n1,),
        in_specs=[pl.BlockSpec((1, 2 * n2, 2 * n2), lambda k: (k, 0, 0)),
                  pl.BlockSpec((1, 2, 1, n2, C), lambda k: (0, 0, k, 0, 0)),
                  pl.BlockSpec((1, C), lambda k: (0, 0))],
        out_specs=pl.BlockSpec((2, 1, n2, C), lambda k: (0, k, 0, 0)),
        out_shape=jax.ShapeDtypeStruct((2, n1, n2, C), F32),
        compiler_params=_params("parallel"),
        name="filter_spectrum",
    )(gs, a5, scale_row)


def _dft_inner_body(g_ref, gi_ref, a_ref, k_ref, p_ref):
    n2 = a_ref.shape[-2]
    a = jnp.concatenate([a_ref[0, 0, 0], a_ref[0, 1, 0]], axis=0).astype(BF16)
    y = jnp.dot(g_ref[0], a, preferred_element_type=F32)
    yr, yi = y[:n2], y[n2:]
    kr, ki = k_ref[0, 0], k_ref[1, 0]
    z = jnp.concatenate([yr * kr - yi * ki, yr * ki + yi * kr], axis=0).astype(BF16)
    p = jnp.dot(gi_ref[0], z, preferred_element_type=F32)
    p_ref[0, 0, 0] = p[:n2].astype(p_ref.dtype)
    p_ref[0, 1, 0] = p[n2:].astype(p_ref.dtype)


def dft_inner(gs, gis, a5, kspec):
    P, _, n1, n2, C = a5.shape
    gspec = pl.BlockSpec((1, 2 * n2, 2 * n2), lambda k, p: (k, 0, 0))
    return pl.pallas_call(
        _dft_inner_body,
        grid=(n1, P),
        in_specs=[gspec, gspec,
                  pl.BlockSpec((1, 2, 1, n2, C), lambda k, p: (p, 0, k, 0, 0)),
                  pl.BlockSpec((2, 1, n2, C), lambda k, p: (0, k, 0, 0))],
        out_specs=pl.BlockSpec((1, 2, 1, n2, C), lambda k, p: (p, 0, k, 0, 0)),
        out_shape=jax.ShapeDtypeStruct((P, 2, n1, n2, C), BF16),
        compiler_params=_params("parallel", "parallel"),
        name="dft_inner",
    )(gs, gis, a5, kspec)


def _dft_tables(n1, n2):
    n = n1 * n2
    h = n1 // 2
    a = np.arange(n1)
    ang1 = -2.0 * np.pi * ((a[:, None] * a[None, :]) % n1) / n1
    f1r, f1i = np.cos(ang1), np.sin(ang1)
    fwd_sig = np.block([[f1r[:, :h], -f1i[:, :h]], [f1i[:, :h], f1r[:, :h]]])
    fwd_flt = np.concatenate([f1r, f1i], axis=0)
    inv = np.block([[f1r[:h], f1i[:h]], [-f1i[:h], f1r[:h]]])
    b = np.arange(n2)
    ang_t = -2.0 * np.pi * ((np.arange(n1)[:, None] * b[None, :]) % n) / n
    ang_f = -2.0 * np.pi * ((b[:, None] * b[None, :]) % n2) / n2
    twr, twi = jnp.asarray(np.cos(ang_t), F32)[:, None, :], jnp.asarray(np.sin(ang_t), F32)[:, None, :]
    f2r, f2i = jnp.asarray(np.cos(ang_f), F32)[None], jnp.asarray(np.sin(ang_f), F32)[None]
    gr = twr * f2r - twi * f2i
    gi = twr * f2i + twi * f2r
    gs = jnp.concatenate([jnp.concatenate([gr, -gi], axis=2), jnp.concatenate([gi, gr], axis=2)], axis=1)
    gis = jnp.swapaxes(gs, 1, 2) * (1.0 / n)
    cvt = lambda m: jnp.asarray(m, dtype=F32).astype(BF16)
    return cvt(fwd_sig), cvt(fwd_flt), cvt(inv), gs.astype(BF16), gis.astype(BF16)


def _filter_embedding(L):
    t = jnp.linspace(0.0, 1.0, L, dtype=F32)
    w = 2.0 * math.pi * jnp.arange(L, dtype=F32) / L
    bands = jnp.linspace(1e-4, HY_BANDS - 1, HY_BANDS, dtype=F32)
    fw = w[:, None] * bands[None, :]
    emb = jnp.concatenate([t[:, None], jnp.cos(fw), -jnp.sin(fw)], axis=-1)
    valid = jnp.ones((L, 1), F32)
    emb = jnp.concatenate([emb, valid, jnp.zeros((L, EMB_PAD - HY_EMB - 1), F32)], axis=-1)
    back = jnp.concatenate([jnp.zeros((1, EMB_PAD), F32), emb[:0:-1]], axis=0)
    return jnp.concatenate([emb, back], axis=0)


def hyena_branch(z3, lp, hy_w, off_hy):
    B, L, _ = z3.shape
    n = 2 * L
    n2 = DFT_N2
    n1 = n // n2
    P = B // 2
    vin, x0 = hyena_pre(z3, lp['hy_conv_w'], lp['hy_conv_b'], hy_w, off_hy)
    fwd_sig, fwd_flt, inv, gs, gis = _dft_tables(n1, n2)

    hidden = lp['hf_w2'].shape[0]
    w1p = jnp.concatenate([lp['hf_w1'], jnp.zeros((EMB_PAD - HY_EMB, hidden), F32)], axis=0)
    kf, ss = hyena_filter(_filter_embedding(L), w1p, lp['hf_b1'], lp['hf_w2'], lp['hf_b2'], lp['hf_w3'],
                          lp['hf_b3'], lp['hf_freq'], lp['hy_decay'], hy_w)
    kscale = lax.rsqrt(ss + EPS)
    ka = left_matmul(fwd_flt, kf.reshape(1, n1, n2 * hy_w))
    kspec = filter_spectrum(gs, ka.reshape(1, 2, n1, n2, hy_w), kscale)

    a = left_matmul(fwd_sig, vin.reshape(P, n1, n2 * hy_w))
    pm = dft_inner(gs, gis, a.reshape(P, 2, n1, n2, hy_w), kspec)
    pstack = pm.reshape(P, 2 * n1, n2 * hy_w)
    bias_row = jnp.tile(lp['hy_bias'].reshape(1, hy_w), (1, n2))
    out = left_matmul(inv, pstack, post=(vin.reshape(P, n1, n2 * hy_w), x0.reshape(P, n1, n2 * hy_w), bias_row))
    return out.reshape(B, L, hy_w)


def _merge_body(a_ref, h_ref, m_ref, wa_ref, wh_ref, wm_ref, zg0_ref, zg1_ref, zg2_ref,
                bg0_ref, bg1_ref, bg2_ref, o_ref):
    def term(x_ref, w_ref, zg_ref, bg_ref):
        gate = jax.nn.sigmoid(zg_ref[...].astype(F32) + bg_ref[...])
        return gate * jnp.dot(x_ref[...], w_ref[...], preferred_element_type=F32)

    acc = term(a_ref, wa_ref, zg0_ref, bg0_ref)
    acc = acc + term(h_ref, wh_ref, zg1_ref, bg1_ref)
    acc = acc + term(m_ref, wm_ref, zg2_ref, bg2_ref)
    o_ref[...] = acc.astype(o_ref.dtype)


def gated_merge(a_out, h_out, m_out, wa, wh, wm, z2, b_gate, off_gate):
    T = a_out.shape[0]
    D = wa.shape[1]
    tm = min(ROW_TILE, T)
    tn = COL_TILE
    nj = D // tn
    gb = off_gate // tn

    def xspec(width):
        return pl.BlockSpec((tm, width), lambda i, j: (i, 0))

    def wspec(width):
        return pl.BlockSpec((width, tn), lambda i, j: (0, j))

    def zgspec(b):
        return pl.BlockSpec((tm, tn), lambda i, j: (i, gb + b * nj + j))

    def bgspec(b):
        return pl.BlockSpec((1, tn), lambda i, j: (0, b * nj + j))

    bg = b_gate.reshape(1, -1)
    return pl.pallas_call(
        _merge_body,
        grid=(T // tm, nj),
        in_specs=[xspec(a_out.shape[1]), xspec(h_out.shape[1]), xspec(m_out.shape[1]),
                  wspec(wa.shape[0]), wspec(wh.shape[0]), wspec(wm.shape[0]),
                  zgspec(0), zgspec(1), zgspec(2), bgspec(0), bgspec(1), bgspec(2)],
        out_specs=pl.BlockSpec((tm, tn), lambda i, j: (i, j)),
        out_shape=jax.ShapeDtypeStruct((T, D), BF16),
        compiler_params=_params("parallel", "parallel"),
        name="gated_merge",
    )(a_out, h_out, m_out, wa, wh, wm, z2, z2, z2, bg, bg, bg)


def _slab_store(ref, x):
    rows, width = x.shape
    n = width // LANES
    for s in range(n):
        ref[pl.ds(s, rows, stride=n), :] = x[:, s * LANES:(s + 1) * LANES]


def _slab_load(ref, n):
    rows = ref.shape[0] // n
    return jnp.concatenate([ref[pl.ds(s, rows, stride=n), :] for s in range(n)], axis=-1)


ROUTE_E, ROUTE_W, ROUTE_RANK = 0, 2, 4


def _route_tile(lg, carry):
    tm = lg.shape[0]
    lanef = lax.broadcasted_iota(jnp.int32, lg.shape, 1).astype(F32)
    row_max = lambda x: jnp.max(x, axis=-1, keepdims=True)
    first_at = lambda x, v: jnp.min(jnp.where(x == v, lanef, float(LANES)), axis=-1, keepdims=True)

    gl = jnp.where(lanef < N_GROUPS, lg, MASK_VALUE)
    gmax = row_max(gl)
    g_p = 1.0 / jnp.sum(jnp.where(lanef < N_GROUPS, jnp.exp(gl - gmax), 0.0), axis=-1, keepdims=True)
    lo = N_GROUPS + first_at(gl, gmax) * EXPERTS_PER_GROUP
    el = jnp.where((lanef >= lo) & (lanef < lo + EXPERTS_PER_GROUP), lg, MASK_VALUE)
    v1 = row_max(el)
    i1 = first_at(el, v1)
    el2 = jnp.where(lanef == i1, MASK_VALUE, el)
    v2 = row_max(el2)
    i2 = first_at(el2, v2)
    t = jnp.exp(v2 - v1)
    w1 = g_p / (1.0 + t)
    w2 = w1 * t
    e1, e2 = i1 - N_GROUPS, i2 - N_GROUPS

    oh1 = jnp.where(lanef == e1, 1.0, 0.0)
    oh2 = jnp.where(lanef == e2, 1.0, 0.0)
    oh = oh1 + oh2
    r = lax.broadcasted_iota(jnp.int32, (tm, tm), 0)
    c = lax.broadcasted_iota(jnp.int32, (tm, tm), 1)
    earlier = jnp.where(c < r, 1.0, 0.0).astype(BF16)
    before = jnp.dot(earlier, oh.astype(BF16), preferred_element_type=F32) + carry
    rank1 = jnp.sum(oh1 * before, axis=-1, keepdims=True)
    rank2 = jnp.sum(oh2 * before, axis=-1, keepdims=True)

    rec = jnp.zeros(lg.shape, F32)
    for lane_id, val in ((ROUTE_E, e1), (ROUTE_E + 1, e2), (ROUTE_W, w1), (ROUTE_W + 1, w2),
                         (ROUTE_RANK, rank1), (ROUTE_RANK + 1, rank2)):
        rec = jnp.where(lanef == lane_id, val, rec)
    return rec, carry + jnp.sum(oh, axis=0, keepdims=True)


def _pack_bf16_pairs(x):
    w = x.shape[1] // 2
    bits = pltpu.bitcast(x.astype(F32), jnp.uint32)
    return (bits[:, :w] & jnp.uint32(0xFFFF0000)) | (bits[:, w:] >> jnp.uint32(16))


def _unpack_bf16_pairs(words):
    hi = pltpu.bitcast(words & jnp.uint32(0xFFFF0000), F32)
    lo = pltpu.bitcast(words << jnp.uint32(16), F32)
    return jnp.concatenate([hi, lo], axis=-1).astype(BF16)


def _out_router_body(mg_ref, x_ref, wo_ref, g2_ref, w2_ref, br_ref, x1_ref, h2_ref, rec_ref, cnt_ref):
    @pl.when(pl.program_id(0) == 0)
    def _():
        cnt_ref[...] = jnp.zeros(cnt_ref.shape, F32)

    x1 = x_ref[...] + jnp.dot(mg_ref[...], wo_ref[...], preferred_element_type=F32)
    x1_ref[...] = x1
    h = _rms(x1, g2_ref[...])
    h_hi = h.astype(BF16)
    _slab_store(h2_ref, _pack_bf16_pairs(h_hi))
    h_lo = (h - h_hi.astype(F32)).astype(BF16)
    pa = jnp.dot(h_hi, w2_ref[...], preferred_element_type=F32)
    pb = jnp.dot(h_lo, w2_ref[...], preferred_element_type=F32)
    lg = (pa[:, :ROUTER_PAD] + pb[:, :ROUTER_PAD]) + (pa[:, ROUTER_PAD:] + pb[:, ROUTER_PAD:])
    rec, cnt = _route_tile(lg + br_ref[...], cnt_ref[...])
    rec_ref[...] = rec
    cnt_ref[...] = cnt


def out_router(merged, x2, w_out, g2, wr2, br):
    T, D = x2.shape
    tm = min(PREP_TILE, T)
    const = lambda i: (0, 0)
    row = lambda i: (i, 0)
    return pl.pallas_call(
        _out_router_body,
        grid=(T // tm,),
        in_specs=[pl.BlockSpec((tm, D), row), pl.BlockSpec((tm, D), row),
                  pl.BlockSpec((D, D), const), pl.BlockSpec((1, D), const),
                  pl.BlockSpec((D, 2 * ROUTER_PAD), const), pl.BlockSpec((1, ROUTER_PAD), const)],
        out_specs=[pl.BlockSpec((tm, D), row), pl.BlockSpec((tm * (D // (2 * LANES)), LANES), row),
                   pl.BlockSpec((tm, ROUTER_PAD), row), pl.BlockSpec((1, ROUTER_PAD), const)],
        out_shape=[jax.ShapeDtypeStruct((T, D), F32),
                   jax.ShapeDtypeStruct((T * (D // (2 * LANES)), LANES), jnp.uint32),
                   jax.ShapeDtypeStruct((T, ROUTER_PAD), F32), jax.ShapeDtypeStruct((1, ROUTER_PAD), F32)],
        compiler_params=_params("arbitrary"),
        name="out_router",
    )(merged, x2, w_out, g2.reshape(1, D), wr2, br)


def _row_copy(src_ref, src_row, dst_ref, dst_row, n, sem):
    src = src_ref.at[pl.ds(pl.multiple_of(src_row * n, n), n), :]
    dst = dst_ref.at[pl.ds(pl.multiple_of(dst_row * n, n), n), :]
    return pltpu.make_async_copy(src, dst, sem)


def _for_rows(rows, fn):
    def group(gi, carry):
        for u in range(DMA_UNROLL):
            fn(gi * DMA_UNROLL + u)
        return carry

    lax.fori_loop(0, rows // DMA_UNROLL, group, 0)


def _dispatch_body(dest_ref, h_ref, xs_in_ref, xs_ref, sem, *, n):
    del xs_in_ref
    rows = h_ref.shape[0] // n
    base = pl.program_id(0) * rows

    def copies(r):
        a = TOP_K * (base + r)
        return [_row_copy(h_ref, r, xs_ref, dest_ref[a + j], n, sem) for j in range(TOP_K)]

    def issue(r):
        for c in copies(r):
            c.start()

    def drain(r):
        for c in copies(r):
            c.wait()

    _for_rows(rows, issue)
    _for_rows(rows, drain)


def dispatch_rows(dest_flat, h_slab, n, P):
    T = h_slab.shape[0] // n
    rows = min(GATHER_ROWS, T)
    xs0 = jnp.zeros((P * n, LANES), h_slab.dtype)
    return pl.pallas_call(
        functools.partial(_dispatch_body, n=n),
        grid_spec=pltpu.PrefetchScalarGridSpec(
            num_scalar_prefetch=1,
            grid=(T // rows,),
            in_specs=[pl.BlockSpec((rows * n, LANES), lambda i, dest: (i, 0)),
                      pl.BlockSpec(memory_space=pl.ANY)],
            out_specs=pl.BlockSpec(memory_space=pl.ANY),
            scratch_shapes=[pltpu.SemaphoreType.DMA(())]),
        out_shape=jax.ShapeDtypeStruct((P * n, LANES), h_slab.dtype),
        input_output_aliases={2: 0},
        compiler_params=_params("arbitrary"),
        name="dispatch_rows",
    )(dest_flat, h_slab, xs0)


def _moe_body(blk_e_ref, n_used_ref, x_ref, wg_ref, wu_ref, wd_ref, o_ref):
    del blk_e_ref
    i = pl.program_id(0)
    n = wg_ref.shape[1] // LANES

    @pl.when(i < n_used_ref[0])
    def _():
        x = _unpack_bf16_pairs(_slab_load(x_ref, n // 2))
        g = jnp.dot(x, wg_ref[0], preferred_element_type=F32)
        u = jnp.dot(x, wu_ref[0], preferred_element_type=F32)
        a = (g * jax.nn.sigmoid(g) * u).astype(BF16)
        _slab_store(o_ref, jnp.dot(a, wd_ref[0], preferred_element_type=F32))

    @pl.when(i >= n_used_ref[0])
    def _():
        o_ref[...] = jnp.zeros(o_ref.shape, o_ref.dtype)


def moe_experts(blk_e, n_used, xs_slab, wg, wu, wd):
    _, D, De = wg.shape
    n = D // LANES
    P = xs_slab.shape[0] // (n // 2)
    tm = MOE_TM
    return pl.pallas_call(
        _moe_body,
        grid_spec=pltpu.PrefetchScalarGridSpec(
            num_scalar_prefetch=2,
            grid=(P // tm,),
            in_specs=[pl.BlockSpec((tm * n // 2, LANES), lambda i, be, nu: (i, 0)),
                      pl.BlockSpec((1, D, De), lambda i, be, nu: (be[i], 0, 0)),
                      pl.BlockSpec((1, D, De), lambda i, be, nu: (be[i], 0, 0)),
                      pl.BlockSpec((1, De, D), lambda i, be, nu: (be[i], 0, 0))],
            out_specs=pl.BlockSpec((tm * n, LANES), lambda i, be, nu: (i, 0))),
        out_shape=jax.ShapeDtypeStruct((P * n, LANES), F32),
        compiler_params=_params("arbitrary"),
        name="moe_experts",
    )(blk_e, n_used, xs_slab, wg, wu, wd)


def _combine_body(dest_ref, x1_ref, rec_ref, yb_ref, o_ref, *scratch, n_steps):
    bufs, sem = scratch[:TOP_K], scratch[TOP_K]
    tg, D = x1_ref.shape
    n = D // LANES
    group = 8
    i = pl.program_id(0)
    slot = i % 2

    def copies(step, r, to_slot):
        a = TOP_K * (step * tg + r)
        return [_row_copy(yb_ref, dest_ref[a + j], bufs[j].at[to_slot], r, n, sem) for j in range(TOP_K)]

    def issue(step, r, to_slot):
        for c in copies(step, r, to_slot):
            c.start()

    @pl.when(i == 0)
    def _():
        _for_rows(tg, lambda r: issue(0, r, 0))

    def drain(r):
        for c in copies(i, r, slot):
            c.wait()

    _for_rows(tg, drain)

    def combine_group(g):
        rows = pl.ds(pl.multiple_of(g * group, group), group)
        w = [jnp.broadcast_to(rec_ref[rows, ROUTE_W + j:ROUTE_W + j + 1], (group, LANES)) for j in range(TOP_K)]
        for s in range(n):
            cols = slice(s * LANES, (s + 1) * LANES)
            y = x1_ref[rows, cols]
            for j in range(TOP_K):
                y = y + w[j] * bufs[j][slot, pl.ds(g * (group * n) + s, group, stride=n), :]
            o_ref[rows, cols] = y

    def combine_and_prefetch(g, carry):
        for u in range(group):
            issue(i + 1, g * group + u, 1 - slot)
        combine_group(g)
        return carry

    def combine_only(g, carry):
        combine_group(g)
        return carry

    @pl.when(i + 1 < n_steps)
    def _():
        lax.fori_loop(0, tg // group, combine_and_prefetch, 0)

    @pl.when(i + 1 >= n_steps)
    def _():
        lax.fori_loop(0, tg // group, combine_only, 0)


def moe_combine(dest_flat, x1, rec, yb_slab):
    T, D = x1.shape
    tg = min(GATHER_ROWS, T)
    n = D // LANES
    return pl.pallas_call(
        functools.partial(_combine_body, n_steps=T // tg),
        grid_spec=pltpu.PrefetchScalarGridSpec(
            num_scalar_prefetch=1,
            grid=(T // tg,),
            in_specs=[pl.BlockSpec((tg, D), lambda i, dest: (i, 0)),
                      pl.BlockSpec((tg, ROUTER_PAD), lambda i, dest: (i, 0)),
                      pl.BlockSpec(memory_space=pl.ANY)],
            out_specs=pl.BlockSpec((tg, D), lambda i, dest: (i, 0)),
            scratch_shapes=[pltpu.VMEM((2, tg * n, LANES), F32) for _ in range(TOP_K)]
            + [pltpu.SemaphoreType.DMA(())]),
        out_shape=jax.ShapeDtypeStruct((T, D), F32),
        compiler_params=_params("arbitrary"),
        name="moe_combine",
    )(dest_flat, x1, rec, yb_slab)


def _block_layout(rec, counts_row):
    T = rec.shape[0]
    A = T * TOP_K
    P = (A + N_EXPERTS * (MOE_TM - 1) + MOE_TM - 1) // MOE_TM * MOE_TM
    n_blk = P // MOE_TM
    counts = counts_row[0, :N_EXPERTS].astype(jnp.int32)
    padded = (counts + MOE_TM - 1) // MOE_TM * MOE_TM
    pends = jnp.cumsum(padded)
    pstarts = pends - padded
    e = rec[:, ROUTE_E:ROUTE_E + TOP_K].astype(jnp.int32)
    rank = rec[:, ROUTE_RANK:ROUTE_RANK + TOP_K].astype(jnp.int32)
    one_hot = e[:, :, None] == jnp.arange(N_EXPERTS, dtype=jnp.int32)
    dest = jnp.sum(jnp.where(one_hot, pstarts, 0), axis=-1) + rank
    blk_start = jnp.arange(n_blk, dtype=jnp.int32) * MOE_TM
    blk_e = jnp.minimum(jnp.sum(blk_start[:, None] >= pends[None, :], axis=-1), N_EXPERTS - 1).astype(jnp.int32)
    n_used = (pends[-1:] // MOE_TM).astype(jnp.int32)
    return dest.reshape(A), blk_e, n_used, P


def _rope_tables(L):
    rows = L // GRID_W
    pairs = HEAD_DIM // 4
    row = jnp.repeat(jnp.arange(rows, dtype=F32), GRID_W)
    col = jnp.tile(jnp.arange(GRID_W, dtype=F32), rows)
    inv = jnp.power(ROPE_THETA, -jnp.arange(pairs, dtype=F32) / pairs)
    ang = jnp.concatenate([row[:, None] * inv, col[:, None] * inv], axis=-1)
    cos, sin = jnp.cos(ang), jnp.sin(ang)
    cosf = jnp.repeat(cos, 2, axis=-1)
    sinf = jnp.stack([-sin, sin], axis=-1).reshape(L, HEAD_DIM)
    return cosf, sinf


def _trunk(x, mem, lp, wb):
    B, L, D = x.shape
    T = B * L
    att_w = lp['w_br_attn'].shape[0]
    hy_w = lp['w_br_hyena'].shape[0]
    mem_w = lp['w_br_mem'].shape[0]
    kv_w = (lp['w_in'].shape[1] - att_w - 3 * hy_w - mem_w - 3 * D) // 2
    off_hy = att_w + 2 * kv_w
    off_mq = off_hy + 3 * hy_w
    off_gate = off_mq + mem_w
    x2 = x.reshape(T, D)

    z2 = normed_matmul(x2, lp['norm1_g'], wb['w_in'], BF16)
    z3 = z2.reshape(B, L, -1)
    cosf, sinf = _rope_tables(L)
    q, k2, v2, stats = qkv_prep(z3, cosf, sinf, lp['q_norm_g'], lp['k_norm_g'], att_w, kv_w)
    a_out = flash_attention(q, k2, v2, stats)
    h_out = hyena_branch(z3, lp, hy_w, off_hy)
    M = mem.shape[1]
    kv = normed_matmul(mem.reshape(B * M, D), lp['mem_norm_g'], wb['w_mem_kv'], F32).reshape(B, M, -1)
    m_out = mem_attention(z3, kv, lp['mq_norm_g'], lp['mk_norm_g'], mem_w, off_mq)
    merged = gated_merge(a_out.reshape(T, att_w), h_out.reshape(T, hy_w), m_out.reshape(T, mem_w),
                         wb['w_br_attn'], wb['w_br_hyena'], wb['w_br_mem'], z2, lp['b_gate'], off_gate)
    x1, h2, rec, counts = out_router(merged, x2, wb['w_out'], lp['norm2_g'], wb['wr2'], wb['br'])

    dest_flat, blk_e, n_used, P = _block_layout(rec, counts)
    xs = dispatch_rows(dest_flat, h2, D // (2 * LANES), P)
    yb = moe_experts(blk_e, n_used, xs, wb['w_gate_e'], wb['w_up_e'], wb['w_down_e'])
    y = moe_combine(dest_flat, x1, rec, yb)
    return y.reshape(B, L, D)


def kernel(x_prompt, x_sample, mem_prompt, mem_sample, norm1_g, w_in, b_gate, q_norm_g, k_norm_g, hy_conv_w, hy_conv_b, hf_w1, hf_b1, hf_w2, hf_b2, hf_w3, hf_b3, hf_freq, hy_decay, hy_bias, mem_norm_g, w_mem_kv, mq_norm_g, mk_norm_g, w_br_attn, w_br_hyena, w_br_mem, w_out, norm2_g, w_router_group, b_router_group, w_router_expert, b_router_expert, w_gate_e, w_up_e, w_down_e):
    params = dict(norm1_g=norm1_g, w_in=w_in, b_gate=b_gate, q_norm_g=q_norm_g, k_norm_g=k_norm_g,
                  hy_conv_w=hy_conv_w, hy_conv_b=hy_conv_b, hf_w1=hf_w1, hf_b1=hf_b1, hf_w2=hf_w2,
                  hf_b2=hf_b2, hf_w3=hf_w3, hf_b3=hf_b3, hf_freq=hf_freq, hy_decay=hy_decay,
                  hy_bias=hy_bias, mem_norm_g=mem_norm_g, w_mem_kv=w_mem_kv, mq_norm_g=mq_norm_g,
                  mk_norm_g=mk_norm_g, w_br_attn=w_br_attn, w_br_hyena=w_br_hyena, w_br_mem=w_br_mem,
                  w_out=w_out, norm2_g=norm2_g, w_router_group=w_router_group,
                  b_router_group=b_router_group, w_router_expert=w_router_expert,
                  b_router_expert=b_router_expert, w_gate_e=w_gate_e, w_up_e=w_up_e, w_down_e=w_down_e)
    depth = w_in.shape[0]
    xp, xs = x_prompt, x_sample
    for d in range(depth):
        lp = {name: arr[d] for name, arr in params.items()}
        D = lp['w_in'].shape[0]
        wb = {name: lp[name].astype(BF16) for name in
              ('w_in', 'w_mem_kv', 'w_br_attn', 'w_br_hyena', 'w_br_mem', 'w_out',
               'w_gate_e', 'w_up_e', 'w_down_e')}
        wr = jnp.concatenate([lp['w_router_group'], lp['w_router_expert'],
                              jnp.zeros((D, ROUTER_PAD - N_GROUPS - N_EXPERTS), F32)], axis=1)
        wr_hi = wr.astype(BF16)
        wb['wr2'] = jnp.concatenate([wr_hi, (wr - wr_hi.astype(F32)).astype(BF16)], axis=1)
        wb['br'] = jnp.concatenate([lp['b_router_group'], lp['b_router_expert'],
                                    jnp.zeros((ROUTER_PAD - N_GROUPS - N_EXPERTS,), F32)]).reshape(1, ROUTER_PAD)
        xp = _trunk(xp, mem_prompt, lp, wb)
        xs = _trunk(xs, mem_sample, lp, wb)
    return (xp, xs)
```

```python
import functools
import math

import jax
import jax.numpy as jnp
import numpy as np
from jax import lax
from jax.experimental import pallas as pl
from jax.experimental.pallas import tpu as pltpu

F32 = jnp.float32
BF16 = jnp.bfloat16

HEAD_DIM = 128
KV_GROUP = 4
GRID_W = 64
ROPE_THETA = 10000.0
HY_BANDS = 8
HY_EMB = 1 + 2 * HY_BANDS
HY_SHORT = 3
MEM_HEADS = 4
N_GROUPS = 4
EXPERTS_PER_GROUP = 8
N_EXPERTS = N_GROUPS * EXPERTS_PER_GROUP
TOP_K = 2
EPS = 1e-6
MASK_VALUE = -1e30
EXP2_SAFE_RANGE = 120.0
LANES = 128
MXU_DIM = 256
VMEM_LIMIT_BYTES = 52 * 1024 * 1024

ROW_TILE = 1024
COL_TILE = 512
PREP_TILE = 512
ATT_TQ = 1024
ATT_TK = 2048
DFT_N2 = 256
DFT_COLS = 4096
MOE_TM = 256
GATHER_ROWS = 512
DMA_UNROLL = 16
EMB_PAD = 32
ROUTER_PAD = 128


def _params(*sem):
    return pltpu.CompilerParams(dimension_semantics=sem, vmem_limit_bytes=VMEM_LIMIT_BYTES)


def _rms(x, g):
    ms = jnp.mean(x * x, axis=-1, keepdims=True)
    return x * lax.rsqrt(ms + EPS) * g


def _normed_matmul_body(x_ref, g_ref, w_ref, o_ref, xn_ref):
    @pl.when(pl.program_id(1) == 0)
    def _():
        xn_ref[...] = _rms(x_ref[...], g_ref[...]).astype(BF16)

    o_ref[...] = jnp.dot(xn_ref[...], w_ref[...], preferred_element_type=F32).astype(o_ref.dtype)


def normed_matmul(x, g, w, out_dtype):
    T, D = x.shape
    N = w.shape[1]
    tm = min(ROW_TILE, T)
    tn = COL_TILE
    return pl.pallas_call(
        _normed_matmul_body,
        grid=(T // tm, N // tn),
        in_specs=[pl.BlockSpec((tm, D), lambda i, j: (i, 0)),
                  pl.BlockSpec((1, D), lambda i, j: (0, 0)),
                  pl.BlockSpec((D, tn), lambda i, j: (0, j))],
        out_specs=pl.BlockSpec((tm, tn), lambda i, j: (i, j)),
        out_shape=jax.ShapeDtypeStruct((T, N), out_dtype),
        scratch_shapes=[pltpu.VMEM((tm, D), BF16)],
        compiler_params=_params("parallel", "arbitrary"),
        name="normed_matmul",
    )(x, g.reshape(1, D), w)


def _qkv_prep_body(zq_ref, zk_ref, zv_ref, cos_ref, sin_ref, gq_ref, gk_ref, q_ref, k_ref, v_ref, st_ref, *, scale):
    cosf = cos_ref[...]
    sinf = sin_ref[...]
    lane = lax.broadcasted_iota(jnp.int32, cosf.shape, 1)
    even = (lane % 2) == 0
    one_hot0 = jnp.where(lane == 0, 1.0, 0.0).astype(BF16)

    def prep(x, g):
        xn = _rms(x, g)
        partner = jnp.where(even, pltpu.roll(xn, LANES - 1, 1), pltpu.roll(xn, 1, 1))
        return xn * cosf + partner * sinf

    def max_norm2(xb):
        xf = xb.astype(F32)
        return jnp.max(jnp.sum(xf * xf, axis=-1, keepdims=True), axis=0, keepdims=True)

    qmax = jnp.zeros((1, 1), F32)
    for h in range(zq_ref.shape[-1] // HEAD_DIM):
        sl = slice(h * HEAD_DIM, (h + 1) * HEAD_DIM)
        qb = (prep(zq_ref[0, :, sl].astype(F32), gq_ref[...]) * scale).astype(BF16)
        q_ref[0, :, sl] = qb
        qmax = jnp.maximum(qmax, max_norm2(qb))
    stats = [qmax]
    for h in range(zk_ref.shape[-1] // HEAD_DIM):
        sl = slice(h * HEAD_DIM, (h + 1) * HEAD_DIM)
        kb = prep(zk_ref[0, :, sl].astype(F32), gk_ref[...]).astype(BF16)
        k_ref[0, :, 2 * h * HEAD_DIM:(2 * h + 1) * HEAD_DIM] = kb
        k_ref[0, :, (2 * h + 1) * HEAD_DIM:(2 * h + 2) * HEAD_DIM] = one_hot0
        v_ref[0, :, 2 * h * HEAD_DIM:(2 * h + 1) * HEAD_DIM] = zv_ref[0, :, sl].astype(BF16)
        v_ref[0, :, (2 * h + 1) * HEAD_DIM:(2 * h + 2) * HEAD_DIM] = one_hot0
        stats.append(max_norm2(kb))
    row = lax.broadcasted_iota(jnp.int32, (8, LANES), 0)
    st = jnp.zeros((8, LANES), F32)
    for r, val in enumerate(stats):
        st = jnp.where(row == r, val, st)
    st_ref[0, 0] = st


def qkv_prep(z3, cosf, sinf, gq, gk, att_w, kv_w):
    B, L, _ = z3.shape
    tl = min(PREP_TILE, L)
    off_k = att_w // kv_w
    body = functools.partial(_qkv_prep_body, scale=HEAD_DIM ** -0.5 * math.log2(math.e))
    return pl.pallas_call(
        body,
        grid=(B, L // tl),
        in_specs=[pl.BlockSpec((1, tl, att_w), lambda b, i: (b, i, 0)),
                  pl.BlockSpec((1, tl, kv_w), lambda b, i: (b, i, off_k)),
                  pl.BlockSpec((1, tl, kv_w), lambda b, i: (b, i, off_k + 1)),
                  pl.BlockSpec((tl, HEAD_DIM), lambda b, i: (i, 0)),
                  pl.BlockSpec((tl, HEAD_DIM), lambda b, i: (i, 0)),
                  pl.BlockSpec((1, HEAD_DIM), lambda b, i: (0, 0)),
                  pl.BlockSpec((1, HEAD_DIM), lambda b, i: (0, 0))],
        out_specs=[pl.BlockSpec((1, tl, att_w), lambda b, i: (b, i, 0)),
                   pl.BlockSpec((1, tl, 2 * kv_w), lambda b, i: (b, i, 0)),
                   pl.BlockSpec((1, tl, 2 * kv_w), lambda b, i: (b, i, 0)),
                   pl.BlockSpec((1, 1, 8, LANES), lambda b, i: (b, i, 0, 0))],
        out_shape=[jax.ShapeDtypeStruct((B, L, att_w), BF16),
                   jax.ShapeDtypeStruct((B, L, 2 * kv_w), BF16),
                   jax.ShapeDtypeStruct((B, L, 2 * kv_w), BF16),
                   jax.ShapeDtypeStruct((B, L // tl, 8, LANES), F32)],
        compiler_params=_params("parallel", "parallel"),
        name="qkv_prep",
    )(z3, z3, z3, cosf, sinf, gq.reshape(1, HEAD_DIM), gk.reshape(1, HEAD_DIM))


def _flash_body(fixed_ref, kmax_ref, q_ref, k_ref, v_ref, o_ref, qx_ref, acc_ref, m_ref, *, n_k, n_kv):
    b, g, kk = pl.program_id(0), pl.program_id(1), pl.program_id(3)
    nt_dims = (((1,), (1,)), ((), ()))
    tq = qx_ref.shape[1]

    @pl.when(kk == 0)
    def _():
        acc_ref[...] = jnp.zeros(acc_ref.shape, F32)
        m_ref[...] = jnp.full(m_ref.shape, MASK_VALUE, F32)
        kmax = kmax_ref[b * n_kv + g]
        lane0 = lax.broadcasted_iota(jnp.int32, (tq, HEAD_DIM), 1) == 0
        for h in range(KV_GROUP):
            q = q_ref[0, :, h * HEAD_DIM:(h + 1) * HEAD_DIM]
            qf = q.astype(F32)
            bound = jnp.sqrt(jnp.sum(qf * qf, axis=-1, keepdims=True)) * kmax
            qx_ref[h, :, :HEAD_DIM] = q
            qx_ref[h, :, HEAD_DIM:] = jnp.where(lane0, -bound, 0.0).astype(BF16)

    k = k_ref[0]
    v = v_ref[0]

    @pl.when(fixed_ref[0] == 1)
    def _():
        for h in range(KV_GROUP):
            s = lax.dot_general(qx_ref[h], k, nt_dims, preferred_element_type=F32)
            acc_ref[h] += jnp.dot(jnp.exp2(s).astype(BF16), v, preferred_element_type=F32)

    @pl.when(fixed_ref[0] == 0)
    def _():
        for h in range(KV_GROUP):
            s = lax.dot_general(qx_ref[h, :, :HEAD_DIM], k[:, :HEAD_DIM], nt_dims, preferred_element_type=F32)
            m_prev = m_ref[h]
            m_new = jnp.maximum(m_prev, jnp.max(s, axis=-1, keepdims=True))
            p = jnp.exp2(s - m_new).astype(BF16)
            acc_ref[h] = jnp.exp2(m_prev - m_new) * acc_ref[h] + jnp.dot(p, v, preferred_element_type=F32)
            m_ref[h] = m_new

    @pl.when(kk == n_k - 1)
    def _():
        for h in range(KV_GROUP):
            a = acc_ref[h]
            o_ref[0, :, h * HEAD_DIM:(h + 1) * HEAD_DIM] = (
                a[:, :HEAD_DIM] / a[:, HEAD_DIM:HEAD_DIM + 1]).astype(o_ref.dtype)


def flash_attention(q, k2, v2, stats):
    B, L, att_w = q.shape
    n_kv = k2.shape[-1] // (2 * HEAD_DIM)
    tq = min(ATT_TQ, L)
    tk = min(ATT_TK, L)
    gw = KV_GROUP * HEAD_DIM
    smax = jnp.sqrt(jnp.max(stats[:, :, :1 + n_kv, 0], axis=1))
    kmax = smax[:, 1:].reshape(B * n_kv)
    fixed = (2.0 * jnp.max(smax[:, :1] * smax[:, 1:]) < EXP2_SAFE_RANGE).astype(jnp.int32).reshape(1)
    return pl.pallas_call(
        functools.partial(_flash_body, n_k=L // tk, n_kv=n_kv),
        grid_spec=pltpu.PrefetchScalarGridSpec(
            num_scalar_prefetch=2,
            grid=(B, n_kv, L // tq, L // tk),
            in_specs=[pl.BlockSpec((1, tq, gw), lambda b, g, i, kk, *_: (b, i, g)),
                      pl.BlockSpec((1, tk, 2 * HEAD_DIM), lambda b, g, i, kk, *_: (b, kk, g)),
                      pl.BlockSpec((1, tk, 2 * HEAD_DIM), lambda b, g, i, kk, *_: (b, kk, g))],
            out_specs=pl.BlockSpec((1, tq, gw), lambda b, g, i, kk, *_: (b, i, g)),
            scratch_shapes=[pltpu.VMEM((KV_GROUP, tq, 2 * HEAD_DIM), BF16),
                            pltpu.VMEM((KV_GROUP, tq, 2 * HEAD_DIM), F32),
                            pltpu.VMEM((KV_GROUP, tq, 1), F32)]),
        out_shape=jax.ShapeDtypeStruct((B, L, att_w), BF16),
        compiler_params=_params("parallel", "parallel", "parallel", "arbitrary"),
        name="flash_attention",
    )(fixed, kmax, q, k2, v2)


def _mem_attn_body(zm_ref, kv_ref, gq_ref, gk_ref, o_ref, *, scale):
    mem_w = zm_ref.shape[-1]
    for h in range(mem_w // HEAD_DIM):
        sl = slice(h * HEAD_DIM, (h + 1) * HEAD_DIM)
        q = (_rms(zm_ref[0, :, sl].astype(F32), gq_ref[...]) * scale).astype(BF16)
        k = _rms(kv_ref[0, :, sl], gk_ref[...]).astype(BF16)
        v = kv_ref[0, :, mem_w + h * HEAD_DIM:mem_w + (h + 1) * HEAD_DIM].astype(BF16)
        s = lax.dot_general(q, k, (((1,), (1,)), ((), ())), preferred_element_type=F32)
        p = jnp.exp(s - jnp.max(s, axis=-1, keepdims=True))
        o = jnp.dot(p.astype(BF16), v, preferred_element_type=F32)
        o_ref[0, :, sl] = (o / jnp.sum(p, axis=-1, keepdims=True)).astype(o_ref.dtype)


def mem_attention(z3, kv, gq, gk, mem_w, off_mq):
    B, L, _ = z3.shape
    M = kv.shape[1]
    tl = min(PREP_TILE, L)
    body = functools.partial(_mem_attn_body, scale=HEAD_DIM ** -0.5)
    return pl.pallas_call(
        body,
        grid=(B, L // tl),
        in_specs=[pl.BlockSpec((1, tl, mem_w), lambda b, i: (b, i, off_mq // mem_w)),
                  pl.BlockSpec((1, M, 2 * mem_w), lambda b, i: (b, 0, 0)),
                  pl.BlockSpec((1, HEAD_DIM), lambda b, i: (0, 0)),
                  pl.BlockSpec((1, HEAD_DIM), lambda b, i: (0, 0))],
        out_specs=pl.BlockSpec((1, tl, mem_w), lambda b, i: (b, i, 0)),
        out_shape=jax.ShapeDtypeStruct((B, L, mem_w), BF16),
        compiler_params=_params("parallel", "parallel"),
        name="mem_attention",
    )(z3, kv, gq.reshape(1, HEAD_DIM), gk.reshape(1, HEAD_DIM))


def _hyena_pre_body(z_ref, prev_ref, next_ref, w_ref, b_ref, vin_ref, x0_ref, *, n_tiles):
    i = pl.program_id(1)
    x = z_ref[0].astype(F32)
    tl, width = x.shape
    hy_w = width // 3
    row = lax.broadcasted_iota(jnp.int32, x.shape, 0)
    halo = prev_ref.shape[1]
    prev_row = jnp.where(i > 0, prev_ref[0, halo - 1:halo, :].astype(F32), 0.0)
    next_row = jnp.where(i < n_tiles - 1, next_ref[0, 0:1, :].astype(F32), 0.0)
    x_prev = jnp.where(row == 0, prev_row, pltpu.roll(x, 1, 0))
    x_next = jnp.where(row == tl - 1, next_row, pltpu.roll(x, tl - 1, 0))
    u = w_ref[0:1, :] * x_prev + w_ref[1:2, :] * x + w_ref[2:3, :] * x_next + b_ref[...]
    vin_ref[0] = u[:, 2 * hy_w:] * u[:, hy_w:2 * hy_w]
    x0_ref[0] = u[:, :hy_w]


def hyena_pre(z3, conv_w, conv_b, hy_w, off_hy):
    B, L, _ = z3.shape
    width = 3 * hy_w
    tl = min(PREP_TILE, L)
    cb = off_hy // width
    halo = 16
    r8 = tl // halo
    last8 = L // halo - 1
    return pl.pallas_call(
        functools.partial(_hyena_pre_body, n_tiles=L // tl),
        grid=(B, L // tl),
        in_specs=[pl.BlockSpec((1, tl, width), lambda b, i: (b, i, cb)),
                  pl.BlockSpec((1, halo, width), lambda b, i: (b, jnp.maximum(i * r8 - 1, 0), cb)),
                  pl.BlockSpec((1, halo, width), lambda b, i: (b, jnp.minimum((i + 1) * r8, last8), cb)),
                  pl.BlockSpec((HY_SHORT, width), lambda b, i: (0, 0)),
                  pl.BlockSpec((1, width), lambda b, i: (0, 0))],
        out_specs=[pl.BlockSpec((1, tl, hy_w), lambda b, i: (b, i, 0)),
                   pl.BlockSpec((1, tl, hy_w), lambda b, i: (b, i, 0))],
        out_shape=[jax.ShapeDtypeStruct((B, L, hy_w), F32),
                   jax.ShapeDtypeStruct((B, L, hy_w), F32)],
        compiler_params=_params("parallel", "parallel"),
        name="hyena_pre",
    )(z3, z3, z3, conv_w.reshape(HY_SHORT, width), conv_b.reshape(1, width))


def _hyena_filter_body(e_ref, w1_ref, b1_ref, w2_ref, b2_ref, w3_ref, b3_ref, freq_ref, dec_ref,
                       kf_ref, ss_ref):
    hp = lax.Precision.HIGHEST
    e = e_ref[...]
    t = e[:, 0:1]
    valid = e[:, HY_EMB:HY_EMB + 1]
    freq = freq_ref[...]
    h = jnp.sin(freq * (jnp.dot(e, w1_ref[...], precision=hp, preferred_element_type=F32) + b1_ref[...]))
    h = jnp.sin(freq * (jnp.dot(h, w2_ref[...], precision=hp, preferred_element_type=F32) + b2_ref[...]))
    h = jnp.dot(h, w3_ref[...], precision=hp, preferred_element_type=F32) + b3_ref[...]
    kf = h * jnp.exp(-t * jnp.abs(dec_ref[...])) * valid
    kf_ref[...] = kf

    @pl.when(pl.program_id(0) == 0)
    def _():
        ss_ref[...] = jnp.zeros(ss_ref.shape, F32)

    ss_ref[...] += jnp.sum(kf * kf, axis=0, keepdims=True)


def hyena_filter(emb2, w1p, b1, w2, b2, w3, b3, freq, decay, hy_w):
    n_rows = emb2.shape[0]
    hidden = w2.shape[0]
    tl = min(PREP_TILE, n_rows // 2)
    nblk = n_rows // tl
    half = nblk // 2

    def dirmap(i):
        return (0, jnp.where(i >= half, 1, 0))

    const = lambda i: (0, 0)
    return pl.pallas_call(
        _hyena_filter_body,
        grid=(nblk,),
        in_specs=[pl.BlockSpec((tl, EMB_PAD), lambda i: (i, 0)),
                  pl.BlockSpec((EMB_PAD, hidden), const),
                  pl.BlockSpec((1, hidden), const),
                  pl.BlockSpec((hidden, hidden), const),
                  pl.BlockSpec((1, hidden), const),
                  pl.BlockSpec((hidden, hy_w), dirmap),
                  pl.BlockSpec((1, hy_w), dirmap),
                  pl.BlockSpec((1, hidden), const),
                  pl.BlockSpec((1, hy_w), dirmap)],
        out_specs=[pl.BlockSpec((tl, hy_w), lambda i: (i, 0)),
                   pl.BlockSpec((1, hy_w), const)],
        out_shape=[jax.ShapeDtypeStruct((n_rows, hy_w), F32),
                   jax.ShapeDtypeStruct((1, hy_w), F32)],
        compiler_params=_params("arbitrary"),
        name="hyena_filter",
    )(emb2, w1p, b1.reshape(1, hidden), w2, b2.reshape(1, hidden), w3, b3.reshape(1, -1),
      freq.reshape(1, hidden), decay.reshape(1, -1))


def _left_matmul_body(m_ref, x_ref, o_ref):
    o_ref[0] = jnp.dot(m_ref[...], x_ref[0].astype(BF16), preferred_element_type=F32).astype(o_ref.dtype)


def _left_matmul_post_body(m_ref, x_ref, vin_ref, x0_ref, bias_ref, o_ref):
    y = jnp.dot(m_ref[...], x_ref[0].astype(BF16), preferred_element_type=F32)
    vin = vin_ref[0]
    o_ref[0] = ((y + vin * bias_ref[...]) * x0_ref[0]).astype(o_ref.dtype)


def left_matmul(mat, x, post=None):
    P, K, cols = x.shape
    R = mat.shape[0]
    tc = min(DFT_COLS, cols)
    in_specs = [pl.BlockSpec((R, K), lambda p, c: (0, 0)),
                pl.BlockSpec((1, K, tc), lambda p, c: (p, 0, c))]
    args = [mat, x]
    if post is None:
        body, dtype = _left_matmul_body, BF16
    else:
        vin, x0, bias_row = post
        body, dtype = _left_matmul_post_body, BF16
        in_specs += [pl.BlockSpec((1, R, tc), lambda p, c: (p, 0, c)),
                     pl.BlockSpec((1, R, tc), lambda p, c: (p, 0, c)),
                     pl.BlockSpec((1, tc), lambda p, c: (0, c))]
        args += [vin, x0, bias_row]
    return pl.pallas_call(
        body,
        grid=(P, cols // tc),
        in_specs=in_specs,
        out_specs=pl.BlockSpec((1, R, tc), lambda p, c: (p, 0, c)),
        out_shape=jax.ShapeDtypeStruct((P, R, cols), dtype),
        compiler_params=_params("parallel", "parallel"),
        name="dft_outer",
    )(*args)


def _filter_spectrum_body(g_ref, a_ref, s_ref, k_ref):
    n2 = a_ref.shape[-2]
    a = jnp.concatenate([a_ref[0, 0, 0], a_ref[0, 1, 0]], axis=0).astype(BF16)
    y = jnp.dot(g_ref[0], a, preferred_element_type=F32) * s_ref[...]
    k_ref[0, 0] = y[:n2]
    k_ref[1, 0] = y[n2:]


def filter_spectrum(gs, a5, scale_row):
    _, _, n1, n2, C = a5.shape
    return pl.pallas_call(
        _filter_spectrum_body,
        grid=(n1,),
        in_specs=[pl.BlockSpec((1, 2 * n2, 2 * n2), lambda k: (k, 0, 0)),
                  pl.BlockSpec((1, 2, 1, n2, C), lambda k: (0, 0, k, 0, 0)),
                  pl.BlockSpec((1, C), lambda k: (0, 0))],
        out_specs=pl.BlockSpec((2, 1, n2, C), lambda k: (0, k, 0, 0)),
        out_shape=jax.ShapeDtypeStruct((2, n1, n2, C), F32),
        compiler_params=_params("parallel"),
        name="filter_spectrum",
    )(gs, a5, scale_row)


def _dft_inner_body(g_ref, gi_ref, a_ref, k_ref, p_ref):
    n2 = a_ref.shape[-2]
    a = jnp.concatenate([a_ref[0, 0, 0], a_ref[0, 1, 0]], axis=0).astype(BF16)
    y = jnp.dot(g_ref[0], a, preferred_element_type=F32)
    yr, yi = y[:n2], y[n2:]
    kr, ki = k_ref[0, 0], k_ref[1, 0]
    z = jnp.concatenate([yr * kr - yi * ki, yr * ki + yi * kr], axis=0).astype(BF16)
    p = jnp.dot(gi_ref[0], z, preferred_element_type=F32)
    p_ref[0, 0, 0] = p[:n2].astype(p_ref.dtype)
    p_ref[0, 1, 0] = p[n2:].astype(p_ref.dtype)


def dft_inner(gs, gis, a5, kspec):
    P, _, n1, n2, C = a5.shape
    gspec = pl.BlockSpec((1, 2 * n2, 2 * n2), lambda k, p: (k, 0, 0))
    return pl.pallas_call(
        _dft_inner_body,
        grid=(n1, P),
        in_specs=[gspec, gspec,
                  pl.BlockSpec((1, 2, 1, n2, C), lambda k, p: (p, 0, k, 0, 0)),
                  pl.BlockSpec((2, 1, n2, C), lambda k, p: (0, k, 0, 0))],
        out_specs=pl.BlockSpec((1, 2, 1, n2, C), lambda k, p: (p, 0, k, 0, 0)),
        out_shape=jax.ShapeDtypeStruct((P, 2, n1, n2, C), BF16),
        compiler_params=_params("parallel", "parallel"),
        name="dft_inner",
    )(gs, gis, a5, kspec)


def _dft_tables(n1, n2):
    n = n1 * n2
    h = n1 // 2
    a = np.arange(n1)
    ang1 = -2.0 * np.pi * ((a[:, None] * a[None, :]) % n1) / n1
    f1r, f1i = np.cos(ang1), np.sin(ang1)
    fwd_sig = np.block([[f1r[:, :h], -f1i[:, :h]], [f1i[:, :h], f1r[:, :h]]])
    fwd_flt = np.concatenate([f1r, f1i], axis=0)
    inv = np.block([[f1r[:h], f1i[:h]], [-f1i[:h], f1r[:h]]])
    b = np.arange(n2)
    ang_t = -2.0 * np.pi * ((np.arange(n1)[:, None] * b[None, :]) % n) / n
    ang_f = -2.0 * np.pi * ((b[:, None] * b[None, :]) % n2) / n2
    twr, twi = jnp.asarray(np.cos(ang_t), F32)[:, None, :], jnp.asarray(np.sin(ang_t), F32)[:, None, :]
    f2r, f2i = jnp.asarray(np.cos(ang_f), F32)[None], jnp.asarray(np.sin(ang_f), F32)[None]
    gr = twr * f2r - twi * f2i
    gi = twr * f2i + twi * f2r
    gs = jnp.concatenate([jnp.concatenate([gr, -gi], axis=2), jnp.concatenate([gi, gr], axis=2)], axis=1)
    gis = jnp.swapaxes(gs, 1, 2) * (1.0 / n)
    cvt = lambda m: jnp.asarray(m, dtype=F32).astype(BF16)
    return cvt(fwd_sig), cvt(fwd_flt), cvt(inv), gs.astype(BF16), gis.astype(BF16)


def _filter_embedding(L):
    t = jnp.linspace(0.0, 1.0, L, dtype=F32)
    w = 2.0 * math.pi * jnp.arange(L, dtype=F32) / L
    bands = jnp.linspace(1e-4, HY_BANDS - 1, HY_BANDS, dtype=F32)
    fw = w[:, None] * bands[None, :]
    emb = jnp.concatenate([t[:, None], jnp.cos(fw), -jnp.sin(fw)], axis=-1)
    valid = jnp.ones((L, 1), F32)
    emb = jnp.concatenate([emb, valid, jnp.zeros((L, EMB_PAD - HY_EMB - 1), F32)], axis=-1)
    back = jnp.concatenate([jnp.zeros((1, EMB_PAD), F32), emb[:0:-1]], axis=0)
    return jnp.concatenate([emb, back], axis=0)


def hyena_branch(z3, lp, hy_w, off_hy):
    B, L, _ = z3.shape
    n = 2 * L
    n2 = DFT_N2
    n1 = n // n2
    P = B // 2
    vin, x0 = hyena_pre(z3, lp['hy_conv_w'], lp['hy_conv_b'], hy_w, off_hy)
    fwd_sig, fwd_flt, inv, gs, gis = _dft_tables(n1, n2)

    hidden = lp['hf_w2'].shape[0]
    w1p = jnp.concatenate([lp['hf_w1'], jnp.zeros((EMB_PAD - HY_EMB, hidden), F32)], axis=0)
    kf, ss = hyena_filter(_filter_embedding(L), w1p, lp['hf_b1'], lp['hf_w2'], lp['hf_b2'], lp['hf_w3'],
                          lp['hf_b3'], lp['hf_freq'], lp['hy_decay'], hy_w)
    kscale = lax.rsqrt(ss + EPS)
    ka = left_matmul(fwd_flt, kf.reshape(1, n1, n2 * hy_w))
    kspec = filter_spectrum(gs, ka.reshape(1, 2, n1, n2, hy_w), kscale)

    a = left_matmul(fwd_sig, vin.reshape(P, n1, n2 * hy_w))
    pm = dft_inner(gs, gis, a.reshape(P, 2, n1, n2, hy_w), kspec)
    pstack = pm.reshape(P, 2 * n1, n2 * hy_w)
    bias_row = jnp.tile(lp['hy_bias'].reshape(1, hy_w), (1, n2))
    out = left_matmul(inv, pstack, post=(vin.reshape(P, n1, n2 * hy_w), x0.reshape(P, n1, n2 * hy_w), bias_row))
    return out.reshape(B, L, hy_w)


def _merge_body(a_ref, h_ref, m_ref, wa_ref, wh_ref, wm_ref, zg0_ref, zg1_ref, zg2_ref,
                bg0_ref, bg1_ref, bg2_ref, o_ref):
    def term(x_ref, w_ref, zg_ref, bg_ref):
        gate = jax.nn.sigmoid(zg_ref[...].astype(F32) + bg_ref[...])
        return gate * jnp.dot(x_ref[...], w_ref[...], preferred_element_type=F32)

    acc = term(a_ref, wa_ref, zg0_ref, bg0_ref)
    acc = acc + term(h_ref, wh_ref, zg1_ref, bg1_ref)
    acc = acc + term(m_ref, wm_ref, zg2_ref, bg2_ref)
    o_ref[...] = acc.astype(o_ref.dtype)


def gated_merge(a_out, h_out, m_out, wa, wh, wm, z2, b_gate, off_gate):
    T = a_out.shape[0]
    D = wa.shape[1]
    tm = min(ROW_TILE, T)
    tn = COL_TILE
    nj = D // tn
    gb = off_gate // tn

    def xspec(width):
        return pl.BlockSpec((tm, width), lambda i, j: (i, 0))

    def wspec(width):
        return pl.BlockSpec((width, tn), lambda i, j: (0, j))

    def zgspec(b):
        return pl.BlockSpec((tm, tn), lambda i, j: (i, gb + b * nj + j))

    def bgspec(b):
        return pl.BlockSpec((1, tn), lambda i, j: (0, b * nj + j))

    bg = b_gate.reshape(1, -1)
    return pl.pallas_call(
        _merge_body,
        grid=(T // tm, nj),
        in_specs=[xspec(a_out.shape[1]), xspec(h_out.shape[1]), xspec(m_out.shape[1]),
                  wspec(wa.shape[0]), wspec(wh.shape[0]), wspec(wm.shape[0]),
                  zgspec(0), zgspec(1), zgspec(2), bgspec(0), bgspec(1), bgspec(2)],
        out_specs=pl.BlockSpec((tm, tn), lambda i, j: (i, j)),
        out_shape=jax.ShapeDtypeStruct((T, D), BF16),
        compiler_params=_params("parallel", "parallel"),
        name="gated_merge",
    )(a_out, h_out, m_out, wa, wh, wm, z2, z2, z2, bg, bg, bg)


def _slab_store(ref, x):
    rows, width = x.shape
    n = width // LANES
    for s in range(n):
        ref[pl.ds(s, rows, stride=n), :] = x[:, s * LANES:(s + 1) * LANES]


def _slab_load(ref, n):
    rows = ref.shape[0] // n
    return jnp.concatenate([ref[pl.ds(s, rows, stride=n), :] for s in range(n)], axis=-1)


ROUTE_E, ROUTE_W, ROUTE_RANK = 0, 2, 4


def _route_tile(lg, carry):
    tm = lg.shape[0]
    lanef = lax.broadcasted_iota(jnp.int32, lg.shape, 1).astype(F32)
    row_max = lambda x: jnp.max(x, axis=-1, keepdims=True)
    first_at = lambda x, v: jnp.min(jnp.where(x == v, lanef, float(LANES)), axis=-1, keepdims=True)

    gl = jnp.where(lanef < N_GROUPS, lg, MASK_VALUE)
    gmax = row_max(gl)
    g_p = 1.0 / jnp.sum(jnp.where(lanef < N_GROUPS, jnp.exp(gl - gmax), 0.0), axis=-1, keepdims=True)
    lo = N_GROUPS + first_at(gl, gmax) * EXPERTS_PER_GROUP
    el = jnp.where((lanef >= lo) & (lanef < lo + EXPERTS_PER_GROUP), lg, MASK_VALUE)
    v1 = row_max(el)
    i1 = first_at(el, v1)
    el2 = jnp.where(lanef == i1, MASK_VALUE, el)
    v2 = row_max(el2)
    i2 = first_at(el2, v2)
    t = jnp.exp(v2 - v1)
    w1 = g_p / (1.0 + t)
    w2 = w1 * t
    e1, e2 = i1 - N_GROUPS, i2 - N_GROUPS

    oh1 = jnp.where(lanef == e1, 1.0, 0.0)
    oh2 = jnp.where(lanef == e2, 1.0, 0.0)
    oh = oh1 + oh2
    r = lax.broadcasted_iota(jnp.int32, (tm, tm), 0)
    c = lax.broadcasted_iota(jnp.int32, (tm, tm), 1)
    earlier = jnp.where(c < r, 1.0, 0.0).astype(BF16)
    before = jnp.dot(earlier, oh.astype(BF16), preferred_element_type=F32) + carry
    rank1 = jnp.sum(oh1 * before, axis=-1, keepdims=True)
    rank2 = jnp.sum(oh2 * before, axis=-1, keepdims=True)

    rec = jnp.zeros(lg.shape, F32)
    for lane_id, val in ((ROUTE_E, e1), (ROUTE_E + 1, e2), (ROUTE_W, w1), (ROUTE_W + 1, w2),
                         (ROUTE_RANK, rank1), (ROUTE_RANK + 1, rank2)):
        rec = jnp.where(lanef == lane_id, val, rec)
    return rec, carry + jnp.sum(oh, axis=0, keepdims=True)


def _pack_bf16_pairs(x):
    w = x.shape[1] // 2
    bits = pltpu.bitcast(x.astype(F32), jnp.uint32)
    return (bits[:, :w] & jnp.uint32(0xFFFF0000)) | (bits[:, w:] >> jnp.uint32(16))


def _unpack_bf16_pairs(words):
    hi = pltpu.bitcast(words & jnp.uint32(0xFFFF0000), F32)
    lo = pltpu.bitcast(words << jnp.uint32(16), F32)
    return jnp.concatenate([hi, lo], axis=-1).astype(BF16)


def _out_router_body(mg_ref, x_ref, wo_ref, g2_ref, w2_ref, br_ref, x1_ref, h2_ref, rec_ref, cnt_ref):
    @pl.when(pl.program_id(0) == 0)
    def _():
        cnt_ref[...] = jnp.zeros(cnt_ref.shape, F32)

    x1 = x_ref[...] + jnp.dot(mg_ref[...], wo_ref[...], preferred_element_type=F32)
    x1_ref[...] = x1
    h = _rms(x1, g2_ref[...])
    h_hi = h.astype(BF16)
    _slab_store(h2_ref, _pack_bf16_pairs(h_hi))
    h_lo = (h - h_hi.astype(F32)).astype(BF16)
    pa = jnp.dot(h_hi, w2_ref[...], preferred_element_type=F32)
    pb = jnp.dot(h_lo, w2_ref[...], preferred_element_type=F32)
    lg = (pa[:, :ROUTER_PAD] + pb[:, :ROUTER_PAD]) + (pa[:, ROUTER_PAD:] + pb[:, ROUTER_PAD:])
    rec, cnt = _route_tile(lg + br_ref[...], cnt_ref[...])
    rec_ref[...] = rec
    cnt_ref[...] = cnt


def out_router(merged, x2, w_out, g2, wr2, br):
    T, D = x2.shape
    tm = min(PREP_TILE, T)
    const = lambda i: (0, 0)
    row = lambda i: (i, 0)
    return pl.pallas_call(
        _out_router_body,
        grid=(T // tm,),
        in_specs=[pl.BlockSpec((tm, D), row), pl.BlockSpec((tm, D), row),
                  pl.BlockSpec((D, D), const), pl.BlockSpec((1, D), const),
                  pl.BlockSpec((D, 2 * ROUTER_PAD), const), pl.BlockSpec((1, ROUTER_PAD), const)],
        out_specs=[pl.BlockSpec((tm, D), row), pl.BlockSpec((tm * (D // (2 * LANES)), LANES), row),
                   pl.BlockSpec((tm, ROUTER_PAD), row), pl.BlockSpec((1, ROUTER_PAD), const)],
        out_shape=[jax.ShapeDtypeStruct((T, D), F32),
                   jax.ShapeDtypeStruct((T * (D // (2 * LANES)), LANES), jnp.uint32),
                   jax.ShapeDtypeStruct((T, ROUTER_PAD), F32), jax.ShapeDtypeStruct((1, ROUTER_PAD), F32)],
        compiler_params=_params("arbitrary"),
        name="out_router",
    )(merged, x2, w_out, g2.reshape(1, D), wr2, br)


def _row_copy(src_ref, src_row, dst_ref, dst_row, n, sem):
    src = src_ref.at[pl.ds(pl.multiple_of(src_row * n, n), n), :]
    dst = dst_ref.at[pl.ds(pl.multiple_of(dst_row * n, n), n), :]
    return pltpu.make_async_copy(src, dst, sem)


def _for_rows(rows, fn):
    def group(gi, carry):
        for u in range(DMA_UNROLL):
            fn(gi * DMA_UNROLL + u)
        return carry

    lax.fori_loop(0, rows // DMA_UNROLL, group, 0)


def _dispatch_body(dest_ref, h_ref, xs_in_ref, xs_ref, sem, *, n):
    del xs_in_ref
    rows = h_ref.shape[0] // n
    base = pl.program_id(0) * rows

    def copies(r):
        a = TOP_K * (base + r)
        return [_row_copy(h_ref, r, xs_ref, dest_ref[a + j], n, sem) for j in range(TOP_K)]

    def issue(r):
        for c in copies(r):
            c.start()

    def drain(r):
        for c in copies(r):
            c.wait()

    _for_rows(rows, issue)
    _for_rows(rows, drain)


def dispatch_rows(dest_flat, h_slab, n, P):
    T = h_slab.shape[0] // n
    rows = min(GATHER_ROWS, T)
    xs0 = jnp.zeros((P * n, LANES), h_slab.dtype)
    return pl.pallas_call(
        functools.partial(_dispatch_body, n=n),
        grid_spec=pltpu.PrefetchScalarGridSpec(
            num_scalar_prefetch=1,
            grid=(T // rows,),
            in_specs=[pl.BlockSpec((rows * n, LANES), lambda i, dest: (i, 0)),
                      pl.BlockSpec(memory_space=pl.ANY)],
            out_specs=pl.BlockSpec(memory_space=pl.ANY),
            scratch_shapes=[pltpu.SemaphoreType.DMA(())]),
        out_shape=jax.ShapeDtypeStruct((P * n, LANES), h_slab.dtype),
        input_output_aliases={2: 0},
        compiler_params=_params("arbitrary"),
        name="dispatch_rows",
    )(dest_flat, h_slab, xs0)


def _moe_body(blk_e_ref, n_used_ref, x_ref, wg_ref, wu_ref, wd_ref, o_ref):
    del blk_e_ref
    i = pl.program_id(0)
    n = wg_ref.shape[1] // LANES

    @pl.when(i < n_used_ref[0])
    def _():
        x = _unpack_bf16_pairs(_slab_load(x_ref, n // 2))
        g = jnp.dot(x, wg_ref[0], preferred_element_type=F32)
        u = jnp.dot(x, wu_ref[0], preferred_element_type=F32)
        a = (g * jax.nn.sigmoid(g) * u).astype(BF16)
        _slab_store(o_ref, jnp.dot(a, wd_ref[0], preferred_element_type=F32))

    @pl.when(i >= n_used_ref[0])
    def _():
        o_ref[...] = jnp.zeros(o_ref.shape, o_ref.dtype)


def moe_experts(blk_e, n_used, xs_slab, wg, wu, wd):
    _, D, De = wg.shape
    n = D // LANES
    P = xs_slab.shape[0] // (n // 2)
    tm = MOE_TM
    return pl.pallas_call(
        _moe_body,
        grid_spec=pltpu.PrefetchScalarGridSpec(
            num_scalar_prefetch=2,
            grid=(P // tm,),
            in_specs=[pl.BlockSpec((tm * n // 2, LANES), lambda i, be, nu: (i, 0)),
                      pl.BlockSpec((1, D, De), lambda i, be, nu: (be[i], 0, 0)),
                      pl.BlockSpec((1, D, De), lambda i, be, nu: (be[i], 0, 0)),
                      pl.BlockSpec((1, De, D), lambda i, be, nu: (be[i], 0, 0))],
            out_specs=pl.BlockSpec((tm * n, LANES), lambda i, be, nu: (i, 0))),
        out_shape=jax.ShapeDtypeStruct((P * n, LANES), F32),
        compiler_params=_params("arbitrary"),
        name="moe_experts",
    )(blk_e, n_used, xs_slab, wg, wu, wd)


def _combine_body(dest_ref, x1_ref, rec_ref, yb_ref, o_ref, *scratch, n_steps):
    bufs, sem = scratch[:TOP_K], scratch[TOP_K]
    tg, D = x1_ref.shape
    n = D // LANES
    group = 8
    i = pl.program_id(0)
    slot = i % 2

    def copies(step, r, to_slot):
        a = TOP_K * (step * tg + r)
        return [_row_copy(yb_ref, dest_ref[a + j], bufs[j].at[to_slot], r, n, sem) for j in range(TOP_K)]

    def issue(step, r, to_slot):
        for c in copies(step, r, to_slot):
            c.start()

    @pl.when(i == 0)
    def _():
        _for_rows(tg, lambda r: issue(0, r, 0))

    def drain(r):
        for c in copies(i, r, slot):
            c.wait()

    _for_rows(tg, drain)

    def combine_group(g):
        rows = pl.ds(pl.multiple_of(g * group, group), group)
        w = [jnp.broadcast_to(rec_ref[rows, ROUTE_W + j:ROUTE_W + j + 1], (group, LANES)) for j in range(TOP_K)]
        for s in range(n):
            cols = slice(s * LANES, (s + 1) * LANES)
            y = x1_ref[rows, cols]
            for j in range(TOP_K):
                y = y + w[j] * bufs[j][slot, pl.ds(g * (group * n) + s, group, stride=n), :]
            o_ref[rows, cols] = y

    def combine_and_prefetch(g, carry):
        for u in range(group):
            issue(i + 1, g * group + u, 1 - slot)
        combine_group(g)
        return carry

    def combine_only(g, carry):
        combine_group(g)
        return carry

    @pl.when(i + 1 < n_steps)
    def _():
        lax.fori_loop(0, tg // group, combine_and_prefetch, 0)

    @pl.when(i + 1 >= n_steps)
    def _():
        lax.fori_loop(0, tg // group, combine_only, 0)


def moe_combine(dest_flat, x1, rec, yb_slab):
    T, D = x1.shape
    tg = min(GATHER_ROWS, T)
    n = D // LANES
    return pl.pallas_call(
        functools.partial(_combine_body, n_steps=T // tg),
        grid_spec=pltpu.PrefetchScalarGridSpec(
            num_scalar_prefetch=1,
            grid=(T // tg,),
            in_specs=[pl.BlockSpec((tg, D), lambda i, dest: (i, 0)),
                      pl.BlockSpec((tg, ROUTER_PAD), lambda i, dest: (i, 0)),
                      pl.BlockSpec(memory_space=pl.ANY)],
            out_specs=pl.BlockSpec((tg, D), lambda i, dest: (i, 0)),
            scratch_shapes=[pltpu.VMEM((2, tg * n, LANES), F32) for _ in range(TOP_K)]
            + [pltpu.SemaphoreType.DMA(())]),
        out_shape=jax.ShapeDtypeStruct((T, D), F32),
        compiler_params=_params("arbitrary"),
        name="moe_combine",
    )(dest_flat, x1, rec, yb_slab)


def _block_layout(rec, counts_row):
    T = rec.shape[0]
    A = T * TOP_K
    P = (A + N_EXPERTS * (MOE_TM - 1) + MOE_TM - 1) // MOE_TM * MOE_TM
    n_blk = P // MOE_TM
    counts = counts_row[0, :N_EXPERTS].astype(jnp.int32)
    padded = (counts + MOE_TM - 1) // MOE_TM * MOE_TM
    pends = jnp.cumsum(padded)
    pstarts = pends - padded
    e = rec[:, ROUTE_E:ROUTE_E + TOP_K].astype(jnp.int32)
    rank = rec[:, ROUTE_RANK:ROUTE_RANK + TOP_K].astype(jnp.int32)
    one_hot = e[:, :, None] == jnp.arange(N_EXPERTS, dtype=jnp.int32)
    dest = jnp.sum(jnp.where(one_hot, pstarts, 0), axis=-1) + rank
    blk_start = jnp.arange(n_blk, dtype=jnp.int32) * MOE_TM
    blk_e = jnp.minimum(jnp.sum(blk_start[:, None] >= pends[None, :], axis=-1), N_EXPERTS - 1).astype(jnp.int32)
    n_used = (pends[-1:] // MOE_TM).astype(jnp.int32)
    return dest.reshape(A), blk_e, n_used, P


def _rope_tables(L):
    rows = L // GRID_W
    pairs = HEAD_DIM // 4
    row = jnp.repeat(jnp.arange(rows, dtype=F32), GRID_W)
    col = jnp.tile(jnp.arange(GRID_W, dtype=F32), rows)
    inv = jnp.power(ROPE_THETA, -jnp.arange(pairs, dtype=F32) / pairs)
    ang = jnp.concatenate([row[:, None] * inv, col[:, None] * inv], axis=-1)
    cos, sin = jnp.cos(ang), jnp.sin(ang)
    cosf = jnp.repeat(cos, 2, axis=-1)
    sinf = jnp.stack([-sin, sin], axis=-1).reshape(L, HEAD_DIM)
    return cosf, sinf


def _trunk(x, mem, lp, wb):
    B, L, D = x.shape
    T = B * L
    att_w = lp['w_br_attn'].shape[0]
    hy_w = lp['w_br_hyena'].shape[0]
    mem_w = lp['w_br_mem'].shape[0]
    kv_w = (lp['w_in'].shape[1] - att_w - 3 * hy_w - mem_w - 3 * D) // 2
    off_hy = att_w + 2 * kv_w
    off_mq = off_hy + 3 * hy_w
    off_gate = off_mq + mem_w
    x2 = x.reshape(T, D)

    z2 = normed_matmul(x2, lp['norm1_g'], wb['w_in'], BF16)
    z3 = z2.reshape(B, L, -1)
    cosf, sinf = _rope_tables(L)
    q, k2, v2, stats = qkv_prep(z3, cosf, sinf, lp['q_norm_g'], lp['k_norm_g'], att_w, kv_w)
    a_out = flash_attention(q, k2, v2, stats)
    h_out = hyena_branch(z3, lp, hy_w, off_hy)
    M = mem.shape[1]
    kv = normed_matmul(mem.reshape(B * M, D), lp['mem_norm_g'], wb['w_mem_kv'], F32).reshape(B, M, -1)
    m_out = mem_attention(z3, kv, lp['mq_norm_g'], lp['mk_norm_g'], mem_w, off_mq)
    merged = gated_merge(a_out.reshape(T, att_w), h_out.reshape(T, hy_w), m_out.reshape(T, mem_w),
                         wb['w_br_attn'], wb['w_br_hyena'], wb['w_br_mem'], z2, lp['b_gate'], off_gate)
    x1, h2, rec, counts = out_router(merged, x2, wb['w_out'], lp['norm2_g'], wb['wr2'], wb['br'])

    dest_flat, blk_e, n_used, P = _block_layout(rec, counts)
    xs = dispatch_rows(dest_flat, h2, D // (2 * LANES), P)
    yb = moe_experts(blk_e, n_used, xs, wb['w_gate_e'], wb['w_up_e'], wb['w_down_e'])
    y = moe_combine(dest_flat, x1, rec, yb)
    return y.reshape(B, L, D)


def kernel(x_prompt, x_sample, mem_prompt, mem_sample, norm1_g, w_in, b_gate, q_norm_g, k_norm_g, hy_conv_w, hy_conv_b, hf_w1, hf_b1, hf_w2, hf_b2, hf_w3, hf_b3, hf_freq, hy_decay, hy_bias, mem_norm_g, w_mem_kv, mq_norm_g, mk_norm_g, w_br_attn, w_br_hyena, w_br_mem, w_out, norm2_g, w_router_group, b_router_group, w_router_expert, b_router_expert, w_gate_e, w_up_e, w_down_e):
    params = dict(norm1_g=norm1_g, w_in=w_in, b_gate=b_gate, q_norm_g=q_norm_g, k_norm_g=k_norm_g,
                  hy_conv_w=hy_conv_w, hy_conv_b=hy_conv_b, hf_w1=hf_w1, hf_b1=hf_b1, hf_w2=hf_w2,
                  hf_b2=hf_b2, hf_w3=hf_w3, hf_b3=hf_b3, hf_freq=hf_freq, hy_decay=hy_decay,
                  hy_bias=hy_bias, mem_norm_g=mem_norm_g, w_mem_kv=w_mem_kv, mq_norm_g=mq_norm_g,
                  mk_norm_g=mk_norm_g, w_br_attn=w_br_attn, w_br_hyena=w_br_hyena, w_br_mem=w_br_mem,
                  w_out=w_out, norm2_g=norm2_g, w_router_group=w_router_group,
                  b_router_group=b_router_group, w_router_expert=w_router_expert,
                  b_router_expert=b_router_expert, w_gate_e=w_gate_e, w_up_e=w_up_e, w_down_e=w_down_e)
    depth = w_in.shape[0]
    xp, xs = x_prompt, x_sample
    for d in range(depth):
        lp = {name: arr[d] for name, arr in params.items()}
        D = lp['w_in'].shape[0]
        wb = {name: lp[name].astype(BF16) for name in
              ('w_in', 'w_mem_kv', 'w_br_attn', 'w_br_hyena', 'w_br_mem', 'w_out',
               'w_gate_e', 'w_up_e', 'w_down_e')}
        wr = jnp.concatenate([lp['w_router_group'], lp['w_router_expert'],
                              jnp.zeros((D, ROUTER_PAD - N_GROUPS - N_EXPERTS), F32)], axis=1)
        wr_hi = wr.astype(BF16)
        wb['wr2'] = jnp.concatenate([wr_hi, (wr - wr_hi.astype(F32)).astype(BF16)], axis=1)
        wb['br'] = jnp.concatenate([lp['b_router_group'], lp['b_router_expert'],
                                    jnp.zeros((ROUTER_PAD - N_GROUPS - N_EXPERTS,), F32)]).reshape(1, ROUTER_PAD)
        xp = _trunk(xp, mem_prompt, lp, wb)
        xs = _trunk(xs, mem_sample, lp, wb)
    return (xp, xs)
```

```python
import functools
import math

import jax
import jax.numpy as jnp
import numpy as np
from jax import lax
from jax.experimental import pallas as pl
from jax.experimental.pallas import tpu as pltpu

F32 = jnp.float32
BF16 = jnp.bfloat16

HEAD_DIM = 128
KV_GROUP = 4
GRID_W = 64
ROPE_THETA = 10000.0
HY_BANDS = 8
HY_EMB = 1 + 2 * HY_BANDS
HY_SHORT = 3
MEM_HEADS = 4
N_GROUPS = 4
EXPERTS_PER_GROUP = 8
N_EXPERTS = N_GROUPS * EXPERTS_PER_GROUP
TOP_K = 2
EPS = 1e-6
MASK_VALUE = -1e30
EXP2_SAFE_RANGE = 120.0
LANES = 128
MXU_DIM = 256
VMEM_LIMIT_BYTES = 52 * 1024 * 1024

ROW_TILE = 1024
COL_TILE = 512
PREP_TILE = 512
ATT_TQ = 1024
ATT_TK = 2048
DFT_N2 = 256
DFT_COLS = 4096
MOE_TM = 256
GATHER_ROWS = 512
DMA_UNROLL = 16
EMB_PAD = 32
ROUTER_PAD = 128


def _params(*sem):
    return pltpu.CompilerParams(dimension_semantics=sem, vmem_limit_bytes=VMEM_LIMIT_BYTES)


def _rms(x, g):
    ms = jnp.mean(x * x, axis=-1, keepdims=True)
    return x * lax.rsqrt(ms + EPS) * g


def _normed_matmul_body(x_ref, g_ref, w_ref, o_ref, xn_ref):
    @pl.when(pl.program_id(1) == 0)
    def _():
        xn_ref[...] = _rms(x_ref[...], g_ref[...]).astype(BF16)

    o_ref[...] = jnp.dot(xn_ref[...], w_ref[...], preferred_element_type=F32).astype(o_ref.dtype)


def normed_matmul(x, g, w, out_dtype):
    T, D = x.shape
    N = w.shape[1]
    tm = min(ROW_TILE, T)
    tn = COL_TILE
    return pl.pallas_call(
        _normed_matmul_body,
        grid=(T // tm, N // tn),
        in_specs=[pl.BlockSpec((tm, D), lambda i, j: (i, 0)),
                  pl.BlockSpec((1, D), lambda i, j: (0, 0)),
                  pl.BlockSpec((D, tn), lambda i, j: (0, j))],
        out_specs=pl.BlockSpec((tm, tn), lambda i, j: (i, j)),
        out_shape=jax.ShapeDtypeStruct((T, N), out_dtype),
        scratch_shapes=[pltpu.VMEM((tm, D), BF16)],
        compiler_params=_params("parallel", "arbitrary"),
        name="normed_matmul",
    )(x, g.reshape(1, D), w)


def _qkv_prep_body(zq_ref, zk_ref, zv_ref, cos_ref, sin_ref, gq_ref, gk_ref, q_ref, k_ref, v_ref, st_ref, *, scale):
    cosf = cos_ref[...]
    sinf = sin_ref[...]
    lane = lax.broadcasted_iota(jnp.int32, cosf.shape, 1)
    even = (lane % 2) == 0
    one_hot0 = jnp.where(lane == 0, 1.0, 0.0).astype(BF16)

    def prep(x, g):
        xn = _rms(x, g)
        partner = jnp.where(even, pltpu.roll(xn, LANES - 1, 1), pltpu.roll(xn, 1, 1))
        return xn * cosf + partner * sinf

    def max_norm2(xb):
        xf = xb.astype(F32)
        return jnp.max(jnp.sum(xf * xf, axis=-1, keepdims=True), axis=0, keepdims=True)

    qmax = jnp.zeros((1, 1), F32)
    for h in range(zq_ref.shape[-1] // HEAD_DIM):
        sl = slice(h * HEAD_DIM, (h + 1) * HEAD_DIM)
        qb = (prep(zq_ref[0, :, sl].astype(F32), gq_ref[...]) * scale).astype(BF16)
        q_ref[0, :, sl] = qb
        qmax = jnp.maximum(qmax, max_norm2(qb))
    stats = [qmax]
    for h in range(zk_ref.shape[-1] // HEAD_DIM):
        sl = slice(h * HEAD_DIM, (h + 1) * HEAD_DIM)
        kb = prep(zk_ref[0, :, sl].astype(F32), gk_ref[...]).astype(BF16)
        k_ref[0, :, 2 * h * HEAD_DIM:(2 * h + 1) * HEAD_DIM] = kb
        k_ref[0, :, (2 * h + 1) * HEAD_DIM:(2 * h + 2) * HEAD_DIM] = one_hot0
        v_ref[0, :, 2 * h * HEAD_DIM:(2 * h + 1) * HEAD_DIM] = zv_ref[0, :, sl].astype(BF16)
        v_ref[0, :, (2 * h + 1) * HEAD_DIM:(2 * h + 2) * HEAD_DIM] = one_hot0
        stats.append(max_norm2(kb))
    row = lax.broadcasted_iota(jnp.int32, (8, LANES), 0)
    st = jnp.zeros((8, LANES), F32)
    for r, val in enumerate(stats):
        st = jnp.where(row == r, val, st)
    st_ref[0, 0] = st


def qkv_prep(z3, cosf, sinf, gq, gk, att_w, kv_w):
    B, L, _ = z3.shape
    tl = min(PREP_TILE, L)
    off_k = att_w // kv_w
    body = functools.partial(_qkv_prep_body, scale=HEAD_DIM ** -0.5 * math.log2(math.e))
    return pl.pallas_call(
        body,
        grid=(B, L // tl),
        in_specs=[pl.BlockSpec((1, tl, att_w), lambda b, i: (b, i, 0)),
                  pl.BlockSpec((1, tl, kv_w), lambda b, i: (b, i, off_k)),
                  pl.BlockSpec((1, tl, kv_w), lambda b, i: (b, i, off_k + 1)),
                  pl.BlockSpec((tl, HEAD_DIM), lambda b, i: (i, 0)),
                  pl.BlockSpec((tl, HEAD_DIM), lambda b, i: (i, 0)),
                  pl.BlockSpec((1, HEAD_DIM), lambda b, i: (0, 0)),
                  pl.BlockSpec((1, HEAD_DIM), lambda b, i: (0, 0))],
        out_specs=[pl.BlockSpec((1, tl, att_w), lambda b, i: (b, i, 0)),
                   pl.BlockSpec((1, tl, 2 * kv_w), lambda b, i: (b, i, 0)),
                   pl.BlockSpec((1, tl, 2 * kv_w), lambda b, i: (b, i, 0)),
                   pl.BlockSpec((1, 1, 8, LANES), lambda b, i: (b, i, 0, 0))],
        out_shape=[jax.ShapeDtypeStruct((B, L, att_w), BF16),
                   jax.ShapeDtypeStruct((B, L, 2 * kv_w), BF16),
                   jax.ShapeDtypeStruct((B, L, 2 * kv_w), BF16),
                   jax.ShapeDtypeStruct((B, L // tl, 8, LANES), F32)],
        compiler_params=_params("parallel", "parallel"),
        name="qkv_prep",
    )(z3, z3, z3, cosf, sinf, gq.reshape(1, HEAD_DIM), gk.reshape(1, HEAD_DIM))


def _flash_body(fixed_ref, kmax_ref, q_ref, k_ref, v_ref, o_ref, qx_ref, acc_ref, m_ref, *, n_k, n_kv):
    b, g, kk = pl.program_id(0), pl.program_id(1), pl.program_id(3)
    nt_dims = (((1,), (1,)), ((), ()))
    tq = qx_ref.shape[1]

    @pl.when(kk == 0)
    def _():
        acc_ref[...] = jnp.zeros(acc_ref.shape, F32)
        m_ref[...] = jnp.full(m_ref.shape, MASK_VALUE, F32)
        kmax = kmax_ref[b * n_kv + g]
        lane0 = lax.broadcasted_iota(jnp.int32, (tq, HEAD_DIM), 1) == 0
        for h in range(KV_GROUP):
            q = q_ref[0, :, h * HEAD_DIM:(h + 1) * HEAD_DIM]
            qf = q.astype(F32)
            bound = jnp.sqrt(jnp.sum(qf * qf, axis=-1, keepdims=True)) * kmax
            qx_ref[h, :, :HEAD_DIM] = q
            qx_ref[h, :, HEAD_DIM:] = jnp.where(lane0, -bound, 0.0).astype(BF16)

    k = k_ref[0]
    v = v_ref[0]

    @pl.when(fixed_ref[0] == 1)
    def _():
        for h in range(KV_GROUP):
            s = lax.dot_general(qx_ref[h], k, nt_dims, preferred_element_type=F32)
            acc_ref[h] += jnp.dot(jnp.exp2(s).astype(BF16), v, preferred_element_type=F32)

    @pl.when(fixed_ref[0] == 0)
    def _():
        for h in range(KV_GROUP):
            s = lax.dot_general(qx_ref[h, :, :HEAD_DIM], k[:, :HEAD_DIM], nt_dims, preferred_element_type=F32)
            m_prev = m_ref[h]
            m_new = jnp.maximum(m_prev, jnp.max(s, axis=-1, keepdims=True))
            p = jnp.exp2(s - m_new).astype(BF16)
            acc_ref[h] = jnp.exp2(m_prev - m_new) * acc_ref[h] + jnp.dot(p, v, preferred_element_type=F32)
            m_ref[h] = m_new

    @pl.when(kk == n_k - 1)
    def _():
        for h in range(KV_GROUP):
            a = acc_ref[h]
            o_ref[0, :, h * HEAD_DIM:(h + 1) * HEAD_DIM] = (
                a[:, :HEAD_DIM] / a[:, HEAD_DIM:HEAD_DIM + 1]).astype(o_ref.dtype)


def flash_attention(q, k2, v2, stats):
    B, L, att_w = q.shape
    n_kv = k2.shape[-1] // (2 * HEAD_DIM)
    tq = min(ATT_TQ, L)
    tk = min(ATT_TK, L)
    gw = KV_GROUP * HEAD_DIM
    smax = jnp.sqrt(jnp.max(stats[:, :, :1 + n_kv, 0], axis=1))
    kmax = smax[:, 1:].reshape(B * n_kv)
    fixed = (2.0 * jnp.max(smax[:, :1] * smax[:, 1:]) < EXP2_SAFE_RANGE).astype(jnp.int32).reshape(1)
    return pl.pallas_call(
        functools.partial(_flash_body, n_k=L // tk, n_kv=n_kv),
        grid_spec=pltpu.PrefetchScalarGridSpec(
            num_scalar_prefetch=2,
            grid=(B, n_kv, L // tq, L // tk),
            in_specs=[pl.BlockSpec((1, tq, gw), lambda b, g, i, kk, *_: (b, i, g)),
                      pl.BlockSpec((1, tk, 2 * HEAD_DIM), lambda b, g, i, kk, *_: (b, kk, g)),
                      pl.BlockSpec((1, tk, 2 * HEAD_DIM), lambda b, g, i, kk, *_: (b, kk, g))],
            out_specs=pl.BlockSpec((1, tq, gw), lambda b, g, i, kk, *_: (b, i, g)),
            scratch_shapes=[pltpu.VMEM((KV_GROUP, tq, 2 * HEAD_DIM), BF16),
                            pltpu.VMEM((KV_GROUP, tq, 2 * HEAD_DIM), F32),
                            pltpu.VMEM((KV_GROUP, tq, 1), F32)]),
        out_shape=jax.ShapeDtypeStruct((B, L, att_w), BF16),
        compiler_params=_params("parallel", "parallel", "parallel", "arbitrary"),
        name="flash_attention",
    )(fixed, kmax, q, k2, v2)


def _mem_attn_body(zm_ref, kv_ref, gq_ref, gk_ref, o_ref, *, scale):
    mem_w = zm_ref.shape[-1]
    for h in range(mem_w // HEAD_DIM):
        sl = slice(h * HEAD_DIM, (h + 1) * HEAD_DIM)
        q = (_rms(zm_ref[0, :, sl].astype(F32), gq_ref[...]) * scale).astype(BF16)
        k = _rms(kv_ref[0, :, sl], gk_ref[...]).astype(BF16)
        v = kv_ref[0, :, mem_w + h * HEAD_DIM:mem_w + (h + 1) * HEAD_DIM].astype(BF16)
        s = lax.dot_general(q, k, (((1,), (1,)), ((), ())), preferred_element_type=F32)
        p = jnp.exp(s - jnp.max(s, axis=-1, keepdims=True))
        o = jnp.dot(p.astype(BF16), v, preferred_element_type=F32)
        o_ref[0, :, sl] = (o / jnp.sum(p, axis=-1, keepdims=True)).astype(o_ref.dtype)


def mem_attention(z3, kv, gq, gk, mem_w, off_mq):
    B, L, _ = z3.shape
    M = kv.shape[1]
    tl = min(PREP_TILE, L)
    body = functools.partial(_mem_attn_body, scale=HEAD_DIM ** -0.5)
    return pl.pallas_call(
        body,
        grid=(B, L // tl),
        in_specs=[pl.BlockSpec((1, tl, mem_w), lambda b, i: (b, i, off_mq // mem_w)),
                  pl.BlockSpec((1, M, 2 * mem_w), lambda b, i: (b, 0, 0)),
                  pl.BlockSpec((1, HEAD_DIM), lambda b, i: (0, 0)),
                  pl.BlockSpec((1, HEAD_DIM), lambda b, i: (0, 0))],
        out_specs=pl.BlockSpec((1, tl, mem_w), lambda b, i: (b, i, 0)),
        out_shape=jax.ShapeDtypeStruct((B, L, mem_w), BF16),
        compiler_params=_params("parallel", "parallel"),
        name="mem_attention",
    )(z3, kv, gq.reshape(1, HEAD_DIM), gk.reshape(1, HEAD_DIM))


def _hyena_pre_body(z_ref, prev_ref, next_ref, w_ref, b_ref, vin_ref, x0_ref, *, n_tiles):
    i = pl.program_id(1)
    x = z_ref[0].astype(F32)
    tl, width = x.shape
    hy_w = width // 3
    row = lax.broadcasted_iota(jnp.int32, x.shape, 0)
    halo = prev_ref.shape[1]
    prev_row = jnp.where(i > 0, prev_ref[0, halo - 1:halo, :].astype(F32), 0.0)
    next_row = jnp.where(i < n_tiles - 1, next_ref[0, 0:1, :].astype(F32), 0.0)
    x_prev = jnp.where(row == 0, prev_row, pltpu.roll(x, 1, 0))
    x_next = jnp.where(row == tl - 1, next_row, pltpu.roll(x, tl - 1, 0))
    u = w_ref[0:1, :] * x_prev + w_ref[1:2, :] * x + w_ref[2:3, :] * x_next + b_ref[...]
    vin_ref[0] = u[:, 2 * hy_w:] * u[:, hy_w:2 * hy_w]
    x0_ref[0] = u[:, :hy_w]


def hyena_pre(z3, conv_w, conv_b, hy_w, off_hy):
    B, L, _ = z3.shape
    width = 3 * hy_w
    tl = min(PREP_TILE, L)
    cb = off_hy // width
    halo = 16
    r8 = tl // halo
    last8 = L // halo - 1
    return pl.pallas_call(
        functools.partial(_hyena_pre_body, n_tiles=L // tl),
        grid=(B, L // tl),
        in_specs=[pl.BlockSpec((1, tl, width), lambda b, i: (b, i, cb)),
                  pl.BlockSpec((1, halo, width), lambda b, i: (b, jnp.maximum(i * r8 - 1, 0), cb)),
                  pl.BlockSpec((1, halo, width), lambda b, i: (b, jnp.minimum((i + 1) * r8, last8), cb)),
                  pl.BlockSpec((HY_SHORT, width), lambda b, i: (0, 0)),
                  pl.BlockSpec((1, width), lambda b, i: (0, 0))],
        out_specs=[pl.BlockSpec((1, tl, hy_w), lambda b, i: (b, i, 0)),
                   pl.BlockSpec((1, tl, hy_w), lambda b, i: (b, i, 0))],
        out_shape=[jax.ShapeDtypeStruct((B, L, hy_w), F32),
                   jax.ShapeDtypeStruct((B, L, hy_w), F32)],
        compiler_params=_params("parallel", "parallel"),
        name="hyena_pre",
    )(z3, z3, z3, conv_w.reshape(HY_SHORT, width), conv_b.reshape(1, width))


def _hyena_filter_body(e_ref, w1_ref, b1_ref, w2_ref, b2_ref, w3_ref, b3_ref, freq_ref, dec_ref,
                       kf_ref, ss_ref):
    hp = lax.Precision.HIGHEST
    e = e_ref[...]
    t = e[:, 0:1]
    valid = e[:, HY_EMB:HY_EMB + 1]
    freq = freq_ref[...]
    h = jnp.sin(freq * (jnp.dot(e, w1_ref[...], precision=hp, preferred_element_type=F32) + b1_ref[...]))
    h = jnp.sin(freq * (jnp.dot(h, w2_ref[...], precision=hp, preferred_element_type=F32) + b2_ref[...]))
    h = jnp.dot(h, w3_ref[...], precision=hp, preferred_element_type=F32) + b3_ref[...]
    kf = h * jnp.exp(-t * jnp.abs(dec_ref[...])) * valid
    kf_ref[...] = kf

    @pl.when(pl.program_id(0) == 0)
    def _():
        ss_ref[...] = jnp.zeros(ss_ref.shape, F32)

    ss_ref[...] += jnp.sum(kf * kf, axis=0, keepdims=True)


def hyena_filter(emb2, w1p, b1, w2, b2, w3, b3, freq, decay, hy_w):
    n_rows = emb2.shape[0]
    hidden = w2.shape[0]
    tl = min(PREP_TILE, n_rows // 2)
    nblk = n_rows // tl
    half = nblk // 2

    def dirmap(i):
        return (0, jnp.where(i >= half, 1, 0))

    const = lambda i: (0, 0)
    return pl.pallas_call(
        _hyena_filter_body,
        grid=(nblk,),
        in_specs=[pl.BlockSpec((tl, EMB_PAD), lambda i: (i, 0)),
                  pl.BlockSpec((EMB_PAD, hidden), const),
                  pl.BlockSpec((1, hidden), const),
                  pl.BlockSpec((hidden, hidden), const),
                  pl.BlockSpec((1, hidden), const),
                  pl.BlockSpec((hidden, hy_w), dirmap),
                  pl.BlockSpec((1, hy_w), dirmap),
                  pl.BlockSpec((1, hidden), const),
                  pl.BlockSpec((1, hy_w), dirmap)],
        out_specs=[pl.BlockSpec((tl, hy_w), lambda i: (i, 0)),
                   pl.BlockSpec((1, hy_w), const)],
        out_shape=[jax.ShapeDtypeStruct((n_rows, hy_w), F32),
                   jax.ShapeDtypeStruct((1, hy_w), F32)],
        compiler_params=_params("arbitrary"),
        name="hyena_filter",
    )(emb2, w1p, b1.reshape(1, hidden), w2, b2.reshape(1, hidden), w3, b3.reshape(1, -1),
      freq.reshape(1, hidden), decay.reshape(1, -1))


def _left_matmul_body(m_ref, x_ref, o_ref):
    o_ref[0] = jnp.dot(m_ref[...], x_ref[0].astype(BF16), preferred_element_type=F32).astype(o_ref.dtype)


def _left_matmul_post_body(m_ref, x_ref, vin_ref, x0_ref, bias_ref, o_ref):
    y = jnp.dot(m_ref[...], x_ref[0].astype(BF16), preferred_element_type=F32)
    vin = vin_ref[0]
    o_ref[0] = ((y + vin * bias_ref[...]) * x0_ref[0]).astype(o_ref.dtype)


def left_matmul(mat, x, post=None):
    P, K, cols = x.shape
    R = mat.shape[0]
    tc = min(DFT_COLS, cols)
    in_specs = [pl.BlockSpec((R, K), lambda p, c: (0, 0)),
                pl.BlockSpec((1, K, tc), lambda p, c: (p, 0, c))]
    args = [mat, x]
    if post is None:
        body, dtype = _left_matmul_body, BF16
    else:
        vin, x0, bias_row = post
        body, dtype = _left_matmul_post_body, BF16
        in_specs += [pl.BlockSpec((1, R, tc), lambda p, c: (p, 0, c)),
                     pl.BlockSpec((1, R, tc), lambda p, c: (p, 0, c)),
                     pl.BlockSpec((1, tc), lambda p, c: (0, c))]
        args += [vin, x0, bias_row]
    return pl.pallas_call(
        body,
        grid=(P, cols // tc),
        in_specs=in_specs,
        out_specs=pl.BlockSpec((1, R, tc), lambda p, c: (p, 0, c)),
        out_shape=jax.ShapeDtypeStruct((P, R, cols), dtype),
        compiler_params=_params("parallel", "parallel"),
        name="dft_outer",
    )(*args)


def _filter_spectrum_body(g_ref, a_ref, s_ref, k_ref):
    n2 = a_ref.shape[-2]
    a = jnp.concatenate([a_ref[0, 0, 0], a_ref[0, 1, 0]], axis=0).astype(BF16)
    y = jnp.dot(g_ref[0], a, preferred_element_type=F32) * s_ref[...]
    k_ref[0, 0] = y[:n2]
    k_ref[1, 0] = y[n2:]


def filter_spectrum(gs, a5, scale_row):
    _, _, n1, n2, C = a5.shape
    return pl.pallas_call(
        _filter_spectrum_body,
        grid=(n1,),
        in_specs=[pl.BlockSpec((1, 2 * n2, 2 * n2), lambda k: (k, 0, 0)),
                  pl.BlockSpec((1, 2, 1, n2, C), lambda k: (0, 0, k, 0, 0)),
                  pl.BlockSpec((1, C), lambda k: (0, 0))],
        out_specs=pl.BlockSpec((2, 1, n2, C), lambda k: (0, k, 0, 0)),
        out_shape=jax.ShapeDtypeStruct((2, n1, n2, C), F32),
        compiler_params=_params("parallel"),
        name="filter_spectrum",
    )(gs, a5, scale_row)


def _dft_inner_body(g_ref, gi_ref, a_ref, k_ref, p_ref):
    n2 = a_ref.shape[-2]
    a = jnp.concatenate([a_ref[0, 0, 0], a_ref[0, 1, 0]], axis=0).astype(BF16)
    y = jnp.dot(g_ref[0], a, preferred_element_type=F32)
    yr, yi = y[:n2], y[n2:]
    kr, ki = k_ref[0, 0], k_ref[1, 0]
    z = jnp.concatenate([yr * kr - yi * ki, yr * ki + yi * kr], axis=0).astype(BF16)
    p = jnp.dot(gi_ref[0], z, preferred_element_type=F32)
    p_ref[0, 0, 0] = p[:n2].astype(p_ref.dtype)
    p_ref[0, 1, 0] = p[n2:].astype(p_ref.dtype)


def dft_inner(gs, gis, a5, kspec):
    P, _, n1, n2, C = a5.shape
    gspec = pl.BlockSpec((1, 2 * n2, 2 * n2), lambda k, p: (k, 0, 0))
    return pl.pallas_call(
        _dft_inner_body,
        grid=(n1, P),
        in_specs=[gspec, gspec,
                  pl.BlockSpec((1, 2, 1, n2, C), lambda k, p: (p, 0, k, 0, 0)),
                  pl.BlockSpec((2, 1, n2, C), lambda k, p: (0, k, 0, 0))],
        out_specs=pl.BlockSpec((1, 2, 1, n2, C), lambda k, p: (p, 0, k, 0, 0)),
        out_shape=jax.ShapeDtypeStruct((P, 2, n1, n2, C), BF16),
        compiler_params=_params("parallel", "parallel"),
        name="dft_inner",
    )(gs, gis, a5, kspec)


def _dft_tables(n1, n2):
    n = n1 * n2
    h = n1 // 2
    a = np.arange(n1)
    ang1 = -2.0 * np.pi * ((a[:, None] * a[None, :]) % n1) / n1
    f1r, f1i = np.cos(ang1), np.sin(ang1)
    fwd_sig = np.block([[f1r[:, :h], -f1i[:, :h]], [f1i[:, :h], f1r[:, :h]]])
    fwd_flt = np.concatenate([f1r, f1i], axis=0)
    inv = np.block([[f1r[:h], f1i[:h]], [-f1i[:h], f1r[:h]]])
    b = np.arange(n2)
    ang_t = -2.0 * np.pi * ((np.arange(n1)[:, None] * b[None, :]) % n) / n
    ang_f = -2.0 * np.pi * ((b[:, None] * b[None, :]) % n2) / n2
    twr, twi = jnp.asarray(np.cos(ang_t), F32)[:, None, :], jnp.asarray(np.sin(ang_t), F32)[:, None, :]
    f2r, f2i = jnp.asarray(np.cos(ang_f), F32)[None], jnp.asarray(np.sin(ang_f), F32)[None]
    gr = twr * f2r - twi * f2i
    gi = twr * f2i + twi * f2r
    gs = jnp.concatenate([jnp.concatenate([gr, -gi], axis=2), jnp.concatenate([gi, gr], axis=2)], axis=1)
    gis = jnp.swapaxes(gs, 1, 2) * (1.0 / n)
    cvt = lambda m: jnp.asarray(m, dtype=F32).astype(BF16)
    return cvt(fwd_sig), cvt(fwd_flt), cvt(inv), gs.astype(BF16), gis.astype(BF16)


def _filter_embedding(L):
    t = jnp.linspace(0.0, 1.0, L, dtype=F32)
    w = 2.0 * math.pi * jnp.arange(L, dtype=F32) / L
    bands = jnp.linspace(1e-4, HY_BANDS - 1, HY_BANDS, dtype=F32)
    fw = w[:, None] * bands[None, :]
    emb = jnp.concatenate([t[:, None], jnp.cos(fw), -jnp.sin(fw)], axis=-1)
    valid = jnp.ones((L, 1), F32)
    emb = jnp.concatenate([emb, valid, jnp.zeros((L, EMB_PAD - HY_EMB - 1), F32)], axis=-1)
    back = jnp.concatenate([jnp.zeros((1, EMB_PAD), F32), emb[:0:-1]], axis=0)
    return jnp.concatenate([emb, back], axis=0)


def hyena_branch(z3, lp, hy_w, off_hy):
    B, L, _ = z3.shape
    n = 2 * L
    n2 = DFT_N2
    n1 = n // n2
    P = B // 2
    vin, x0 = hyena_pre(z3, lp['hy_conv_w'], lp['hy_conv_b'], hy_w, off_hy)
    fwd_sig, fwd_flt, inv, gs, gis = _dft_tables(n1, n2)

    hidden = lp['hf_w2'].shape[0]
    w1p = jnp.concatenate([lp['hf_w1'], jnp.zeros((EMB_PAD - HY_EMB, hidden), F32)], axis=0)
    kf, ss = hyena_filter(_filter_embedding(L), w1p, lp['hf_b1'], lp['hf_w2'], lp['hf_b2'], lp['hf_w3'],
                          lp['hf_b3'], lp['hf_freq'], lp['hy_decay'], hy_w)
    kscale = lax.rsqrt(ss + EPS)
    ka = left_matmul(fwd_flt, kf.reshape(1, n1, n2 * hy_w))
    kspec = filter_spectrum(gs, ka.reshape(1, 2, n1, n2, hy_w), kscale)

    a = left_matmul(fwd_sig, vin.reshape(P, n1, n2 * hy_w))
    pm = dft_inner(gs, gis, a.reshape(P, 2, n1, n2, hy_w), kspec)
    pstack = pm.reshape(P, 2 * n1, n2 * hy_w)
    bias_row = jnp.tile(lp['hy_bias'].reshape(1, hy_w), (1, n2))
    out = left_matmul(inv, pstack, post=(vin.reshape(P, n1, n2 * hy_w), x0.reshape(P, n1, n2 * hy_w), bias_row))
    return out.reshape(B, L, hy_w)


def _merge_body(a_ref, h_ref, m_ref, wa_ref, wh_ref, wm_ref, zg0_ref, zg1_ref, zg2_ref,
                bg0_ref, bg1_ref, bg2_ref, o_ref):
    def term(x_ref, w_ref, zg_ref, bg_ref):
        gate = jax.nn.sigmoid(zg_ref[...].astype(F32) + bg_ref[...])
        return gate * jnp.dot(x_ref[...], w_ref[...], preferred_element_type=F32)

    acc = term(a_ref, wa_ref, zg0_ref, bg0_ref)
    acc = acc + term(h_ref, wh_ref, zg1_ref, bg1_ref)
    acc = acc + term(m_ref, wm_ref, zg2_ref, bg2_ref)
    o_ref[...] = acc.astype(o_ref.dtype)


def gated_merge(a_out, h_out, m_out, wa, wh, wm, z2, b_gate, off_gate):
    T = a_out.shape[0]
    D = wa.shape[1]
    tm = min(ROW_TILE, T)
    tn = COL_TILE
    nj = D // tn
    gb = off_gate // tn

    def xspec(width):
        return pl.BlockSpec((tm, width), lambda i, j: (i, 0))

    def wspec(width):
        return pl.BlockSpec((width, tn), lambda i, j: (0, j))

    def zgspec(b):
        return pl.BlockSpec((tm, tn), lambda i, j: (i, gb + b * nj + j))

    def bgspec(b):
        return pl.BlockSpec((1, tn), lambda i, j: (0, b * nj + j))

    bg = b_gate.reshape(1, -1)
    return pl.pallas_call(
        _merge_body,
        grid=(T // tm, nj),
        in_specs=[xspec(a_out.shape[1]), xspec(h_out.shape[1]), xspec(m_out.shape[1]),
                  wspec(wa.shape[0]), wspec(wh.shape[0]), wspec(wm.shape[0]),
                  zgspec(0), zgspec(1), zgspec(2), bgspec(0), bgspec(1), bgspec(2)],
        out_specs=pl.BlockSpec((tm, tn), lambda i, j: (i, j)),
        out_shape=jax.ShapeDtypeStruct((T, D), BF16),
        compiler_params=_params("parallel", "parallel"),
        name="gated_merge",
    )(a_out, h_out, m_out, wa, wh, wm, z2, z2, z2, bg, bg, bg)


def _slab_store(ref, x):
    rows, width = x.shape
    n = width // LANES
    for s in range(n):
        ref[pl.ds(s, rows, stride=n), :] = x[:, s * LANES:(s + 1) * LANES]


def _slab_load(ref, n):
    rows = ref.shape[0] // n
    return jnp.concatenate([ref[pl.ds(s, rows, stride=n), :] for s in range(n)], axis=-1)


ROUTE_E, ROUTE_W, ROUTE_RANK = 0, 2, 4


def _route_tile(lg, carry):
    tm = lg.shape[0]
    lanef = lax.broadcasted_iota(jnp.int32, lg.shape, 1).astype(F32)
    row_max = lambda x: jnp.max(x, axis=-1, keepdims=True)
    first_at = lambda x, v: jnp.min(jnp.where(x == v, lanef, float(LANES)), axis=-1, keepdims=True)

    gl = jnp.where(lanef < N_GROUPS, lg, MASK_VALUE)
    gmax = row_max(gl)
    g_p = 1.0 / jnp.sum(jnp.where(lanef < N_GROUPS, jnp.exp(gl - gmax), 0.0), axis=-1, keepdims=True)
    lo = N_GROUPS + first_at(gl, gmax) * EXPERTS_PER_GROUP
    el = jnp.where((lanef >= lo) & (lanef < lo + EXPERTS_PER_GROUP), lg, MASK_VALUE)
    v1 = row_max(el)
    i1 = first_at(el, v1)
    el2 = jnp.where(lanef == i1, MASK_VALUE, el)
    v2 = row_max(el2)
    i2 = first_at(el2, v2)
    t = jnp.exp(v2 - v1)
    w1 = g_p / (1.0 + t)
    w2 = w1 * t
    e1, e2 = i1 - N_GROUPS, i2 - N_GROUPS

    oh1 = jnp.where(lanef == e1, 1.0, 0.0)
    oh2 = jnp.where(lanef == e2, 1.0, 0.0)
    oh = oh1 + oh2
    r = lax.broadcasted_iota(jnp.int32, (tm, tm), 0)
    c = lax.broadcasted_iota(jnp.int32, (tm, tm), 1)
    earlier = jnp.where(c < r, 1.0, 0.0).astype(BF16)
    before = jnp.dot(earlier, oh.astype(BF16), preferred_element_type=F32) + carry
    rank1 = jnp.sum(oh1 * before, axis=-1, keepdims=True)
    rank2 = jnp.sum(oh2 * before, axis=-1, keepdims=True)

    rec = jnp.zeros(lg.shape, F32)
    for lane_id, val in ((ROUTE_E, e1), (ROUTE_E + 1, e2), (ROUTE_W, w1), (ROUTE_W + 1, w2),
                         (ROUTE_RANK, rank1), (ROUTE_RANK + 1, rank2)):
        rec = jnp.where(lanef == lane_id, val, rec)
    return rec, carry + jnp.sum(oh, axis=0, keepdims=True)


def _pack_bf16_pairs(x):
    w = x.shape[1] // 2
    bits = pltpu.bitcast(x.astype(F32), jnp.uint32)
    return (bits[:, :w] & jnp.uint32(0xFFFF0000)) | (bits[:, w:] >> jnp.uint32(16))


def _unpack_bf16_pairs(words):
    hi = pltpu.bitcast(words & jnp.uint32(0xFFFF0000), F32)
    lo = pltpu.bitcast(words << jnp.uint32(16), F32)
    return jnp.concatenate([hi, lo], axis=-1).astype(BF16)


def _out_router_body(mg_ref, x_ref, wo_ref, g2_ref, w2_ref, br_ref, x1_ref, h2_ref, rec_ref, cnt_ref):
    @pl.when(pl.program_id(0) == 0)
    def _():
        cnt_ref[...] = jnp.zeros(cnt_ref.shape, F32)

    x1 = x_ref[...] + jnp.dot(mg_ref[...], wo_ref[...], preferred_element_type=F32)
    x1_ref[...] = x1
    h = _rms(x1, g2_ref[...])
    h_hi = h.astype(BF16)
    _slab_store(h2_ref, _pack_bf16_pairs(h_hi))
    h_lo = (h - h_hi.astype(F32)).astype(BF16)
    pa = jnp.dot(h_hi, w2_ref[...], preferred_element_type=F32)
    pb = jnp.dot(h_lo, w2_ref[...], preferred_element_type=F32)
    lg = (pa[:, :ROUTER_PAD] + pb[:, :ROUTER_PAD]) + (pa[:, ROUTER_PAD:] + pb[:, ROUTER_PAD:])
    rec, cnt = _route_tile(lg + br_ref[...], cnt_ref[...])
    rec_ref[...] = rec
    cnt_ref[...] = cnt


def out_router(merged, x2, w_out, g2, wr2, br):
    T, D = x2.shape
    tm = min(PREP_TILE, T)
    const = lambda i: (0, 0)
    row = lambda i: (i, 0)
    return pl.pallas_call(
        _out_router_body,
        grid=(T // tm,),
        in_specs=[pl.BlockSpec((tm, D), row), pl.BlockSpec((tm, D), row),
                  pl.BlockSpec((D, D), const), pl.BlockSpec((1, D), const),
                  pl.BlockSpec((D, 2 * ROUTER_PAD), const), pl.BlockSpec((1, ROUTER_PAD), const)],
        out_specs=[pl.BlockSpec((tm, D), row), pl.BlockSpec((tm * (D // (2 * LANES)), LANES), row),
                   pl.BlockSpec((tm, ROUTER_PAD), row), pl.BlockSpec((1, ROUTER_PAD), const)],
        out_shape=[jax.ShapeDtypeStruct((T, D), F32),
                   jax.ShapeDtypeStruct((T * (D // (2 * LANES)), LANES), jnp.uint32),
                   jax.ShapeDtypeStruct((T, ROUTER_PAD), F32), jax.ShapeDtypeStruct((1, ROUTER_PAD), F32)],
        compiler_params=_params("arbitrary"),
        name="out_router",
    )(merged, x2, w_out, g2.reshape(1, D), wr2, br)


def _row_copy(src_ref, src_row, dst_ref, dst_row, n, sem):
    src = src_ref.at[pl.ds(pl.multiple_of(src_row * n, n), n), :]
    dst = dst_ref.at[pl.ds(pl.multiple_of(dst_row * n, n), n), :]
    return pltpu.make_async_copy(src, dst, sem)


def _for_rows(rows, fn):
    def group(gi, carry):
        for u in range(DMA_UNROLL):
            fn(gi * DMA_UNROLL + u)
        return carry

    lax.fori_loop(0, rows // DMA_UNROLL, group, 0)


def _dispatch_body(dest_ref, h_ref, xs_in_ref, xs_ref, sem, *, n):
    del xs_in_ref
    rows = h_ref.shape[0] // n
    base = pl.program_id(0) * rows

    def copies(r):
        a = TOP_K * (base + r)
        return [_row_copy(h_ref, r, xs_ref, dest_ref[a + j], n, sem) for j in range(TOP_K)]

    def issue(r):
        for j, c in enumerate(copies(r)):
            c.start(priority=j % 2)

    def drain(r):
        for c in copies(r):
            c.wait()

    _for_rows(rows, issue)
    _for_rows(rows, drain)


def dispatch_rows(dest_flat, h_slab, n, P):
    T = h_slab.shape[0] // n
    rows = min(GATHER_ROWS, T)
    xs0 = jnp.zeros((P * n, LANES), h_slab.dtype)
    return pl.pallas_call(
        functools.partial(_dispatch_body, n=n),
        grid_spec=pltpu.PrefetchScalarGridSpec(
            num_scalar_prefetch=1,
            grid=(T // rows,),
            in_specs=[pl.BlockSpec((rows * n, LANES), lambda i, dest: (i, 0)),
                      pl.BlockSpec(memory_space=pl.ANY)],
            out_specs=pl.BlockSpec(memory_space=pl.ANY),
            scratch_shapes=[pltpu.SemaphoreType.DMA(())]),
        out_shape=jax.ShapeDtypeStruct((P * n, LANES), h_slab.dtype),
        input_output_aliases={2: 0},
        compiler_params=_params("arbitrary"),
        name="dispatch_rows",
    )(dest_flat, h_slab, xs0)


def _moe_body(blk_e_ref, n_used_ref, x_ref, wg_ref, wu_ref, wd_ref, o_ref):
    del blk_e_ref
    i = pl.program_id(0)
    n = wg_ref.shape[1] // LANES

    @pl.when(i < n_used_ref[0])
    def _():
        x = _unpack_bf16_pairs(_slab_load(x_ref, n // 2))
        g = jnp.dot(x, wg_ref[0], preferred_element_type=F32)
        u = jnp.dot(x, wu_ref[0], preferred_element_type=F32)
        a = (g * jax.nn.sigmoid(g) * u).astype(BF16)
        _slab_store(o_ref, jnp.dot(a, wd_ref[0], preferred_element_type=F32))

    @pl.when(i >= n_used_ref[0])
    def _():
        o_ref[...] = jnp.zeros(o_ref.shape, o_ref.dtype)


def moe_experts(blk_e, n_used, xs_slab, wg, wu, wd):
    _, D, De = wg.shape
    n = D // LANES
    P = xs_slab.shape[0] // (n // 2)
    tm = MOE_TM
    return pl.pallas_call(
        _moe_body,
        grid_spec=pltpu.PrefetchScalarGridSpec(
            num_scalar_prefetch=2,
            grid=(P // tm,),
            in_specs=[pl.BlockSpec((tm * n // 2, LANES), lambda i, be, nu: (i, 0)),
                      pl.BlockSpec((1, D, De), lambda i, be, nu: (be[i], 0, 0)),
                      pl.BlockSpec((1, D, De), lambda i, be, nu: (be[i], 0, 0)),
                      pl.BlockSpec((1, De, D), lambda i, be, nu: (be[i], 0, 0))],
            out_specs=pl.BlockSpec((tm * n, LANES), lambda i, be, nu: (i, 0))),
        out_shape=jax.ShapeDtypeStruct((P * n, LANES), F32),
        compiler_params=_params("arbitrary"),
        name="moe_experts",
    )(blk_e, n_used, xs_slab, wg, wu, wd)


def _combine_body(dest_ref, x1_ref, rec_ref, yb_ref, o_ref, *scratch, n_steps):
    bufs, sem = scratch[:TOP_K], scratch[TOP_K]
    tg, D = x1_ref.shape
    n = D // LANES
    group = 8
    i = pl.program_id(0)
    slot = i % 2

    def copies(step, r, to_slot):
        a = TOP_K * (step * tg + r)
        return [_row_copy(yb_ref, dest_ref[a + j], bufs[j].at[to_slot], r, n, sem) for j in range(TOP_K)]

    def issue(step, r, to_slot):
        for c in copies(step, r, to_slot):
            c.start()

    @pl.when(i == 0)
    def _():
        _for_rows(tg, lambda r: issue(0, r, 0))

    def drain(r):
        for c in copies(i, r, slot):
            c.wait()

    _for_rows(tg, drain)

    def combine_group(g):
        rows = pl.ds(pl.multiple_of(g * group, group), group)
        w = [jnp.broadcast_to(rec_ref[rows, ROUTE_W + j:ROUTE_W + j + 1], (group, LANES)) for j in range(TOP_K)]
        for s in range(n):
            cols = slice(s * LANES, (s + 1) * LANES)
            y = x1_ref[rows, cols]
            for j in range(TOP_K):
                y = y + w[j] * bufs[j][slot, pl.ds(g * (group * n) + s, group, stride=n), :]
            o_ref[rows, cols] = y

    def combine_and_prefetch(g, carry):
        for u in range(group):
            issue(i + 1, g * group + u, 1 - slot)
        combine_group(g)
        return carry

    def combine_only(g, carry):
        combine_group(g)
        return carry

    @pl.when(i + 1 < n_steps)
    def _():
        lax.fori_loop(0, tg // group, combine_and_prefetch, 0)

    @pl.when(i + 1 >= n_steps)
    def _():
        lax.fori_loop(0, tg // group, combine_only, 0)


def moe_combine(dest_flat, x1, rec, yb_slab):
    T, D = x1.shape
    tg = min(GATHER_ROWS, T)
    n = D // LANES
    return pl.pallas_call(
        functools.partial(_combine_body, n_steps=T // tg),
        grid_spec=pltpu.PrefetchScalarGridSpec(
            num_scalar_prefetch=1,
            grid=(T // tg,),
            in_specs=[pl.BlockSpec((tg, D), lambda i, dest: (i, 0)),
                      pl.BlockSpec((tg, ROUTER_PAD), lambda i, dest: (i, 0)),
                      pl.BlockSpec(memory_space=pl.ANY)],
            out_specs=pl.BlockSpec((tg, D), lambda i, dest: (i, 0)),
            scratch_shapes=[pltpu.VMEM((2, tg * n, LANES), F32) for _ in range(TOP_K)]
            + [pltpu.SemaphoreType.DMA(())]),
        out_shape=jax.ShapeDtypeStruct((T, D), F32),
        compiler_params=_params("arbitrary"),
        name="moe_combine",
    )(dest_flat, x1, rec, yb_slab)


def _block_layout(rec, counts_row):
    T = rec.shape[0]
    A = T * TOP_K
    P = (A + N_EXPERTS * (MOE_TM - 1) + MOE_TM - 1) // MOE_TM * MOE_TM
    n_blk = P // MOE_TM
    counts = counts_row[0, :N_EXPERTS].astype(jnp.int32)
    padded = (counts + MOE_TM - 1) // MOE_TM * MOE_TM
    pends = jnp.cumsum(padded)
    pstarts = pends - padded
    e = rec[:, ROUTE_E:ROUTE_E + TOP_K].astype(jnp.int32)
    rank = rec[:, ROUTE_RANK:ROUTE_RANK + TOP_K].astype(jnp.int32)
    one_hot = e[:, :, None] == jnp.arange(N_EXPERTS, dtype=jnp.int32)
    dest = jnp.sum(jnp.where(one_hot, pstarts, 0), axis=-1) + rank
    blk_start = jnp.arange(n_blk, dtype=jnp.int32) * MOE_TM
    blk_e = jnp.minimum(jnp.sum(blk_start[:, None] >= pends[None, :], axis=-1), N_EXPERTS - 1).astype(jnp.int32)
    n_used = (pends[-1:] // MOE_TM).astype(jnp.int32)
    return dest.reshape(A), blk_e, n_used, P


def _rope_tables(L):
    rows = L // GRID_W
    pairs = HEAD_DIM // 4
    row = jnp.repeat(jnp.arange(rows, dtype=F32), GRID_W)
    col = jnp.tile(jnp.arange(GRID_W, dtype=F32), rows)
    inv = jnp.power(ROPE_THETA, -jnp.arange(pairs, dtype=F32) / pairs)
    ang = jnp.concatenate([row[:, None] * inv, col[:, None] * inv], axis=-1)
    cos, sin = jnp.cos(ang), jnp.sin(ang)
    cosf = jnp.repeat(cos, 2, axis=-1)
    sinf = jnp.stack([-sin, sin], axis=-1).reshape(L, HEAD_DIM)
    return cosf, sinf


def _trunk(x, mem, lp, wb):
    B, L, D = x.shape
    T = B * L
    att_w = lp['w_br_attn'].shape[0]
    hy_w = lp['w_br_hyena'].shape[0]
    mem_w = lp['w_br_mem'].shape[0]
    kv_w = (lp['w_in'].shape[1] - att_w - 3 * hy_w - mem_w - 3 * D) // 2
    off_hy = att_w + 2 * kv_w
    off_mq = off_hy + 3 * hy_w
    off_gate = off_mq + mem_w
    x2 = x.reshape(T, D)

    z2 = normed_matmul(x2, lp['norm1_g'], wb['w_in'], BF16)
    z3 = z2.reshape(B, L, -1)
    cosf, sinf = _rope_tables(L)
    q, k2, v2, stats = qkv_prep(z3, cosf, sinf, lp['q_norm_g'], lp['k_norm_g'], att_w, kv_w)
    a_out = flash_attention(q, k2, v2, stats)
    h_out = hyena_branch(z3, lp, hy_w, off_hy)
    M = mem.shape[1]
    kv = normed_matmul(mem.reshape(B * M, D), lp['mem_norm_g'], wb['w_mem_kv'], F32).reshape(B, M, -1)
    m_out = mem_attention(z3, kv, lp['mq_norm_g'], lp['mk_norm_g'], mem_w, off_mq)
    merged = gated_merge(a_out.reshape(T, att_w), h_out.reshape(T, hy_w), m_out.reshape(T, mem_w),
                         wb['w_br_attn'], wb['w_br_hyena'], wb['w_br_mem'], z2, lp['b_gate'], off_gate)
    x1, h2, rec, counts = out_router(merged, x2, wb['w_out'], lp['norm2_g'], wb['wr2'], wb['br'])

    dest_flat, blk_e, n_used, P = _block_layout(rec, counts)
    xs = dispatch_rows(dest_flat, h2, D // (2 * LANES), P)
    yb = moe_experts(blk_e, n_used, xs, wb['w_gate_e'], wb['w_up_e'], wb['w_down_e'])
    y = moe_combine(dest_flat, x1, rec, yb)
    return y.reshape(B, L, D)


def kernel(x_prompt, x_sample, mem_prompt, mem_sample, norm1_g, w_in, b_gate, q_norm_g, k_norm_g, hy_conv_w, hy_conv_b, hf_w1, hf_b1, hf_w2, hf_b2, hf_w3, hf_b3, hf_freq, hy_decay, hy_bias, mem_norm_g, w_mem_kv, mq_norm_g, mk_norm_g, w_br_attn, w_br_hyena, w_br_mem, w_out, norm2_g, w_router_group, b_router_group, w_router_expert, b_router_expert, w_gate_e, w_up_e, w_down_e):
    params = dict(norm1_g=norm1_g, w_in=w_in, b_gate=b_gate, q_norm_g=q_norm_g, k_norm_g=k_norm_g,
                  hy_conv_w=hy_conv_w, hy_conv_b=hy_conv_b, hf_w1=hf_w1, hf_b1=hf_b1, hf_w2=hf_w2,
                  hf_b2=hf_b2, hf_w3=hf_w3, hf_b3=hf_b3, hf_freq=hf_freq, hy_decay=hy_decay,
                  hy_bias=hy_bias, mem_norm_g=mem_norm_g, w_mem_kv=w_mem_kv, mq_norm_g=mq_norm_g,
                  mk_norm_g=mk_norm_g, w_br_attn=w_br_attn, w_br_hyena=w_br_hyena, w_br_mem=w_br_mem,
                  w_out=w_out, norm2_g=norm2_g, w_router_group=w_router_group,
                  b_router_group=b_router_group, w_router_expert=w_router_expert,
                  b_router_expert=b_router_expert, w_gate_e=w_gate_e, w_up_e=w_up_e, w_down_e=w_down_e)
    depth = w_in.shape[0]
    xp, xs = x_prompt, x_sample
    for d in range(depth):
        lp = {name: arr[d] for name, arr in params.items()}
        D = lp['w_in'].shape[0]
        wb = {name: lp[name].astype(BF16) for name in
              ('w_in', 'w_mem_kv', 'w_br_attn', 'w_br_hyena', 'w_br_mem', 'w_out',
               'w_gate_e', 'w_up_e', 'w_down_e')}
        wr = jnp.concatenate([lp['w_router_group'], lp['w_router_expert'],
                              jnp.zeros((D, ROUTER_PAD - N_GROUPS - N_EXPERTS), F32)], axis=1)
        wr_hi = wr.astype(BF16)
        wb['wr2'] = jnp.concatenate([wr_hi, (wr - wr_hi.astype(F32)).astype(BF16)], axis=1)
        wb['br'] = jnp.concatenate([lp['b_router_group'], lp['b_router_expert'],
                                    jnp.zeros((ROUTER_PAD - N_GROUPS - N_EXPERTS,), F32)]).reshape(1, ROUTER_PAD)
        xp = _trunk(xp, mem_prompt, lp, wb)
        xs = _trunk(xs, mem_sample, lp, wb)
    return (xp, xs)
```
